```python
import math
import jax, jax.numpy as jnp
from jax import lax
import numpy as np

D_MODEL = 1024
BATCH = 2
SEQ = 8192
DEPTH = 1

HEAD_DIM = 64
FOX_HEADS = 8
DIFF_HEADS = 4
FOX_W = FOX_HEADS * HEAD_DIM
DIFF_QK_W = DIFF_HEADS * 2 * HEAD_DIM
DIFF_V_W = DIFF_HEADS * 2 * HEAD_DIM
N_BRANCH = 2
IN_COLS = 3 * FOX_W + FOX_HEADS + 2 * DIFF_QK_W + DIFF_V_W + N_BRANCH * D_MODEL
Q_BLOCK = 128
T5_BUCKETS = 32
T5_MAX_DIST = 128
N_EXPERTS = 32
TOP_K = 4
D_EXPERT = D_MODEL
SWIGLU_ALPHA = 1.702
SWIGLU_LIMIT = 7.0
EXPERT_BLOCK = 512
NORM_EPS = 1e-6
SUBLN_EPS = 1e-5

kernel_name = "hybrid_fox_diffattn_gptoss_moe_adaln"


def _rmsnorm(x, g, eps=NORM_EPS):
    xf = x.astype(jnp.float32)
    y = xf * lax.rsqrt(jnp.mean(xf * xf, axis=-1, keepdims=True) + eps)
    return (y * g.astype(jnp.float32)).astype(x.dtype)


def _t5_bucket(rel):
    n = jnp.maximum(rel, 0)
    max_exact = T5_BUCKETS // 2
    nf = jnp.maximum(n, max_exact).astype(jnp.float32)
    large = max_exact + (jnp.log(nf / max_exact) / math.log(T5_MAX_DIST / max_exact)
                         * (T5_BUCKETS - max_exact)).astype(jnp.int32)
    large = jnp.minimum(large, T5_BUCKETS - 1)
    return jnp.where(n < max_exact, n, large)


def _fox_attention(q, k, v, log_f):
    B, S, H, Dh = q.shape
    nb = S // Q_BLOCK
    F = jnp.cumsum(log_f, axis=1)
    Fk = F.transpose(0, 2, 1)
    qb = q.reshape(B, nb, Q_BLOCK, H, Dh).swapaxes(0, 1)
    Fb = F.reshape(B, nb, Q_BLOCK, H).swapaxes(0, 1)
    k_pos = jnp.arange(S)
    scale = Dh ** -0.5

    def block(args):
        qi, Fq, i = args
        q_pos = i * Q_BLOCK + jnp.arange(Q_BLOCK)
        s = jnp.einsum('bqhd,bkhd->bhqk', qi, k, preferred_element_type=jnp.float32) * scale
        s = s + (Fq.transpose(0, 2, 1)[..., None] - Fk[:, :, None, :])
        s = jnp.where(k_pos[None, :] <= q_pos[:, None], s, -jnp.inf)
        p = jax.nn.softmax(s, axis=-1)
        return jnp.einsum('bhqk,bkhd->bqhd', p.astype(v.dtype), v)

    out = lax.map(block, (qb, Fb, jnp.arange(nb)))
    return out.swapaxes(0, 1).reshape(B, S, H * Dh)


def _diff_attention(q1, q2, k1, k2, v, bias_table, lam):
    B, S, H, Dh = q1.shape
    nb = S // Q_BLOCK
    q1b = q1.reshape(B, nb, Q_BLOCK, H, Dh).swapaxes(0, 1)
    q2b = q2.reshape(B, nb, Q_BLOCK, H, Dh).swapaxes(0, 1)
    k_pos = jnp.arange(S)
    scale = Dh ** -0.5

    def block(args):
        q1i, q2i, i = args
        q_pos = i * Q_BLOCK + jnp.arange(Q_BLOCK)
        rel = q_pos[:, None] - k_pos[None, :]
        bias = bias_table[_t5_bucket(rel)].transpose(2, 0, 1)[None].astype(jnp.float32)
        causal = rel >= 0
        s1 = jnp.einsum('bqhd,bkhd->bhqk', q1i, k1, preferred_element_type=jnp.float32) * scale + bias
        s2 = jnp.einsum('bqhd,bkhd->bhqk', q2i, k2, preferred_element_type=jnp.float32) * scale + bias
        a1 = jax.nn.softmax(jnp.where(causal, s1, -jnp.inf), axis=-1)
        a2 = jax.nn.softmax(jnp.where(causal, s2, -jnp.inf), axis=-1)
        a = a1 - lam * a2
        return jnp.einsum('bhqk,bkhd->bqhd', a.astype(v.dtype), v)

    out = lax.map(block, (q1b, q2b, jnp.arange(nb)))
    return out.swapaxes(0, 1).reshape(B, S, H, 2 * Dh)


def _mixer(h, w_in, b_fgate, b_merge, lam_q1, lam_k1, lam_q2, lam_k2, diff_norm_g,
           t5_bias, w_br_fox, w_br_diff, w_o, lambda_init):
    B, S, _ = h.shape
    proj = h @ w_in
    sizes = [FOX_W, FOX_W, FOX_W, FOX_HEADS, DIFF_QK_W, DIFF_QK_W, DIFF_V_W]
    cuts = list(np.cumsum(sizes))
    fq, fk, fv, fl, dq, dk, dv, gl = jnp.split(proj, cuts, axis=-1)

    sh = (B, S, FOX_HEADS, HEAD_DIM)
    log_f = jax.nn.log_sigmoid(fl.astype(jnp.float32) + b_fgate.astype(jnp.float32))
    y_fox = _fox_attention(fq.reshape(sh), fk.reshape(sh), fv.reshape(sh), log_f)

    dq = dq.reshape(B, S, DIFF_HEADS, 2, HEAD_DIM)
    dk = dk.reshape(B, S, DIFF_HEADS, 2, HEAD_DIM)
    dv = dv.reshape(B, S, DIFF_HEADS, 2 * HEAD_DIM)
    lam = (jnp.exp(jnp.sum(lam_q1.astype(jnp.float32) * lam_k1.astype(jnp.float32)))
           - jnp.exp(jnp.sum(lam_q2.astype(jnp.float32) * lam_k2.astype(jnp.float32)))
           + lambda_init)
    yd = _diff_attention(dq[..., 0, :], dq[..., 1, :], dk[..., 0, :], dk[..., 1, :], dv, t5_bias, lam)
    yd = _rmsnorm(yd, diff_norm_g, SUBLN_EPS) * (1.0 - lambda_init)
    y_diff = yd.reshape(B, S, DIFF_V_W)

    gates = jax.nn.sigmoid(gl + b_merge)
    g_fox, g_diff = jnp.split(gates, N_BRANCH, axis=-1)
    merged = g_fox * (y_fox @ w_br_fox) + g_diff * (y_diff @ w_br_diff)
    return merged @ w_o


def _clamped_swiglu(u):
    x_glu = jnp.minimum(u[..., ::2], SWIGLU_LIMIT)
    x_lin = jnp.clip(u[..., 1::2], -SWIGLU_LIMIT, SWIGLU_LIMIT)
    return x_glu * jax.nn.sigmoid(SWIGLU_ALPHA * x_glu) * (x_lin + 1.0)


def _moe(h, w_router, b_router, w_up, b_up, w_down, b_down):
    N, D = h.shape
    logits = (h @ w_router + b_router).astype(jnp.float32)
    top_val, top_idx = lax.top_k(logits, TOP_K)
    gate = jax.nn.softmax(top_val, axis=-1)
    M = N * TOP_K
    flat_e = top_idx.reshape(M)
    flat_tok = jnp.arange(M, dtype=jnp.int32) // TOP_K
    flat_g = gate.reshape(M)
    order = jnp.argsort(flat_e)
    e_sorted = flat_e[order]
    counts = jnp.bincount(flat_e, length=N_EXPERTS)
    starts = jnp.cumsum(counts) - counts
    padded = ((counts + EXPERT_BLOCK - 1) // EXPERT_BLOCK) * EXPERT_BLOCK
    pends = jnp.cumsum(padded)
    pstarts = pends - padded
    dest = pstarts[e_sorted] + (jnp.arange(M) - starts[e_sorted])
    n_blocks = (M + EXPERT_BLOCK - 1) // EXPERT_BLOCK + N_EXPERTS
    P = n_blocks * EXPERT_BLOCK
    row_tok = jnp.zeros((P,), jnp.int32).at[dest].set(flat_tok[order])
    row_gate = jnp.zeros((P,), jnp.float32).at[dest].set(flat_g[order])
    block_e = jnp.minimum(jnp.searchsorted(pends, jnp.arange(n_blocks) * EXPERT_BLOCK, side='right'),
                          N_EXPERTS - 1).astype(jnp.int32)

    def block(args):
        tok, gw, e = args
        xb = h[tok]
        u = xb @ w_up[e] + b_up[e]
        y = _clamped_swiglu(u) @ w_down[e] + b_down[e]
        return y * gw[:, None].astype(y.dtype)

    yr = lax.map(block, (row_tok.reshape(n_blocks, EXPERT_BLOCK),
                         row_gate.reshape(n_blocks, EXPERT_BLOCK), block_e))
    return jnp.zeros((N, D), h.dtype).at[row_tok].add(yr.reshape(P, D))


def setup_inputs(seed: int = 0) -> dict:
    key = jax.random.key(seed)
    ks = jax.random.split(key, 26)
    f32 = jnp.float32
    nrm = lambda k, shape, s: (jax.random.normal(k, shape, f32) * s)
    D, L = D_MODEL, DEPTH
    return {
        "x": nrm(ks[0], (BATCH, SEQ, D), 1.0),
        "c": nrm(ks[1], (BATCH, D), 1.0),
        "w_ada": nrm(ks[2], (L, D, 6 * D), 0.5 * D ** -0.5),
        "b_ada": nrm(ks[3], (L, 6 * D), 0.02),
        "ln1_g": 1.0 + nrm(ks[4], (L, D), 0.02),
        "w_in": nrm(ks[5], (L, D, IN_COLS), D ** -0.5),
        "b_fgate": 2.0 + nrm(ks[6], (L, FOX_HEADS), 0.3),
        "b_merge": nrm(ks[7], (L, N_BRANCH * D), 0.02),
        "lam_q1": nrm(ks[8], (L, HEAD_DIM), 0.1),
        "lam_k1": nrm(ks[9], (L, HEAD_DIM), 0.1),
        "lam_q2": nrm(ks[10], (L, HEAD_DIM), 0.1),
        "lam_k2": nrm(ks[11], (L, HEAD_DIM), 0.1),
        "diff_norm_g": 1.0 + nrm(ks[12], (L, 2 * HEAD_DIM), 0.02),
        "t5_bias": nrm(ks[13], (T5_BUCKETS, DIFF_HEADS), 0.5),
        "w_br_fox": nrm(ks[14], (L, FOX_W, D), FOX_W ** -0.5),
        "w_br_diff": nrm(ks[15], (L, DIFF_V_W, D), DIFF_V_W ** -0.5),
        "w_o": nrm(ks[16], (L, D, D), D ** -0.5),
        "ln2_g": 1.0 + nrm(ks[17], (L, D), 0.02),
        "w_router": nrm(ks[18], (L, D, N_EXPERTS), D ** -0.5),
        "b_router": nrm(ks[19], (L, N_EXPERTS), 0.01),
        "w_up": nrm(ks[20], (L, N_EXPERTS, D, 2 * D_EXPERT), D ** -0.5),
        "b_up": nrm(ks[21], (L, N_EXPERTS, 2 * D_EXPERT), 0.01),
        "w_down": nrm(ks[22], (L, N_EXPERTS, D_EXPERT, D), D_EXPERT ** -0.5),
        "b_down": nrm(ks[23], (L, N_EXPERTS, D), 0.01),
        "final_g": 1.0 + nrm(ks[24], (D,), 0.02),
    }


def reference(x, c, w_ada, b_ada, ln1_g, w_in, b_fgate, b_merge, lam_q1, lam_k1, lam_q2, lam_k2,
              diff_norm_g, t5_bias, w_br_fox, w_br_diff, w_o, ln2_g, w_router, b_router,
              w_up, b_up, w_down, b_down, final_g):
    B, S, D = x.shape
    for l in range(DEPTH):
        lambda_init = 0.8 - 0.6 * math.exp(-0.3 * l)
        mod = jax.nn.silu(c) @ w_ada[l] + b_ada[l]
        sh1, sc1, g1, sh2, sc2, g2 = jnp.split(mod[:, None, :], 6, axis=-1)
        h = _rmsnorm(x, ln1_g[l]) * (1.0 + sc1) + sh1
        x = x + g1 * _mixer(h, w_in[l], b_fgate[l], b_merge[l], lam_q1[l], lam_k1[l], lam_q2[l],
                            lam_k2[l], diff_norm_g[l], t5_bias, w_br_fox[l], w_br_diff[l], w_o[l],
                            lambda_init)
        h = _rmsnorm(x, ln2_g[l]) * (1.0 + sc2) + sh2
        y = _moe(h.reshape(B * S, D), w_router[l], b_router[l], w_up[l], b_up[l], w_down[l], b_down[l])
        x = x + g2 * y.reshape(B, S, D)
    return _rmsnorm(x, final_g)
```

```python
import functools
import math

import numpy as np
import jax
import jax.numpy as jnp
from jax import lax
from jax.experimental import pallas as pl
from jax.experimental.pallas import tpu as pltpu

F32 = jnp.float32
BF16 = jnp.bfloat16
I32 = jnp.int32

HEAD_DIM = 64
FOX_HEADS = 8
DIFF_HEADS = 4
FOX_W = FOX_HEADS * HEAD_DIM
DIFF_W = DIFF_HEADS * 2 * HEAD_DIM
T5_BUCKETS = 32
T5_MAX_DIST = 128
N_EXPERTS = 32
TOP_K = 4
EXPERT_BLOCK = 512
SWIGLU_ALPHA = 1.702
SWIGLU_LIMIT = 7.0
NORM_EPS = 1e-6
SUBLN_EPS = 1e-5
LAMBDA_INIT = 0.8 - 0.6 * math.exp(-0.3 * 0)

LANES = 128
NEG = -1e30
VMEM_LIMIT = 56 * 1024 * 1024

HIGHEST = lax.Precision.HIGHEST
NT_DIMS = (((1,), (1,)), ((), ()))


def _dot(a, b, precision=None):
    return jnp.dot(a, b, preferred_element_type=F32, precision=precision)


def _dot_nt(a, b, precision=None):
    return lax.dot_general(a, b, NT_DIMS, preferred_element_type=F32, precision=precision)


def _params(*sem):
    return pltpu.CompilerParams(dimension_semantics=sem, vmem_limit_bytes=VMEM_LIMIT)


def _modulated_rmsnorm(x, g, scale, shift):
    y = x * lax.rsqrt(jnp.mean(x * x, axis=-1, keepdims=True) + NORM_EPS)
    return (y * g) * (1.0 + scale) + shift


def _split3(v):
    hi = v.astype(BF16)
    r1 = v - hi.astype(F32)
    mid = r1.astype(BF16)
    lo = (r1 - mid.astype(F32)).astype(BF16)
    return hi, mid, lo


def _adaln_kernel(c_ref, w_ref, b_ref, o_ref):
    c = c_ref[...]
    s = c / (1.0 + jnp.exp(-c))
    o_ref[...] = _dot(s, w_ref[...], HIGHEST) + b_ref[...]


def _adaln_mod(c, w, b):
    bsz, d = c.shape
    n = w.shape[1]
    rows = 8
    tn = 1536
    c8 = jnp.zeros((rows, d), F32).at[:bsz].set(c)
    out = pl.pallas_call(
        _adaln_kernel,
        grid=(n // tn,),
        in_specs=[pl.BlockSpec((rows, d), lambda j: (0, 0)),
                  pl.BlockSpec((d, tn), lambda j: (0, j)),
                  pl.BlockSpec((1, tn), lambda j: (0, j))],
        out_specs=pl.BlockSpec((rows, tn), lambda j: (0, j)),
        out_shape=jax.ShapeDtypeStruct((rows, n), F32),
        compiler_params=_params("arbitrary"),
        name="adaln_mod",
    )(c8, w, b.reshape(1, n))
    return out[:bsz]


def _bias_tile_kernel(tb_ref, o_ref, *, t):
    h = pl.program_id(0)
    key = lax.broadcasted_iota(I32, (t, t), 0)
    qry = lax.broadcasted_iota(I32, (t, t), 1)
    far = tb_ref[h, T5_BUCKETS - 1]
    max_exact = T5_BUCKETS // 2
    for which in range(2):
        rel = qry - key + which * t
        n = jnp.maximum(rel, 0)
        nf = jnp.maximum(n, max_exact).astype(F32)
        large = max_exact + (jnp.log(nf / max_exact) / math.log(T5_MAX_DIST / max_exact)
                             * (T5_BUCKETS - max_exact)).astype(I32)
        large = jnp.minimum(large, T5_BUCKETS - 1)
        bucket = jnp.where(n < max_exact, n, large)
        bias = jnp.zeros((t, t), F32)
        for b in range(T5_BUCKETS):
            bias = jnp.where(bucket == b, tb_ref[h, b] - far, bias)
        o_ref[0, which] = jnp.where(rel >= 0, bias, NEG)


def _bias_tiles(t5_bias, t):
    tb = t5_bias.T.astype(F32)
    return pl.pallas_call(
        functools.partial(_bias_tile_kernel, t=t),
        grid=(DIFF_HEADS,),
        in_specs=[pl.BlockSpec(memory_space=pltpu.SMEM)],
        out_specs=pl.BlockSpec((1, 2, t, t), lambda h: (h, 0, 0, 0)),
        out_shape=jax.ShapeDtypeStruct((DIFF_HEADS, 2, t, t), F32),
        compiler_params=_params("arbitrary"),
        name="bias_tiles",
    )(tb)


def _fparts_perm():
    npair = FOX_HEADS // 2
    pk = np.zeros((3 * LANES, npair * LANES), np.float32)
    pq = np.zeros((npair * 32, 3 * LANES), np.float32)
    k_ones = np.zeros((1, npair * LANES), np.float32)
    q_ones = np.zeros((npair * 32, 1), np.float32)
    for p in range(npair):
        for hh in range(2):
            head = 2 * p + hh
            for part in range(3):
                pk[part * LANES + head, p * LANES + hh * 16 + 3 + part] = -1.0
                k_ones[0, p * LANES + hh * 16 + part] = 1.0
                pq[p * 32 + hh * 16 + part, part * LANES + head] = 1.0
                q_ones[p * 32 + hh * 16 + 3 + part, 0] = 1.0
    return pk, pq, k_ones, q_ones


def _in_proj_kernel(x_ref, sc_ref, sh_ref, g_ref, wt_ref, wn_ref, wfl_ref, bf_ref,
                    pk_ref, pq_ref, kone_ref, qone_ref,
                    qtf_ref, fqt_ref, kpf_ref, vtf_ref, qtd_ref, kd_ref, vtd_ref,
                    carry_ref, *, tm):
    i = pl.program_id(1)

    @pl.when(i == 0)
    def _():
        carry_ref[...] = jnp.zeros_like(carry_ref)

    h = _modulated_rmsnorm(x_ref[0], g_ref[...], sc_ref[0], sh_ref[0])
    hb = h.astype(BF16)

    fl = _dot(h, wfl_ref[...], HIGHEST) + bf_ref[...]
    logf = jnp.minimum(fl, 0.0) - jnp.log(1.0 + jnp.exp(-jnp.abs(fl)))
    r = lax.broadcasted_iota(I32, (tm, tm), 0)
    c = lax.broadcasted_iota(I32, (tm, tm), 1)
    tri = jnp.where(c <= r, 1.0, 0.0).astype(F32)
    fsum = _dot(tri, logf, HIGHEST) + carry_ref[...]
    carry_ref[...] = fsum[tm - 1:tm, :]
    parts = jnp.concatenate(_split3(fsum), axis=1)

    scale = HEAD_DIM ** -0.5
    yt = _dot_nt(wt_ref[...], hb)
    yn = _dot(hb, wn_ref[...])

    qtf_ref[0] = (yt[0:FOX_W] * scale).astype(BF16)
    vt = yt[FOX_W:2 * FOX_W].astype(BF16)
    for hd in range(FOX_HEADS):
        vtf_ref[0, hd, 0] = vt[hd * HEAD_DIM:(hd + 1) * HEAD_DIM]
    qtd_ref[0] = (yt[2 * FOX_W:2 * FOX_W + DIFF_W] * scale).astype(BF16)
    vtd = yt[2 * FOX_W + DIFF_W:2 * FOX_W + 2 * DIFF_W].astype(BF16)
    for hd in range(DIFF_HEADS):
        vtd_ref[0, hd, 0] = vtd[hd * 2 * HEAD_DIM:(hd + 1) * 2 * HEAD_DIM]

    fqt_ref[0] = (_dot_nt(pq_ref[...], parts) + qone_ref[...]).astype(BF16)
    kext = (_dot(parts, pk_ref[...]) + kone_ref[...]).astype(BF16)
    kf = yn[:, 0:FOX_W].astype(BF16)
    for p in range(FOX_HEADS // 2):
        kpf_ref[0, :, 2 * p * LANES:(2 * p + 1) * LANES] = kf[:, p * LANES:(p + 1) * LANES]
        kpf_ref[0, :, (2 * p + 1) * LANES:(2 * p + 2) * LANES] = kext[:, p * LANES:(p + 1) * LANES]
    kd_ref[0] = yn[:, FOX_W:FOX_W + DIFF_W].astype(BF16)


def _in_proj(x, sc1, sh1, ln_g, w_in, b_fgate, t):
    bsz, s, d = x.shape
    nb = s // t
    cuts = np.cumsum([FOX_W, FOX_W, FOX_W, FOX_HEADS, DIFF_W, DIFF_W, DIFF_W])
    w_fq, w_fk, w_fv, w_fl, w_dq, w_dk, w_dv = [
        w_in[:, a:b] for a, b in zip([0, *cuts[:-1]], cuts)]
    wt = jnp.concatenate([w_fq, w_fv, w_dq, w_dv], axis=1).T.astype(BF16)
    wn = jnp.concatenate([w_fk, w_dk], axis=1).astype(BF16)
    wfl = jnp.zeros((d, LANES), F32).at[:, :FOX_HEADS].set(w_fl)
    bfl = jnp.zeros((1, LANES), F32).at[0, :FOX_HEADS].set(b_fgate)
    pk, pq, k_ones, q_ones = _fparts_perm()
    npair = FOX_HEADS // 2

    full = lambda shape: pl.BlockSpec(shape, lambda b, i: (0,) * len(shape))
    outs = pl.pallas_call(
        functools.partial(_in_proj_kernel, tm=t),
        grid=(bsz, nb),
        in_specs=[pl.BlockSpec((1, t, d), lambda b, i: (b, i, 0)),
                  pl.BlockSpec((1, 1, d), lambda b, i: (b, 0, 0)),
                  pl.BlockSpec((1, 1, d), lambda b, i: (b, 0, 0)),
                  full((1, d)), full(wt.shape), full(wn.shape), full(wfl.shape), full(bfl.shape),
                  full(pk.shape), full(pq.shape), full(k_ones.shape), full(q_ones.shape)],
        out_specs=[pl.BlockSpec((1, FOX_W, t), lambda b, i: (b, 0, i)),
                   pl.BlockSpec((1, npair * 32, t), lambda b, i: (b, 0, i)),
                   pl.BlockSpec((1, t, npair * 2 * LANES), lambda b, i: (b, i, 0)),
                   pl.BlockSpec((1, FOX_HEADS, 1, HEAD_DIM, t), lambda b, i: (b, 0, i, 0, 0)),
                   pl.BlockSpec((1, DIFF_W, t), lambda b, i: (b, 0, i)),
                   pl.BlockSpec((1, t, DIFF_W), lambda b, i: (b, i, 0)),
                   pl.BlockSpec((1, DIFF_HEADS, 1, 2 * HEAD_DIM, t), lambda b, i: (b, 0, i, 0, 0))],
        out_shape=[jax.ShapeDtypeStruct((bsz, FOX_W, s), BF16),
                   jax.ShapeDtypeStruct((bsz, npair * 32, s), BF16),
                   jax.ShapeDtypeStruct((bsz, s, npair * 2 * LANES), BF16),
                   jax.ShapeDtypeStruct((bsz, FOX_HEADS, nb, HEAD_DIM, t), BF16),
                   jax.ShapeDtypeStruct((bsz, DIFF_W, s), BF16),
                   jax.ShapeDtypeStruct((bsz, s, DIFF_W), BF16),
                   jax.ShapeDtypeStruct((bsz, DIFF_HEADS, nb, 2 * HEAD_DIM, t), BF16)],
        scratch_shapes=[pltpu.VMEM((1, LANES), F32)],
        compiler_params=_params("arbitrary", "arbitrary"),
        name="in_proj",
    )(x, sc1.reshape(bsz, 1, d), sh1.reshape(bsz, 1, d), ln_g.reshape(1, d), wt, wn, wfl, bfl,
      jnp.asarray(pk, BF16), jnp.asarray(pq, BF16), jnp.asarray(k_ones), jnp.asarray(q_ones))
    return outs


def _softmax_block(s, vt, m_ref, l_ref, acc_ref, slot):
    m_old = m_ref[slot]
    m_new = jnp.maximum(m_old, jnp.max(s, axis=0, keepdims=True))
    alpha = jnp.exp(m_old - m_new)
    p = jnp.exp(s - m_new)
    l_ref[slot] = alpha * l_ref[slot] + jnp.sum(p, axis=0, keepdims=True)
    acc_ref[slot] = alpha * acc_ref[slot] + _dot(vt, p.astype(BF16))
    m_ref[slot] = m_new


def _init_softmax_state(m_ref, l_ref, acc_ref):
    m_ref[...] = jnp.full(m_ref.shape, NEG, F32)
    l_ref[...] = jnp.zeros(l_ref.shape, F32)
    acc_ref[...] = jnp.zeros(acc_ref.shape, F32)


def _fox_kernel(q_ref, fq_ref, k_ref, v_ref, o_ref, qbd_ref, m_ref, l_ref, acc_ref, *, t):
    i = pl.program_id(2)
    d = HEAD_DIM
    qbd_ref[...] = jnp.zeros(qbd_ref.shape, BF16)
    qbd_ref[0:d, 0:t] = q_ref[0, 0:d, :]
    qbd_ref[d:2 * d, t:2 * t] = q_ref[0, d:2 * d, :]
    qbd_ref[2 * d:2 * d + 16, 0:t] = fq_ref[0, 0:16, :]
    qbd_ref[2 * d + 16:2 * d + 32, t:2 * t] = fq_ref[0, 16:32, :]
    _init_softmax_state(m_ref, l_ref, acc_ref)

    def block(j, masked):
        kblk = k_ref[0, pl.ds(pl.multiple_of(j * t, t), t), :]
        s = _dot(kblk, qbd_ref[...])
        for hh in range(2):
            sh = s[:, hh * t:(hh + 1) * t]
            if masked:
                key = lax.broadcasted_iota(I32, (t, t), 0)
                qry = lax.broadcasted_iota(I32, (t, t), 1)
                sh = jnp.where(key <= qry, sh, NEG)
            _softmax_block(sh, v_ref[0, hh, j], m_ref, l_ref, acc_ref, hh)

    def body(j, carry):
        block(j, False)
        return carry

    lax.fori_loop(0, i, body, 0)
    block(i, True)

    out_t = jnp.concatenate([acc_ref[0] / l_ref[0], acc_ref[1] / l_ref[1]], axis=0)
    o_ref[0] = out_t.T.astype(BF16)


def _fox_attention(qt, fqt, kp, vt, t):
    bsz, _, s = qt.shape
    nb = s // t
    npair = FOX_HEADS // 2
    return pl.pallas_call(
        functools.partial(_fox_kernel, t=t),
        grid=(bsz, npair, nb),
        in_specs=[pl.BlockSpec((1, 2 * HEAD_DIM, t), lambda b, p, i: (b, p, i)),
                  pl.BlockSpec((1, 32, t), lambda b, p, i: (b, p, i)),
                  pl.BlockSpec((1, s, 2 * LANES), lambda b, p, i: (b, 0, p)),
                  pl.BlockSpec((1, 2, nb, HEAD_DIM, t), lambda b, p, i: (b, p, 0, 0, 0))],
        out_specs=pl.BlockSpec((1, t, 2 * HEAD_DIM), lambda b, p, i: (b, i, p)),
        out_shape=jax.ShapeDtypeStruct((bsz, s, FOX_W), BF16),
        scratch_shapes=[pltpu.VMEM((2 * LANES, 2 * t), BF16),
                        pltpu.VMEM((2, 1, t), F32),
                        pltpu.VMEM((2, 1, t), F32),
                        pltpu.VMEM((2, HEAD_DIM, t), F32)],
        compiler_params=_params("arbitrary", "arbitrary", "arbitrary"),
        name="fox_attn",
    )(qt, fqt, kp, vt)


def _diff_kernel(lam_ref, q_ref, k_ref, v_ref, bias_ref, g_ref, o_ref,
                 qbd_ref, m_ref, l_ref, acc_ref, *, t):
    i = pl.program_id(2)
    d = HEAD_DIM
    qbd_ref[...] = jnp.zeros(qbd_ref.shape, BF16)
    qbd_ref[0:d, 0:t] = q_ref[0, 0:d, :]
    qbd_ref[d:2 * d, t:2 * t] = q_ref[0, d:2 * d, :]
    _init_softmax_state(m_ref, l_ref, acc_ref)

    def block(j, bias):
        kblk = k_ref[0, pl.ds(pl.multiple_of(j * t, t), t), :]
        s = _dot(kblk, qbd_ref[...])
        vt = v_ref[0, 0, j]
        for hh in range(2):
            sh = s[:, hh * t:(hh + 1) * t]
            if bias is not None:
                sh = sh + bias
            _softmax_block(sh, vt, m_ref, l_ref, acc_ref, hh)

    def body(j, carry):
        block(j, None)
        return carry

    lax.fori_loop(0, jnp.maximum(i - 1, 0), body, 0)

    @pl.when(i > 0)
    def _():
        block(i - 1, bias_ref[0, 1])

    block(i, bias_ref[0, 0])

    lam = (jnp.exp(jnp.sum(lam_ref[0:1, :] * lam_ref[1:2, :], axis=1, keepdims=True))
           - jnp.exp(jnp.sum(lam_ref[2:3, :] * lam_ref[3:4, :], axis=1, keepdims=True))
           + LAMBDA_INIT)
    out = acc_ref[0] / l_ref[0] - lam * (acc_ref[1] / l_ref[1])
    ms = jnp.mean(out * out, axis=0, keepdims=True)
    out = out * lax.rsqrt(ms + SUBLN_EPS) * g_ref[...] * (1.0 - LAMBDA_INIT)
    o_ref[0] = out.T.astype(BF16)


def _diff_attention(lam_vecs, qt, k, vt, bias_tiles, norm_g, t):
    bsz, _, s = qt.shape
    nb = s // t
    dv = 2 * HEAD_DIM
    return pl.pallas_call(
        functools.partial(_diff_kernel, t=t),
        grid=(bsz, DIFF_HEADS, nb),
        in_specs=[pl.BlockSpec((4, HEAD_DIM), lambda b, h, i: (0, 0)),
                  pl.BlockSpec((1, dv, t), lambda b, h, i: (b, h, i)),
                  pl.BlockSpec((1, s, dv), lambda b, h, i: (b, 0, h)),
                  pl.BlockSpec((1, 1, nb, dv, t), lambda b, h, i: (b, h, 0, 0, 0)),
                  pl.BlockSpec((1, 2, t, t), lambda b, h, i: (h, 0, 0, 0)),
                  pl.BlockSpec((dv, 1), lambda b, h, i: (0, 0))],
        out_specs=pl.BlockSpec((1, t, dv), lambda b, h, i: (b, i, h)),
        out_shape=jax.ShapeDtypeStruct((bsz, s, DIFF_W), BF16),
        scratch_shapes=[pltpu.VMEM((dv, 2 * t), BF16),
                        pltpu.VMEM((2, 1, t), F32),
                        pltpu.VMEM((2, 1, t), F32),
                        pltpu.VMEM((2, dv, t), F32)],
        compiler_params=_params("arbitrary", "arbitrary", "arbitrary"),
        name="diff_attn",
    )(lam_vecs, qt, k, vt, bias_tiles, norm_g.reshape(dv, 1))


def _post_attn_kernel(x_ref, yf_ref, yd_ref, sc1_ref, sh1_ref, g1_ref, sc2_ref, sh2_ref,
                      ln1_ref, ln2_ref, wgl_ref, bm_ref, wbf_ref, wbd_ref, wo_ref, wr_ref, br_ref,
                      x1_ref, h2_ref, idx_ref, rank_ref, gate_ref, cnt_ref, carry_ref, *, tm):
    i = pl.program_id(0)

    @pl.when(i == 0)
    def _():
        carry_ref[...] = jnp.zeros_like(carry_ref)

    d = x_ref.shape[1]
    x = x_ref[...]
    h1 = _modulated_rmsnorm(x, ln1_ref[...], sc1_ref[0], sh1_ref[0]).astype(BF16)
    gl = _dot(h1, wgl_ref[...]) + bm_ref[...]
    gates = 1.0 / (1.0 + jnp.exp(-gl))
    merged = (gates[:, :d] * _dot(yf_ref[...], wbf_ref[...])
              + gates[:, d:] * _dot(yd_ref[...], wbd_ref[...]))
    x1 = x + g1_ref[0] * _dot(merged.astype(BF16), wo_ref[...])
    x1_ref[...] = x1
    h2 = _modulated_rmsnorm(x1, ln2_ref[...], sc2_ref[0], sh2_ref[0])
    h2_ref[...] = h2

    logits = _dot_nt(wr_ref[...], h2, HIGHEST) + br_ref[...]
    row = lax.broadcasted_iota(I32, (N_EXPERTS, tm), 0)
    cur = logits
    vals, idxs = [], []
    for _ in range(TOP_K):
        mx = jnp.max(cur, axis=0, keepdims=True)
        ik = jnp.min(jnp.where(cur == mx, row, N_EXPERTS), axis=0, keepdims=True)
        vals.append(mx)
        idxs.append(ik)
        cur = jnp.where(row == ik, -jnp.inf, cur)
    exps = [jnp.exp(v - vals[0]) for v in vals]
    denom = exps[0] + exps[1] + exps[2] + exps[3]
    gate_rows = [e / denom for e in exps]

    onehots = [row == ik for ik in idxs]
    cnt = jnp.zeros((N_EXPERTS, tm), F32)
    for oh in onehots:
        cnt = cnt + jnp.where(oh, 1.0, 0.0)
    r = lax.broadcasted_iota(I32, (tm, tm), 0)
    c = lax.broadcasted_iota(I32, (tm, tm), 1)
    before = jnp.where(r < c, 1.0, 0.0).astype(BF16)
    prior = carry_ref[:, 0:1] + _dot(cnt.astype(BF16), before)
    for k in range(TOP_K):
        rk = jnp.sum(jnp.where(onehots[k], prior, 0.0), axis=0, keepdims=True)
        rank_ref[k:k + 1, :] = rk.astype(I32)
        idx_ref[k:k + 1, :] = idxs[k]
    carry_ref[...] = carry_ref[...] + jnp.sum(cnt, axis=1, keepdims=True)
    cnt_ref[...] = carry_ref[...]

    lrow = lax.broadcasted_iota(I32, (LANES, tm), 0)
    g_t = jnp.zeros((LANES, tm), F32)
    for k in range(TOP_K):
        g_t = jnp.where(lrow == k, gate_rows[k], g_t)
    gate_ref[...] = g_t.T


def _post_attn(x2, yf, yd, mods, ln1_g, ln2_g, b_merge, w_gl, w_brf, w_brd, w_o, w_r, b_r, s, tm):
    n, d = x2.shape
    nbb = s // tm
    bsz = n // s
    sc1, sh1, g1, sc2, sh2 = [m.reshape(bsz, 1, d) for m in mods]
    row_spec = lambda w: pl.BlockSpec((tm, w), lambda i: (i, 0))
    mod_spec = pl.BlockSpec((1, 1, d), lambda i: (i // nbb, 0, 0))
    full = lambda shape: pl.BlockSpec(shape, lambda i: (0,) * len(shape))
    tok_spec = pl.BlockSpec((TOP_K, tm), lambda i: (0, i))
    return pl.pallas_call(
        functools.partial(_post_attn_kernel, tm=tm),
        grid=(n // tm,),
        in_specs=[row_spec(d), row_spec(FOX_W), row_spec(DIFF_W),
                  mod_spec, mod_spec, mod_spec, mod_spec, mod_spec,
                  full((1, d)), full((1, d)), full(w_gl.shape), full((1, 2 * d)),
                  full(w_brf.shape), full(w_brd.shape), full(w_o.shape),
                  full((N_EXPERTS, d)), full((N_EXPERTS, 1))],
        out_specs=[row_spec(d), row_spec(d), tok_spec, tok_spec, row_spec(LANES),
                   full((N_EXPERTS, LANES))],
        out_shape=[jax.ShapeDtypeStruct((n, d), F32), jax.ShapeDtypeStruct((n, d), F32),
                   jax.ShapeDtypeStruct((TOP_K, n), I32), jax.ShapeDtypeStruct((TOP_K, n), I32),
                   jax.ShapeDtypeStruct((n, LANES), F32),
                   jax.ShapeDtypeStruct((N_EXPERTS, LANES), F32)],
        scratch_shapes=[pltpu.VMEM((N_EXPERTS, LANES), F32)],
        compiler_params=_params("arbitrary"),
        name="post_attn",
    )(x2, yf, yd, sc1, sh1, g1, sc2, sh2, ln1_g.reshape(1, d), ln2_g.reshape(1, d),
      w_gl.astype(BF16), b_merge.reshape(1, 2 * d), w_brf.astype(BF16), w_brd.astype(BF16),
      w_o.astype(BF16), w_r.T, b_r.reshape(N_EXPERTS, 1))


def _route_kernel(cnt_ref, idx_ref, rank_ref, dest_ref, be_ref, pend_ref, *, blk, nblk_lanes):
    shift = blk.bit_length() - 1
    idx = idx_ref[...]
    start_of = jnp.zeros(idx.shape, I32)
    blk_start = lax.broadcasted_iota(I32, (1, nblk_lanes), 1) * blk
    lane = lax.broadcasted_iota(I32, (1, LANES), 1)
    blk_e = jnp.zeros((1, nblk_lanes), I32)
    pends = jnp.zeros((1, LANES), I32)
    pend = jnp.int32(0)
    for e in range(N_EXPERTS):
        padded = ((cnt_ref[e] + (blk - 1)) >> shift) << shift
        start_of = jnp.where(idx == e, pend, start_of)
        pend = pend + padded
        blk_e = blk_e + jnp.where(blk_start >= pend, 1, 0)
        pends = jnp.where(lane == e, pend, pends)
    dest_ref[...] = start_of + rank_ref[...]
    be_ref[...] = jnp.minimum(blk_e, N_EXPERTS - 1)
    pend_ref[...] = pends


def _route(counts, idx, rank, blk):
    n = idx.shape[1]
    nblk = (n * TOP_K + blk - 1) // blk + N_EXPERTS
    nblk_lanes = -(-nblk // LANES) * LANES
    dest, blk_e, pends = pl.pallas_call(
        functools.partial(_route_kernel, blk=blk, nblk_lanes=nblk_lanes),
        in_specs=[pl.BlockSpec(memory_space=pltpu.SMEM),
                  pl.BlockSpec(memory_space=pltpu.VMEM),
                  pl.BlockSpec(memory_space=pltpu.VMEM)],
        out_specs=[pl.BlockSpec(memory_space=pltpu.VMEM)] * 3,
        out_shape=[jax.ShapeDtypeStruct((TOP_K, n), I32),
                   jax.ShapeDtypeStruct((1, nblk_lanes), I32),
                   jax.ShapeDtypeStruct((1, LANES), I32)],
        compiler_params=pltpu.CompilerParams(vmem_limit_bytes=VMEM_LIMIT),
        name="route",
    )(counts, idx, rank)
    return dest, blk_e[0, :nblk], pends[0, :N_EXPERTS], nblk


def _dispatch_kernel(pend_ref, dest_ref, h_ref, xs_ref, zero_ref, sem, *, tm, blk, nblk):
    shift = blk.bit_length() - 1

    @pl.when(pl.program_id(0) == 0)
    def _():
        zero_ref[...] = jnp.zeros(zero_ref.shape, F32)

        def zero_block(b):
            return pltpu.make_async_copy(
                zero_ref, xs_ref.at[pl.ds(pl.multiple_of(b * blk, blk), blk)], sem)

        def tails(fn):
            for e in range(N_EXPERTS):
                prev = pend_ref[e - 1] if e else 0

                @pl.when(pend_ref[e] > prev)
                def _():
                    fn(zero_block((pend_ref[e] >> shift) - 1))

        def rest(fn):
            def body(b, carry):
                fn(zero_block(b))
                return carry
            lax.fori_loop(pend_ref[N_EXPERTS - 1] >> shift, nblk, body, 0)

        tails(lambda cp: cp.start())
        rest(lambda cp: cp.start())
        tails(lambda cp: cp.wait())
        rest(lambda cp: cp.wait())

    def row_copy(t, k):
        return pltpu.make_async_copy(h_ref.at[pl.ds(t, 1)], xs_ref.at[pl.ds(dest_ref[k, t], 1)], sem)

    def start(t, carry):
        for k in range(TOP_K):
            row_copy(t, k).start()
        return carry

    def wait(t, carry):
        for k in range(TOP_K):
            row_copy(t, k).wait()
        return carry

    lax.fori_loop(0, tm, start, 0)
    lax.fori_loop(0, tm, wait, 0)


def _dispatch(pends, dest, h2, tm, blk, nblk):
    n, d = h2.shape
    return pl.pallas_call(
        functools.partial(_dispatch_kernel, tm=tm, blk=blk, nblk=nblk),
        grid=(n // tm,),
        in_specs=[pl.BlockSpec(memory_space=pltpu.SMEM),
                  pl.BlockSpec((TOP_K, tm), lambda i: (0, i), memory_space=pltpu.SMEM),
                  pl.BlockSpec((tm, d), lambda i: (i, 0))],
        out_specs=pl.BlockSpec(memory_space=pl.ANY),
        out_shape=jax.ShapeDtypeStruct((nblk * blk, d), F32),
        scratch_shapes=[pltpu.VMEM((blk, d), F32), pltpu.SemaphoreType.DMA(())],
        compiler_params=_params("arbitrary"),
        name="dispatch",
    )(pends, dest, h2)


def _expert_kernel(be_ref, pend_ref, x_ref, wg_ref, wl_ref, bg_ref, bl_ref, wd_ref, bd_ref, o_ref,
                   *, blk):
    live = pl.program_id(0) * blk < pend_ref[N_EXPERTS - 1]

    @pl.when(live)
    def _():
        x = x_ref[...].astype(BF16)
        u_glu = jnp.minimum(_dot(x, wg_ref[0]) + bg_ref[0], SWIGLU_LIMIT)
        u_lin = jnp.clip(_dot(x, wl_ref[0]) + bl_ref[0], -SWIGLU_LIMIT, SWIGLU_LIMIT)
        act = u_glu * (1.0 / (1.0 + jnp.exp(-SWIGLU_ALPHA * u_glu))) * (u_lin + 1.0)
        o_ref[...] = _dot(act.astype(BF16), wd_ref[0]) + bd_ref[0]

    @pl.when(jnp.logical_not(live))
    def _():
        o_ref[...] = jnp.zeros(o_ref.shape, F32)


def _experts(blk_e, pends, xs, w_up, b_up, w_down, b_down, blk):
    p, d = xs.shape
    f = w_down.shape[1]
    ne = w_up.shape[0]
    w_gl = w_up.reshape(ne, d, f, 2).transpose(0, 3, 1, 2).astype(BF16)
    b_gl = b_up.reshape(ne, 1, f, 2).transpose(0, 3, 1, 2)
    wg, wl, bg, bl = w_gl[:, 0], w_gl[:, 1], b_gl[:, 0], b_gl[:, 1]
    wd = w_down.astype(BF16)

    shift = blk.bit_length() - 1

    def live(b, be, pe):
        return jnp.minimum(b, (pe[N_EXPERTS - 1] >> shift) - 1)

    x_spec = pl.BlockSpec((blk, d), lambda b, be, pe: (live(b, be, pe), 0))
    w_spec = lambda r, c: pl.BlockSpec((1, r, c), lambda b, be, pe: (be[live(b, be, pe)], 0, 0))
    return pl.pallas_call(
        functools.partial(_expert_kernel, blk=blk),
        grid_spec=pltpu.PrefetchScalarGridSpec(
            num_scalar_prefetch=2,
            grid=(p // blk,),
            in_specs=[x_spec, w_spec(d, f), w_spec(d, f), w_spec(1, f), w_spec(1, f),
                      w_spec(f, d), w_spec(1, d)],
            out_specs=pl.BlockSpec((blk, d), lambda b, be, pe: (b, 0))),
        out_shape=jax.ShapeDtypeStruct((p, d), F32),
        compiler_params=_params("arbitrary"),
        name="experts",
    )(blk_e, pends, xs, wg, wl, bg, bl, wd, b_down.reshape(ne, 1, d))


def _combine_kernel(dest_ref, x1_ref, gate_ref, g2_ref, fg_ref, ys_ref, o_ref, ybuf_ref, sem, *, tm):
    def row_copy(t, k):
        return pltpu.make_async_copy(ys_ref.at[pl.ds(dest_ref[k, t], 1)],
                                     ybuf_ref.at[k, pl.ds(t, 1)], sem)

    def start(t, carry):
        for k in range(TOP_K):
            row_copy(t, k).start()
        return carry

    def wait(t, carry):
        for k in range(TOP_K):
            row_copy(t, k).wait()
        return carry

    lax.fori_loop(0, tm, start, 0)
    lax.fori_loop(0, tm, wait, 0)

    g = gate_ref[...]
    moe = g[:, 0:1] * ybuf_ref[0]
    for k in range(1, TOP_K):
        moe = moe + g[:, k:k + 1] * ybuf_ref[k]
    xo = x1_ref[...] + g2_ref[0] * moe
    o_ref[...] = xo * lax.rsqrt(jnp.mean(xo * xo, axis=-1, keepdims=True) + NORM_EPS) * fg_ref[...]


def _combine(dest, x1, gates, g2, final_g, ys, s, tm):
    n, d = x1.shape
    nbb = s // tm
    bsz = n // s
    return pl.pallas_call(
        functools.partial(_combine_kernel, tm=tm),
        grid=(n // tm,),
        in_specs=[pl.BlockSpec((TOP_K, tm), lambda i: (0, i), memory_space=pltpu.SMEM),
                  pl.BlockSpec((tm, d), lambda i: (i, 0)),
                  pl.BlockSpec((tm, LANES), lambda i: (i, 0)),
                  pl.BlockSpec((1, 1, d), lambda i: (i // nbb, 0, 0)),
                  pl.BlockSpec((1, d), lambda i: (0, 0)),
                  pl.BlockSpec(memory_space=pl.ANY)],
        out_specs=pl.BlockSpec((tm, d), lambda i: (i, 0)),
        out_shape=jax.ShapeDtypeStruct((n, d), F32),
        scratch_shapes=[pltpu.VMEM((TOP_K, tm, d), F32), pltpu.SemaphoreType.DMA(())],
        compiler_params=_params("arbitrary"),
        name="combine",
    )(dest, x1, gates, g2.reshape(bsz, 1, d), final_g.reshape(1, d), ys)


def kernel(x, c, w_ada, b_ada, ln1_g, w_in, b_fgate, b_merge, lam_q1, lam_k1, lam_q2, lam_k2,
           diff_norm_g, t5_bias, w_br_fox, w_br_diff, w_o, ln2_g, w_router, b_router,
           w_up, b_up, w_down, b_down, final_g):
    bsz, s, d = x.shape
    n = bsz * s
    t = min(512, s)
    tr = min(256, s)
    assert s % t == 0 and w_ada.shape[0] == 1

    mod = _adaln_mod(c, w_ada[0], b_ada[0])
    sh1, sc1, g1, sh2, sc2, g2 = jnp.split(mod, 6, axis=-1)

    qtf, fqt, kpf, vtf, qtd, kd, vtd = _in_proj(x, sc1, sh1, ln1_g[0], w_in[0], b_fgate[0], t)
    y_fox = _fox_attention(qtf, fqt, kpf, vtf, t)
    lam_vecs = jnp.stack([lam_q1[0], lam_k1[0], lam_q2[0], lam_k2[0]])
    y_diff = _diff_attention(lam_vecs, qtd, kd, vtd, _bias_tiles(t5_bias, t), diff_norm_g[0], t)

    w_gl = w_in[0][:, w_in.shape[2] - 2 * d:]
    x1, h2, idx, rank, gates, counts = _post_attn(
        x.reshape(n, d), y_fox.reshape(n, FOX_W), y_diff.reshape(n, DIFF_W),
        (sc1, sh1, g1, sc2, sh2), ln1_g[0], ln2_g[0], b_merge[0], w_gl,
        w_br_fox[0], w_br_diff[0], w_o[0], w_router[0], b_router[0], s, t)

    dest, blk_e, pends, nblk = _route(counts[:, 0].astype(I32), idx, rank, EXPERT_BLOCK)
    xs = _dispatch(pends, dest, h2, tr, EXPERT_BLOCK, nblk)
    ys = _experts(blk_e, pends, xs, w_up[0], b_up[0], w_down[0], b_down[0], EXPERT_BLOCK)
    out = _combine(dest, x1, gates, g2, final_g, ys, s, tr)
    return out.reshape(bsz, s, d)
```

```python
import functools
import math

import numpy as np
import jax
import jax.numpy as jnp
from jax import lax
from jax.experimental import pallas as pl
from jax.experimental.pallas import tpu as pltpu

F32 = jnp.float32
BF16 = jnp.bfloat16
I32 = jnp.int32

HEAD_DIM = 64
FOX_HEADS = 8
DIFF_HEADS = 4
FOX_W = FOX_HEADS * HEAD_DIM
DIFF_W = DIFF_HEADS * 2 * HEAD_DIM
T5_BUCKETS = 32
T5_MAX_DIST = 128
N_EXPERTS = 32
TOP_K = 4
EXPERT_BLOCK = 512
SWIGLU_ALPHA = 1.702
SWIGLU_LIMIT = 7.0
NORM_EPS = 1e-6
SUBLN_EPS = 1e-5
LAMBDA_INIT = 0.8 - 0.6 * math.exp(-0.3 * 0)

LANES = 128
NEG = -1e30
LOG2E = math.log2(math.e)
ONES_ROWS = 16
VMEM_LIMIT = 56 * 1024 * 1024

HIGHEST = lax.Precision.HIGHEST
NT_DIMS = (((1,), (1,)), ((), ()))


def _dot(a, b, precision=None):
    return jnp.dot(a, b, preferred_element_type=F32, precision=precision)


def _dot_nt(a, b, precision=None):
    return lax.dot_general(a, b, NT_DIMS, preferred_element_type=F32, precision=precision)


def _params(*sem):
    return pltpu.CompilerParams(dimension_semantics=sem, vmem_limit_bytes=VMEM_LIMIT)


def _modulated_rmsnorm(x, g, scale, shift):
    y = x * lax.rsqrt(jnp.mean(x * x, axis=-1, keepdims=True) + NORM_EPS)
    return (y * g) * (1.0 + scale) + shift


def _split3(v):
    hi = v.astype(BF16)
    r1 = v - hi.astype(F32)
    mid = r1.astype(BF16)
    lo = (r1 - mid.astype(F32)).astype(BF16)
    return hi, mid, lo


def _adaln_kernel(c_ref, w_ref, b_ref, o_ref):
    c = c_ref[...]
    s = c / (1.0 + jnp.exp(-c))
    o_ref[...] = _dot(s, w_ref[...], HIGHEST) + b_ref[...]


def _adaln_mod(c, w, b):
    bsz, d = c.shape
    n = w.shape[1]
    rows = 8
    tn = 1536
    c8 = jnp.zeros((rows, d), F32).at[:bsz].set(c)
    out = pl.pallas_call(
        _adaln_kernel,
        grid=(n // tn,),
        in_specs=[pl.BlockSpec((rows, d), lambda j: (0, 0)),
                  pl.BlockSpec((d, tn), lambda j: (0, j)),
                  pl.BlockSpec((1, tn), lambda j: (0, j))],
        out_specs=pl.BlockSpec((rows, tn), lambda j: (0, j)),
        out_shape=jax.ShapeDtypeStruct((rows, n), F32),
        compiler_params=_params("arbitrary"),
        name="adaln_mod",
    )(c8, w, b.reshape(1, n))
    return out[:bsz]


def _bias_tile_kernel(tb_ref, o_ref, *, t):
    h = pl.program_id(0)
    key = lax.broadcasted_iota(I32, (t, t), 0)
    qry = lax.broadcasted_iota(I32, (t, t), 1)
    far = tb_ref[h, T5_BUCKETS - 1]
    max_exact = T5_BUCKETS // 2
    for which in range(2):
        rel = qry - key + which * t
        n = jnp.maximum(rel, 0)
        nf = jnp.maximum(n, max_exact).astype(F32)
        large = max_exact + (jnp.log(nf / max_exact) / math.log(T5_MAX_DIST / max_exact)
                             * (T5_BUCKETS - max_exact)).astype(I32)
        large = jnp.minimum(large, T5_BUCKETS - 1)
        bucket = jnp.where(n < max_exact, n, large)
        bias = jnp.zeros((t, t), F32)
        for b in range(T5_BUCKETS):
            bias = jnp.where(bucket == b, tb_ref[h, b] - far, bias)
        o_ref[0, which] = jnp.where(rel >= 0, bias * LOG2E, NEG)


def _bias_tiles(t5_bias, t):
    tb = t5_bias.T.astype(F32)
    return pl.pallas_call(
        functools.partial(_bias_tile_kernel, t=t),
        grid=(DIFF_HEADS,),
        in_specs=[pl.BlockSpec(memory_space=pltpu.SMEM)],
        out_specs=pl.BlockSpec((1, 2, t, t), lambda h: (h, 0, 0, 0)),
        out_shape=jax.ShapeDtypeStruct((DIFF_HEADS, 2, t, t), F32),
        compiler_params=_params("arbitrary"),
        name="bias_tiles",
    )(tb)


def _fparts_perm():
    npair = FOX_HEADS // 2
    pk = np.zeros((3 * LANES, npair * LANES), np.float32)
    pq = np.zeros((npair * 32, 3 * LANES), np.float32)
    k_ones = np.zeros((1, npair * LANES), np.float32)
    q_ones = np.zeros((npair * 32, 1), np.float32)
    for p in range(npair):
        for hh in range(2):
            head = 2 * p + hh
            for part in range(3):
                pk[part * LANES + head, p * LANES + hh * 16 + 3 + part] = -1.0
                k_ones[0, p * LANES + hh * 16 + part] = 1.0
                pq[p * 32 + hh * 16 + part, part * LANES + head] = 1.0
                q_ones[p * 32 + hh * 16 + 3 + part, 0] = 1.0
    return pk, pq, k_ones, q_ones


def _in_proj_kernel(x_ref, sc_ref, sh_ref, g_ref, wt_ref, wn_ref, wfl_ref, bf_ref,
                    pk_ref, pq_ref, kone_ref, qone_ref,
                    qtf_ref, fqt_ref, kpf_ref, vtf_ref, qtd_ref, kd_ref, vtd_ref,
                    carry_ref, *, tm):
    i = pl.program_id(1)

    @pl.when(i == 0)
    def _():
        carry_ref[...] = jnp.zeros_like(carry_ref)

    h = _modulated_rmsnorm(x_ref[0], g_ref[...], sc_ref[0], sh_ref[0])
    hb = h.astype(BF16)

    fl = _dot(h, wfl_ref[...], HIGHEST) + bf_ref[...]
    logf = jnp.minimum(fl, 0.0) - jnp.log(1.0 + jnp.exp(-jnp.abs(fl)))
    r = lax.broadcasted_iota(I32, (tm, tm), 0)
    c = lax.broadcasted_iota(I32, (tm, tm), 1)
    tri = jnp.where(c <= r, 1.0, 0.0).astype(F32)
    fsum = _dot(tri, logf, HIGHEST) + carry_ref[...]
    carry_ref[...] = fsum[tm - 1:tm, :]
    parts = jnp.concatenate(_split3(fsum * LOG2E), axis=1)

    scale = HEAD_DIM ** -0.5 * LOG2E
    yt = _dot_nt(wt_ref[...], hb)
    yn = _dot(hb, wn_ref[...])

    ones_rows = jnp.where(lax.broadcasted_iota(I32, (ONES_ROWS, tm), 0) == 0, 1.0, 0.0).astype(BF16)
    qtf_ref[0] = (yt[0:FOX_W] * scale).astype(BF16)
    vt = yt[FOX_W:2 * FOX_W].astype(BF16)
    for hd in range(FOX_HEADS):
        vtf_ref[0, hd, 0, 0:HEAD_DIM] = vt[hd * HEAD_DIM:(hd + 1) * HEAD_DIM]
        vtf_ref[0, hd, 0, HEAD_DIM:HEAD_DIM + ONES_ROWS] = ones_rows
    qtd_ref[0] = (yt[2 * FOX_W:2 * FOX_W + DIFF_W] * scale).astype(BF16)
    vtd = yt[2 * FOX_W + DIFF_W:2 * FOX_W + 2 * DIFF_W].astype(BF16)
    dv = 2 * HEAD_DIM
    for hd in range(DIFF_HEADS):
        vtd_ref[0, hd, 0, 0:dv] = vtd[hd * dv:(hd + 1) * dv]
        vtd_ref[0, hd, 0, dv:dv + ONES_ROWS] = ones_rows

    fqt_ref[0] = (_dot_nt(pq_ref[...], parts) + qone_ref[...]).astype(BF16)
    kext = (_dot(parts, pk_ref[...]) + kone_ref[...]).astype(BF16)
    kf = yn[:, 0:FOX_W].astype(BF16)
    for p in range(FOX_HEADS // 2):
        kpf_ref[0, :, 2 * p * LANES:(2 * p + 1) * LANES] = kf[:, p * LANES:(p + 1) * LANES]
        kpf_ref[0, :, (2 * p + 1) * LANES:(2 * p + 2) * LANES] = kext[:, p * LANES:(p + 1) * LANES]
    kd_ref[0] = yn[:, FOX_W:FOX_W + DIFF_W].astype(BF16)


def _in_proj(x, sc1, sh1, ln_g, w_in, b_fgate, t):
    bsz, s, d = x.shape
    nb = s // t
    cuts = np.cumsum([FOX_W, FOX_W, FOX_W, FOX_HEADS, DIFF_W, DIFF_W, DIFF_W])
    w_fq, w_fk, w_fv, w_fl, w_dq, w_dk, w_dv = [
        w_in[:, a:b] for a, b in zip([0, *cuts[:-1]], cuts)]
    wt = jnp.concatenate([w_fq, w_fv, w_dq, w_dv], axis=1).T.astype(BF16)
    wn = jnp.concatenate([w_fk, w_dk], axis=1).astype(BF16)
    wfl = jnp.zeros((d, LANES), F32).at[:, :FOX_HEADS].set(w_fl)
    bfl = jnp.zeros((1, LANES), F32).at[0, :FOX_HEADS].set(b_fgate)
    pk, pq, k_ones, q_ones = _fparts_perm()
    npair = FOX_HEADS // 2

    full = lambda shape: pl.BlockSpec(shape, lambda b, i: (0,) * len(shape))
    outs = pl.pallas_call(
        functools.partial(_in_proj_kernel, tm=t),
        grid=(bsz, nb),
        in_specs=[pl.BlockSpec((1, t, d), lambda b, i: (b, i, 0)),
                  pl.BlockSpec((1, 1, d), lambda b, i: (b, 0, 0)),
                  pl.BlockSpec((1, 1, d), lambda b, i: (b, 0, 0)),
                  full((1, d)), full(wt.shape), full(wn.shape), full(wfl.shape), full(bfl.shape),
                  full(pk.shape), full(pq.shape), full(k_ones.shape), full(q_ones.shape)],
        out_specs=[pl.BlockSpec((1, FOX_W, t), lambda b, i: (b, 0, i)),
                   pl.BlockSpec((1, npair * 32, t), lambda b, i: (b, 0, i)),
                   pl.BlockSpec((1, t, npair * 2 * LANES), lambda b, i: (b, i, 0)),
                   pl.BlockSpec((1, FOX_HEADS, 1, HEAD_DIM + ONES_ROWS, t),
                                lambda b, i: (b, 0, i, 0, 0)),
                   pl.BlockSpec((1, DIFF_W, t), lambda b, i: (b, 0, i)),
                   pl.BlockSpec((1, t, DIFF_W), lambda b, i: (b, i, 0)),
                   pl.BlockSpec((1, DIFF_HEADS, 1, 2 * HEAD_DIM + ONES_ROWS, t),
                                lambda b, i: (b, 0, i, 0, 0))],
        out_shape=[jax.ShapeDtypeStruct((bsz, FOX_W, s), BF16),
                   jax.ShapeDtypeStruct((bsz, npair * 32, s), BF16),
                   jax.ShapeDtypeStruct((bsz, s, npair * 2 * LANES), BF16),
                   jax.ShapeDtypeStruct((bsz, FOX_HEADS, nb, HEAD_DIM + ONES_ROWS, t), BF16),
                   jax.ShapeDtypeStruct((bsz, DIFF_W, s), BF16),
                   jax.ShapeDtypeStruct((bsz, s, DIFF_W), BF16),
                   jax.ShapeDtypeStruct((bsz, DIFF_HEADS, nb, 2 * HEAD_DIM + ONES_ROWS, t), BF16)],
        scratch_shapes=[pltpu.VMEM((1, LANES), F32)],
        compiler_params=_params("arbitrary", "arbitrary"),
        name="in_proj",
    )(x, sc1.reshape(bsz, 1, d), sh1.reshape(bsz, 1, d), ln_g.reshape(1, d), wt, wn, wfl, bfl,
      jnp.asarray(pk, BF16), jnp.asarray(pq, BF16), jnp.asarray(k_ones), jnp.asarray(q_ones))
    return outs


def _softmax_block(s, vt, m_ref, acc_ref, slot):
    m_old = m_ref[slot]
    m_new = jnp.maximum(m_old, jnp.max(s, axis=0, keepdims=True))
    alpha = jnp.exp2(m_old - m_new)
    p = jnp.exp2(s - m_new)
    acc_ref[slot] = alpha * acc_ref[slot] + _dot(vt, p.astype(BF16))
    m_ref[slot] = m_new


def _init_softmax_state(m_ref, acc_ref):
    m_ref[...] = jnp.full(m_ref.shape, NEG, F32)
    acc_ref[...] = jnp.zeros(acc_ref.shape, F32)


def _scan_key_blocks(i, n_tail, qk_into, process, bufs):
    n_plain = jnp.maximum(i + 1 - n_tail, 0)
    n_pairs = n_plain >> 1
    rem = n_plain & 1
    qk_into(bufs[0], 0)

    def pair(n, carry):
        j = 2 * n
        qk_into(bufs[1], j + 1)
        process(bufs[0], j, None)
        qk_into(bufs[0], j + 2)
        process(bufs[1], j + 1, None)
        return carry

    lax.fori_loop(0, n_pairs, pair, 0)
    first = 2 * n_pairs

    def tail(kinds):
        for n, kind in enumerate(kinds):
            if n + 1 < len(kinds):
                qk_into(bufs[(n + 1) % 2], first + n + 1)
            process(bufs[n % 2], first + n, kind)

    for n_last in range(1, n_tail + 1):
        kinds = list(range(n_last - 1, -1, -1))
        if n_last < n_tail:
            pl.when(i + 1 == n_last)(functools.partial(tail, kinds))
        else:
            for r in range(2):
                pl.when((i + 1 >= n_tail) & (rem == r))(functools.partial(tail, [None] * r + kinds))


def _fox_kernel(q_ref, fq_ref, k_ref, v_ref, o_ref, qbd_ref, sa_ref, sb_ref, m_ref, acc_ref, *, t):
    i = pl.program_id(2)
    d = HEAD_DIM
    qbd_ref[...] = jnp.zeros(qbd_ref.shape, BF16)
    qbd_ref[0:d, 0:t] = q_ref[0, 0:d, :]
    qbd_ref[d:2 * d, t:2 * t] = q_ref[0, d:2 * d, :]
    qbd_ref[2 * d:2 * d + 16, 0:t] = fq_ref[0, 0:16, :]
    qbd_ref[2 * d + 16:2 * d + 32, t:2 * t] = fq_ref[0, 16:32, :]
    _init_softmax_state(m_ref, acc_ref)

    def qk_into(s_ref, j):
        kblk = k_ref[0, pl.ds(pl.multiple_of(j * t, t), t), :]
        s_ref[...] = _dot(kblk, qbd_ref[...])

    def process(s_ref, j, kind):
        for hh in range(2):
            sh = s_ref[:, hh * t:(hh + 1) * t]
            if kind == 0:
                key = lax.broadcasted_iota(I32, (t, t), 0)
                qry = lax.broadcasted_iota(I32, (t, t), 1)
                sh = jnp.where(key <= qry, sh, NEG)
            _softmax_block(sh, v_ref[0, hh, j], m_ref, acc_ref, hh)

    _scan_key_blocks(i, 1, qk_into, process, (sa_ref, sb_ref))

    outs = [acc_ref[hh, 0:d] / acc_ref[hh, d:d + 1] for hh in range(2)]
    o_ref[0] = jnp.concatenate(outs, axis=0).T.astype(BF16)


def _fox_attention(qt, fqt, kp, vt, t):
    bsz, _, s = qt.shape
    nb = s // t
    npair = FOX_HEADS // 2
    return pl.pallas_call(
        functools.partial(_fox_kernel, t=t),
        grid=(bsz, npair, nb),
        in_specs=[pl.BlockSpec((1, 2 * HEAD_DIM, t), lambda b, p, i: (b, p, i)),
                  pl.BlockSpec((1, 32, t), lambda b, p, i: (b, p, i)),
                  pl.BlockSpec((1, s, 2 * LANES), lambda b, p, i: (b, 0, p)),
                  pl.BlockSpec((1, 2, nb, HEAD_DIM + ONES_ROWS, t), lambda b, p, i: (b, p, 0, 0, 0))],
        out_specs=pl.BlockSpec((1, t, 2 * HEAD_DIM), lambda b, p, i: (b, i, p)),
        out_shape=jax.ShapeDtypeStruct((bsz, s, FOX_W), BF16),
        scratch_shapes=[pltpu.VMEM((2 * LANES, 2 * t), BF16),
                        pltpu.VMEM((t, 2 * t), F32),
                        pltpu.VMEM((t, 2 * t), F32),
                        pltpu.VMEM((2, 1, t), F32),
                        pltpu.VMEM((2, HEAD_DIM + ONES_ROWS, t), F32)],
        compiler_params=_params("arbitrary", "arbitrary", "arbitrary"),
        name="fox_attn",
    )(qt, fqt, kp, vt)


def _diff_kernel(lam_ref, q_ref, k_ref, v_ref, bias_ref, g_ref, o_ref,
                 qbd_ref, sa_ref, sb_ref, m_ref, acc_ref, *, t):
    i = pl.program_id(2)
    d = HEAD_DIM
    dv = 2 * HEAD_DIM
    qbd_ref[...] = jnp.zeros(qbd_ref.shape, BF16)
    qbd_ref[0:d, 0:t] = q_ref[0, 0:d, :]
    qbd_ref[d:2 * d, t:2 * t] = q_ref[0, d:2 * d, :]
    _init_softmax_state(m_ref, acc_ref)

    def qk_into(s_ref, j):
        kblk = k_ref[0, pl.ds(pl.multiple_of(j * t, t), t), :]
        s_ref[...] = _dot(kblk, qbd_ref[...])

    def process(s_ref, j, kind):
        vt = v_ref[0, 0, j]
        for hh in range(2):
            sh = s_ref[:, hh * t:(hh + 1) * t]
            if kind is not None:
                sh = sh + bias_ref[0, kind]
            _softmax_block(sh, vt, m_ref, acc_ref, hh)

    _scan_key_blocks(i, 2, qk_into, process, (sa_ref, sb_ref))

    lam = (jnp.exp(jnp.sum(lam_ref[0:1, :] * lam_ref[1:2, :], axis=1, keepdims=True))
           - jnp.exp(jnp.sum(lam_ref[2:3, :] * lam_ref[3:4, :], axis=1, keepdims=True))
           + LAMBDA_INIT)
    out = (acc_ref[0, 0:dv] / acc_ref[0, dv:dv + 1]
           - lam * (acc_ref[1, 0:dv] / acc_ref[1, dv:dv + 1]))
    ms = jnp.mean(out * out, axis=0, keepdims=True)
    out = out * lax.rsqrt(ms + SUBLN_EPS) * g_ref[...] * (1.0 - LAMBDA_INIT)
    o_ref[0] = out.T.astype(BF16)


def _diff_attention(lam_vecs, qt, k, vt, bias_tiles, norm_g, t):
    bsz, _, s = qt.shape
    nb = s // t
    dv = 2 * HEAD_DIM
    return pl.pallas_call(
        functools.partial(_diff_kernel, t=t),
        grid=(bsz, DIFF_HEADS, nb),
        in_specs=[pl.BlockSpec((4, HEAD_DIM), lambda b, h, i: (0, 0)),
                  pl.BlockSpec((1, dv, t), lambda b, h, i: (b, h, i)),
                  pl.BlockSpec((1, s, dv), lambda b, h, i: (b, 0, h)),
                  pl.BlockSpec((1, 1, nb, dv + ONES_ROWS, t), lambda b, h, i: (b, h, 0, 0, 0)),
                  pl.BlockSpec((1, 2, t, t), lambda b, h, i: (h, 0, 0, 0)),
                  pl.BlockSpec((dv, 1), lambda b, h, i: (0, 0))],
        out_specs=pl.BlockSpec((1, t, dv), lambda b, h, i: (b, i, h)),
        out_shape=jax.ShapeDtypeStruct((bsz, s, DIFF_W), BF16),
        scratch_shapes=[pltpu.VMEM((dv, 2 * t), BF16),
                        pltpu.VMEM((t, 2 * t), F32),
                        pltpu.VMEM((t, 2 * t), F32),
                        pltpu.VMEM((2, 1, t), F32),
                        pltpu.VMEM((2, dv + ONES_ROWS, t), F32)],
        compiler_params=_params("arbitrary", "arbitrary", "arbitrary"),
        name="diff_attn",
    )(lam_vecs, qt, k, vt, bias_tiles, norm_g.reshape(dv, 1))


def _post_attn_kernel(x_ref, yf_ref, yd_ref, sc1_ref, sh1_ref, g1_ref, sc2_ref, sh2_ref,
                      ln1_ref, ln2_ref, wgl_ref, bm_ref, wbf_ref, wbd_ref, wo_ref, wr_ref, br_ref,
                      x1_ref, h2_ref, idx_ref, rank_ref, gate_ref, cnt_ref, carry_ref, *, tm):
    i = pl.program_id(0)

    @pl.when(i == 0)
    def _():
        carry_ref[...] = jnp.zeros_like(carry_ref)

    d = x_ref.shape[1]
    x = x_ref[...]
    h1 = _modulated_rmsnorm(x, ln1_ref[...], sc1_ref[0], sh1_ref[0]).astype(BF16)
    gl = _dot(h1, wgl_ref[...]) + bm_ref[...]
    gates = 1.0 / (1.0 + jnp.exp(-gl))
    merged = (gates[:, :d] * _dot(yf_ref[...], wbf_ref[...])
              + gates[:, d:] * _dot(yd_ref[...], wbd_ref[...]))
    x1 = x + g1_ref[0] * _dot(merged.astype(BF16), wo_ref[...])
    x1_ref[...] = x1
    h2 = _modulated_rmsnorm(x1, ln2_ref[...], sc2_ref[0], sh2_ref[0])
    h2_ref[...] = h2

    logits = _dot_nt(wr_ref[...], h2, HIGHEST) + br_ref[...]
    row = lax.broadcasted_iota(I32, (N_EXPERTS, tm), 0)
    cur = logits
    vals, idxs = [], []
    for _ in range(TOP_K):
        mx = jnp.max(cur, axis=0, keepdims=True)
        ik = jnp.min(jnp.where(cur == mx, row, N_EXPERTS), axis=0, keepdims=True)
        vals.append(mx)
        idxs.append(ik)
        cur = jnp.where(row == ik, -jnp.inf, cur)
    exps = [jnp.exp(v - vals[0]) for v in vals]
    denom = exps[0] + exps[1] + exps[2] + exps[3]
    gate_rows = [e / denom for e in exps]

    onehots = [row == ik for ik in idxs]
    cnt = jnp.zeros((N_EXPERTS, tm), F32)
    for oh in onehots:
        cnt = cnt + jnp.where(oh, 1.0, 0.0)
    r = lax.broadcasted_iota(I32, (tm, tm), 0)
    c = lax.broadcasted_iota(I32, (tm, tm), 1)
    before = jnp.where(r < c, 1.0, 0.0).astype(BF16)
    prior = carry_ref[:, 0:1] + _dot(cnt.astype(BF16), before)
    for k in range(TOP_K):
        rk = jnp.sum(jnp.where(onehots[k], prior, 0.0), axis=0, keepdims=True)
        rank_ref[k:k + 1, :] = rk.astype(I32)
        idx_ref[k:k + 1, :] = idxs[k]
    carry_ref[...] = carry_ref[...] + jnp.sum(cnt, axis=1, keepdims=True)
    cnt_ref[...] = carry_ref[...]

    lrow = lax.broadcasted_iota(I32, (LANES, tm), 0)
    g_t = jnp.zeros((LANES, tm), F32)
    for k in range(TOP_K):
        g_t = jnp.where(lrow == k, gate_rows[k], g_t)
    gate_ref[...] = g_t.T


def _post_attn(x2, yf, yd, mods, ln1_g, ln2_g, b_merge, w_gl, w_brf, w_brd, w_o, w_r, b_r, s, tm):
    n, d = x2.shape
    nbb = s // tm
    bsz = n // s
    sc1, sh1, g1, sc2, sh2 = [m.reshape(bsz, 1, d) for m in mods]
    row_spec = lambda w: pl.BlockSpec((tm, w), lambda i: (i, 0))
    mod_spec = pl.BlockSpec((1, 1, d), lambda i: (i // nbb, 0, 0))
    full = lambda shape: pl.BlockSpec(shape, lambda i: (0,) * len(shape))
    tok_spec = pl.BlockSpec((TOP_K, tm), lambda i: (0, i))
    return pl.pallas_call(
        functools.partial(_post_attn_kernel, tm=tm),
        grid=(n // tm,),
        in_specs=[row_spec(d), row_spec(FOX_W), row_spec(DIFF_W),
                  mod_spec, mod_spec, mod_spec, mod_spec, mod_spec,
                  full((1, d)), full((1, d)), full(w_gl.shape), full((1, 2 * d)),
                  full(w_brf.shape), full(w_brd.shape), full(w_o.shape),
                  full((N_EXPERTS, d)), full((N_EXPERTS, 1))],
        out_specs=[row_spec(d), row_spec(d), tok_spec, tok_spec, row_spec(LANES),
                   full((N_EXPERTS, LANES))],
        out_shape=[jax.ShapeDtypeStruct((n, d), F32), jax.ShapeDtypeStruct((n, d), F32),
                   jax.ShapeDtypeStruct((TOP_K, n), I32), jax.ShapeDtypeStruct((TOP_K, n), I32),
                   jax.ShapeDtypeStruct((n, LANES), F32),
                   jax.ShapeDtypeStruct((N_EXPERTS, LANES), F32)],
        scratch_shapes=[pltpu.VMEM((N_EXPERTS, LANES), F32)],
        compiler_params=_params("arbitrary"),
        name="post_attn",
    )(x2, yf, yd, sc1, sh1, g1, sc2, sh2, ln1_g.reshape(1, d), ln2_g.reshape(1, d),
      w_gl.astype(BF16), b_merge.reshape(1, 2 * d), w_brf.astype(BF16), w_brd.astype(BF16),
      w_o.astype(BF16), w_r.T, b_r.reshape(N_EXPERTS, 1))


def _route_kernel(cnt_ref, idx_ref, rank_ref, dest_ref, be_ref, pend_ref, *, blk, nblk_lanes):
    shift = blk.bit_length() - 1
    idx = idx_ref[...]
    start_of = jnp.zeros(idx.shape, I32)
    blk_start = lax.broadcasted_iota(I32, (1, nblk_lanes), 1) * blk
    lane = lax.broadcasted_iota(I32, (1, LANES), 1)
    blk_e = jnp.zeros((1, nblk_lanes), I32)
    pends = jnp.zeros((1, LANES), I32)
    pend = jnp.int32(0)
    for e in range(N_EXPERTS):
        padded = ((cnt_ref[e] + (blk - 1)) >> shift) << shift
        start_of = jnp.where(idx == e, pend, start_of)
        pend = pend + padded
        blk_e = blk_e + jnp.where(blk_start >= pend, 1, 0)
        pends = jnp.where(lane == e, pend, pends)
    dest_ref[...] = start_of + rank_ref[...]
    be_ref[...] = jnp.minimum(blk_e, N_EXPERTS - 1)
    pend_ref[...] = pends


def _route(counts, idx, rank, blk):
    n = idx.shape[1]
    nblk = (n * TOP_K + blk - 1) // blk + N_EXPERTS
    nblk_lanes = -(-nblk // LANES) * LANES
    dest, blk_e, pends = pl.pallas_call(
        functools.partial(_route_kernel, blk=blk, nblk_lanes=nblk_lanes),
        in_specs=[pl.BlockSpec(memory_space=pltpu.SMEM),
                  pl.BlockSpec(memory_space=pltpu.VMEM),
                  pl.BlockSpec(memory_space=pltpu.VMEM)],
        out_specs=[pl.BlockSpec(memory_space=pltpu.VMEM)] * 3,
        out_shape=[jax.ShapeDtypeStruct((TOP_K, n), I32),
                   jax.ShapeDtypeStruct((1, nblk_lanes), I32),
                   jax.ShapeDtypeStruct((1, LANES), I32)],
        compiler_params=pltpu.CompilerParams(vmem_limit_bytes=VMEM_LIMIT),
        name="route",
    )(counts, idx, rank)
    return dest, blk_e[0, :nblk], pends[0, :N_EXPERTS], nblk


def _dispatch_kernel(pend_ref, dest_ref, h_ref, xs_ref, zero_ref, sem, *, tm, blk, nblk):
    shift = blk.bit_length() - 1

    @pl.when(pl.program_id(0) == 0)
    def _():
        zero_ref[...] = jnp.zeros(zero_ref.shape, F32)

        def zero_block(b):
            return pltpu.make_async_copy(
                zero_ref, xs_ref.at[pl.ds(pl.multiple_of(b * blk, blk), blk)], sem)

        def tails(fn):
            for e in range(N_EXPERTS):
                prev = pend_ref[e - 1] if e else 0

                @pl.when(pend_ref[e] > prev)
                def _():
                    fn(zero_block((pend_ref[e] >> shift) - 1))

        def rest(fn):
            def body(b, carry):
                fn(zero_block(b))
                return carry
            lax.fori_loop(pend_ref[N_EXPERTS - 1] >> shift, nblk, body, 0)

        tails(lambda cp: cp.start())
        rest(lambda cp: cp.start())
        tails(lambda cp: cp.wait())
        rest(lambda cp: cp.wait())

    def row_copy(t, k):
        return pltpu.make_async_copy(h_ref.at[pl.ds(t, 1)], xs_ref.at[pl.ds(dest_ref[k, t], 1)], sem)

    def start(t, carry):
        for k in range(TOP_K):
            row_copy(t, k).start()
        return carry

    def wait(t, carry):
        for k in range(TOP_K):
            row_copy(t, k).wait()
        return carry

    lax.fori_loop(0, tm, start, 0)
    lax.fori_loop(0, tm, wait, 0)


def _dispatch(pends, dest, h2, tm, blk, nblk):
    n, d = h2.shape
    return pl.pallas_call(
        functools.partial(_dispatch_kernel, tm=tm, blk=blk, nblk=nblk),
        grid=(n // tm,),
        in_specs=[pl.BlockSpec(memory_space=pltpu.SMEM),
                  pl.BlockSpec((TOP_K, tm), lambda i: (0, i), memory_space=pltpu.SMEM),
                  pl.BlockSpec((tm, d), lambda i: (i, 0))],
        out_specs=pl.BlockSpec(memory_space=pl.ANY),
        out_shape=jax.ShapeDtypeStruct((nblk * blk, d), F32),
        scratch_shapes=[pltpu.VMEM((blk, d), F32), pltpu.SemaphoreType.DMA(())],
        compiler_params=_params("arbitrary"),
        name="dispatch",
    )(pends, dest, h2)


def _expert_kernel(be_ref, pend_ref, x_ref, wg_ref, wl_ref, bg_ref, bl_ref, wd_ref, bd_ref, o_ref,
                   *, blk):
    live = pl.program_id(0) * blk < pend_ref[N_EXPERTS - 1]

    @pl.when(live)
    def _():
        x = x_ref[...].astype(BF16)
        u_glu = jnp.minimum(_dot(x, wg_ref[0]) + bg_ref[0], SWIGLU_LIMIT)
        u_lin = jnp.clip(_dot(x, wl_ref[0]) + bl_ref[0], -SWIGLU_LIMIT, SWIGLU_LIMIT)
        act = u_glu * (1.0 / (1.0 + jnp.exp(-SWIGLU_ALPHA * u_glu))) * (u_lin + 1.0)
        o_ref[...] = _dot(act.astype(BF16), wd_ref[0]) + bd_ref[0]

    @pl.when(jnp.logical_not(live))
    def _():
        o_ref[...] = jnp.zeros(o_ref.shape, F32)


def _experts(blk_e, pends, xs, w_up, b_up, w_down, b_down, blk):
    p, d = xs.shape
    f = w_down.shape[1]
    ne = w_up.shape[0]
    w_gl = w_up.reshape(ne, d, f, 2).transpose(0, 3, 1, 2).astype(BF16)
    b_gl = b_up.reshape(ne, 1, f, 2).transpose(0, 3, 1, 2)
    wg, wl, bg, bl = w_gl[:, 0], w_gl[:, 1], b_gl[:, 0], b_gl[:, 1]
    wd = w_down.astype(BF16)

    shift = blk.bit_length() - 1

    def live(b, be, pe):
        return jnp.minimum(b, (pe[N_EXPERTS - 1] >> shift) - 1)

    x_spec = pl.BlockSpec((blk, d), lambda b, be, pe: (live(b, be, pe), 0))
    w_spec = lambda r, c: pl.BlockSpec((1, r, c), lambda b, be, pe: (be[live(b, be, pe)], 0, 0))
    return pl.pallas_call(
        functools.partial(_expert_kernel, blk=blk),
        grid_spec=pltpu.PrefetchScalarGridSpec(
            num_scalar_prefetch=2,
            grid=(p // blk,),
            in_specs=[x_spec, w_spec(d, f), w_spec(d, f), w_spec(1, f), w_spec(1, f),
                      w_spec(f, d), w_spec(1, d)],
            out_specs=pl.BlockSpec((blk, d), lambda b, be, pe: (b, 0))),
        out_shape=jax.ShapeDtypeStruct((p, d), F32),
        compiler_params=_params("arbitrary"),
        name="experts",
    )(blk_e, pends, xs, wg, wl, bg, bl, wd, b_down.reshape(ne, 1, d))


def _combine_kernel(dest_ref, x1_ref, gate_ref, g2_ref, fg_ref, ys_ref, o_ref, ybuf_ref, sem, *, tm):
    def row_copy(t, k):
        return pltpu.make_async_copy(ys_ref.at[pl.ds(dest_ref[k, t], 1)],
                                     ybuf_ref.at[k, pl.ds(t, 1)], sem)

    def start(t, carry):
        for k in range(TOP_K):
            row_copy(t, k).start()
        return carry

    def wait(t, carry):
        for k in range(TOP_K):
            row_copy(t, k).wait()
        return carry

    lax.fori_loop(0, tm, start, 0)
    lax.fori_loop(0, tm, wait, 0)

    g = gate_ref[...]
    moe = g[:, 0:1] * ybuf_ref[0]
    for k in range(1, TOP_K):
        moe = moe + g[:, k:k + 1] * ybuf_ref[k]
    xo = x1_ref[...] + g2_ref[0] * moe
    o_ref[...] = xo * lax.rsqrt(jnp.mean(xo * xo, axis=-1, keepdims=True) + NORM_EPS) * fg_ref[...]


def _combine(dest, x1, gates, g2, final_g, ys, s, tm):
    n, d = x1.shape
    nbb = s // tm
    bsz = n // s
    return pl.pallas_call(
        functools.partial(_combine_kernel, tm=tm),
        grid=(n // tm,),
        in_specs=[pl.BlockSpec((TOP_K, tm), lambda i: (0, i), memory_space=pltpu.SMEM),
                  pl.BlockSpec((tm, d), lambda i: (i, 0)),
                  pl.BlockSpec((tm, LANES), lambda i: (i, 0)),
                  pl.BlockSpec((1, 1, d), lambda i: (i // nbb, 0, 0)),
                  pl.BlockSpec((1, d), lambda i: (0, 0)),
                  pl.BlockSpec(memory_space=pl.ANY)],
        out_specs=pl.BlockSpec((tm, d), lambda i: (i, 0)),
        out_shape=jax.ShapeDtypeStruct((n, d), F32),
        scratch_shapes=[pltpu.VMEM((TOP_K, tm, d), F32), pltpu.SemaphoreType.DMA(())],
        compiler_params=_params("arbitrary"),
        name="combine",
    )(dest, x1, gates, g2.reshape(bsz, 1, d), final_g.reshape(1, d), ys)


def kernel(x, c, w_ada, b_ada, ln1_g, w_in, b_fgate, b_merge, lam_q1, lam_k1, lam_q2, lam_k2,
           diff_norm_g, t5_bias, w_br_fox, w_br_diff, w_o, ln2_g, w_router, b_router,
           w_up, b_up, w_down, b_down, final_g):
    bsz, s, d = x.shape
    n = bsz * s
    t = min(512, s)
    tr = min(256, s)
    assert s % t == 0 and w_ada.shape[0] == 1

    mod = _adaln_mod(c, w_ada[0], b_ada[0])
    sh1, sc1, g1, sh2, sc2, g2 = jnp.split(mod, 6, axis=-1)

    qtf, fqt, kpf, vtf, qtd, kd, vtd = _in_proj(x, sc1, sh1, ln1_g[0], w_in[0], b_fgate[0], t)
    y_fox = _fox_attention(qtf, fqt, kpf, vtf, t)
    lam_vecs = jnp.stack([lam_q1[0], lam_k1[0], lam_q2[0], lam_k2[0]])
    y_diff = _diff_attention(lam_vecs, qtd, kd, vtd, _bias_tiles(t5_bias, t), diff_norm_g[0], t)

    w_gl = w_in[0][:, w_in.shape[2] - 2 * d:]
    x1, h2, idx, rank, gates, counts = _post_attn(
        x.reshape(n, d), y_fox.reshape(n, FOX_W), y_diff.reshape(n, DIFF_W),
        (sc1, sh1, g1, sc2, sh2), ln1_g[0], ln2_g[0], b_merge[0], w_gl,
        w_br_fox[0], w_br_diff[0], w_o[0], w_router[0], b_router[0], s, t)

    dest, blk_e, pends, nblk = _route(counts[:, 0].astype(I32), idx, rank, EXPERT_BLOCK)
    xs = _dispatch(pends, dest, h2, tr, EXPERT_BLOCK, nblk)
    ys = _experts(blk_e, pends, xs, w_up[0], b_up[0], w_down[0], b_down[0], EXPERT_BLOCK)
    out = _combine(dest, x1, gates, g2, final_g, ys, s, tr)
    return out.reshape(bsz, s, d)
```

```python
import functools
import math

import numpy as np
import jax
import jax.numpy as jnp
from jax import lax
from jax.experimental import pallas as pl
from jax.experimental.pallas import tpu as pltpu

F32 = jnp.float32
BF16 = jnp.bfloat16
I32 = jnp.int32

HEAD_DIM = 64
FOX_HEADS = 8
DIFF_HEADS = 4
FOX_W = FOX_HEADS * HEAD_DIM
DIFF_W = DIFF_HEADS * 2 * HEAD_DIM
T5_BUCKETS = 32
T5_MAX_DIST = 128
N_EXPERTS = 32
TOP_K = 4
EXPERT_BLOCK = 512
SWIGLU_ALPHA = 1.702
SWIGLU_LIMIT = 7.0
NORM_EPS = 1e-6
SUBLN_EPS = 1e-5
LAMBDA_INIT = 0.8 - 0.6 * math.exp(-0.3 * 0)

LANES = 128
NEG = -1e30
LOG2E = math.log2(math.e)
ONES_ROWS = 16
VMEM_LIMIT = 56 * 1024 * 1024

NT_DIMS = (((1,), (1,)), ((), ()))


def _dot(a, b):
    return jnp.dot(a, b, preferred_element_type=F32)


def _dot_nt(a, b):
    return lax.dot_general(a, b, NT_DIMS, preferred_element_type=F32)


def _split2(v):
    hi = v.astype(BF16)
    return hi, (v - hi.astype(F32)).astype(BF16)


def _dot_split(a, b, dot=_dot):
    a_hi, a_lo = _split2(a)
    b_hi, b_lo = _split2(b)
    return dot(a_hi, b_hi) + (dot(a_hi, b_lo) + dot(a_lo, b_hi))


def _params(*sem):
    return pltpu.CompilerParams(dimension_semantics=sem, vmem_limit_bytes=VMEM_LIMIT)


def _modulated_rmsnorm(x, g, scale, shift):
    y = x * lax.rsqrt(jnp.mean(x * x, axis=-1, keepdims=True) + NORM_EPS)
    return (y * g) * (1.0 + scale) + shift


def _split3(v):
    hi = v.astype(BF16)
    r1 = v - hi.astype(F32)
    mid = r1.astype(BF16)
    lo = (r1 - mid.astype(F32)).astype(BF16)
    return hi, mid, lo


def _adaln_kernel(c_ref, w_ref, b_ref, o_ref):
    c = c_ref[...]
    s = c / (1.0 + jnp.exp(-c))
    o_ref[...] = _dot_split(s, w_ref[...]) + b_ref[...]


def _adaln_mod(c, w, b):
    bsz, d = c.shape
    n = w.shape[1]
    rows = 8
    tn = 1536
    c8 = jnp.zeros((rows, d), F32).at[:bsz].set(c)
    out = pl.pallas_call(
        _adaln_kernel,
        grid=(n // tn,),
        in_specs=[pl.BlockSpec((rows, d), lambda j: (0, 0)),
                  pl.BlockSpec((d, tn), lambda j: (0, j)),
                  pl.BlockSpec((1, tn), lambda j: (0, j))],
        out_specs=pl.BlockSpec((rows, tn), lambda j: (0, j)),
        out_shape=jax.ShapeDtypeStruct((rows, n), F32),
        compiler_params=_params("arbitrary"),
        name="adaln_mod",
    )(c8, w, b.reshape(1, n))
    return out[:bsz]


def _bias_tile_kernel(tb_ref, o_ref, *, t):
    h = pl.program_id(0)
    key = lax.broadcasted_iota(I32, (t, t), 0)
    qry = lax.broadcasted_iota(I32, (t, t), 1)
    far = tb_ref[h, T5_BUCKETS - 1]
    max_exact = T5_BUCKETS // 2
    for which in range(2):
        rel = qry - key + which * t
        n = jnp.maximum(rel, 0)
        nf = jnp.maximum(n, max_exact).astype(F32)
        large = max_exact + (jnp.log(nf / max_exact) / math.log(T5_MAX_DIST / max_exact)
                             * (T5_BUCKETS - max_exact)).astype(I32)
        large = jnp.minimum(large, T5_BUCKETS - 1)
        bucket = jnp.where(n < max_exact, n, large)
        bias = jnp.zeros((t, t), F32)
        for b in range(T5_BUCKETS):
            bias = jnp.where(bucket == b, tb_ref[h, b] - far, bias)
        o_ref[0, which] = jnp.where(rel >= 0, bias * LOG2E, NEG)


def _bias_tiles(t5_bias, t):
    tb = t5_bias.T.astype(F32)
    return pl.pallas_call(
        functools.partial(_bias_tile_kernel, t=t),
        grid=(DIFF_HEADS,),
        in_specs=[pl.BlockSpec(memory_space=pltpu.SMEM)],
        out_specs=pl.BlockSpec((1, 2, t, t), lambda h: (h, 0, 0, 0)),
        out_shape=jax.ShapeDtypeStruct((DIFF_HEADS, 2, t, t), F32),
        compiler_params=_params("arbitrary"),
        name="bias_tiles",
    )(tb)


def _fparts_perm():
    npair = FOX_HEADS // 2
    pk = np.zeros((3 * LANES, npair * LANES), np.float32)
    pq = np.zeros((npair * 32, 3 * LANES), np.float32)
    k_ones = np.zeros((1, npair * LANES), np.float32)
    q_ones = np.zeros((npair * 32, 1), np.float32)
    for p in range(npair):
        for hh in range(2):
            head = 2 * p + hh
            for part in range(3):
                pk[part * LANES + head, p * LANES + hh * 16 + 3 + part] = -1.0
                k_ones[0, p * LANES + hh * 16 + part] = 1.0
                pq[p * 32 + hh * 16 + part, part * LANES + head] = 1.0
                q_ones[p * 32 + hh * 16 + 3 + part, 0] = 1.0
    return pk, pq, k_ones, q_ones


def _in_proj_kernel(x_ref, sc_ref, sh_ref, g_ref, wt_ref, wn_ref, wfl_ref, bf_ref,
                    pk_ref, pq_ref, kone_ref, qone_ref,
                    qtf_ref, fqt_ref, kpf_ref, vtf_ref, qtd_ref, kd_ref, vtd_ref,
                    carry_ref, *, tm):
    i = pl.program_id(1)

    @pl.when(i == 0)
    def _():
        carry_ref[...] = jnp.zeros_like(carry_ref)

    h = _modulated_rmsnorm(x_ref[0], g_ref[...], sc_ref[0], sh_ref[0])
    hb = h.astype(BF16)

    h_lo = (h - hb.astype(F32)).astype(BF16)
    wfl_hi, wfl_lo = _split2(wfl_ref[...])
    fl = _dot(hb, wfl_hi) + (_dot(hb, wfl_lo) + _dot(h_lo, wfl_hi)) + bf_ref[...]
    logf = jnp.minimum(fl, 0.0) - jnp.log(1.0 + jnp.exp(-jnp.abs(fl)))
    r = lax.broadcasted_iota(I32, (tm, tm), 0)
    c = lax.broadcasted_iota(I32, (tm, tm), 1)
    tri = jnp.where(c <= r, 1.0, 0.0).astype(BF16)
    psum = _dot(tri, jnp.concatenate(_split3(logf), axis=1))
    fsum = (psum[:, 0:LANES] + (psum[:, LANES:2 * LANES] + psum[:, 2 * LANES:3 * LANES])
            + carry_ref[...])
    carry_ref[...] = fsum[tm - 1:tm, :]
    parts = jnp.concatenate(_split3(fsum * LOG2E), axis=1)

    scale = HEAD_DIM ** -0.5 * LOG2E
    yt = _dot_nt(wt_ref[...], hb)
    yn = _dot(hb, wn_ref[...])

    ones_rows = jnp.where(lax.broadcasted_iota(I32, (ONES_ROWS, tm), 0) == 0, 1.0, 0.0).astype(BF16)
    qtf_ref[0] = (yt[0:FOX_W] * scale).astype(BF16)
    vt = yt[FOX_W:2 * FOX_W].astype(BF16)
    for hd in range(FOX_HEADS):
        vtf_ref[0, hd, 0, 0:HEAD_DIM] = vt[hd * HEAD_DIM:(hd + 1) * HEAD_DIM]
        vtf_ref[0, hd, 0, HEAD_DIM:HEAD_DIM + ONES_ROWS] = ones_rows
    qtd_ref[0] = (yt[2 * FOX_W:2 * FOX_W + DIFF_W] * scale).astype(BF16)
    vtd = yt[2 * FOX_W + DIFF_W:2 * FOX_W + 2 * DIFF_W].astype(BF16)
    dv = 2 * HEAD_DIM
    for hd in range(DIFF_HEADS):
        vtd_ref[0, hd, 0, 0:dv] = vtd[hd * dv:(hd + 1) * dv]
        vtd_ref[0, hd, 0, dv:dv + ONES_ROWS] = ones_rows

    fqt_ref[0] = (_dot_nt(pq_ref[...], parts) + qone_ref[...]).astype(BF16)
    kext = (_dot(parts, pk_ref[...]) + kone_ref[...]).astype(BF16)
    kf = yn[:, 0:FOX_W].astype(BF16)
    for p in range(FOX_HEADS // 2):
        kpf_ref[0, :, 2 * p * LANES:(2 * p + 1) * LANES] = kf[:, p * LANES:(p + 1) * LANES]
        kpf_ref[0, :, (2 * p + 1) * LANES:(2 * p + 2) * LANES] = kext[:, p * LANES:(p + 1) * LANES]
    kd_ref[0] = yn[:, FOX_W:FOX_W + DIFF_W].astype(BF16)


def _in_proj(x, sc1, sh1, ln_g, w_in, b_fgate, t):
    bsz, s, d = x.shape
    nb = s // t
    cuts = np.cumsum([FOX_W, FOX_W, FOX_W, FOX_HEADS, DIFF_W, DIFF_W, DIFF_W])
    w_fq, w_fk, w_fv, w_fl, w_dq, w_dk, w_dv = [
        w_in[:, a:b] for a, b in zip([0, *cuts[:-1]], cuts)]
    wt = jnp.concatenate([w_fq, w_fv, w_dq, w_dv], axis=1).T.astype(BF16)
    wn = jnp.concatenate([w_fk, w_dk], axis=1).astype(BF16)
    wfl = jnp.zeros((d, LANES), F32).at[:, :FOX_HEADS].set(w_fl)
    bfl = jnp.zeros((1, LANES), F32).at[0, :FOX_HEADS].set(b_fgate)
    pk, pq, k_ones, q_ones = _fparts_perm()
    npair = FOX_HEADS // 2

    full = lambda shape: pl.BlockSpec(shape, lambda b, i: (0,) * len(shape))
    outs = pl.pallas_call(
        functools.partial(_in_proj_kernel, tm=t),
        grid=(bsz, nb),
        in_specs=[pl.BlockSpec((1, t, d), lambda b, i: (b, i, 0)),
                  pl.BlockSpec((1, 1, d), lambda b, i: (b, 0, 0)),
                  pl.BlockSpec((1, 1, d), lambda b, i: (b, 0, 0)),
                  full((1, d)), full(wt.shape), full(wn.shape), full(wfl.shape), full(bfl.shape),
                  full(pk.shape), full(pq.shape), full(k_ones.shape), full(q_ones.shape)],
        out_specs=[pl.BlockSpec((1, FOX_W, t), lambda b, i: (b, 0, i)),
                   pl.BlockSpec((1, npair * 32, t), lambda b, i: (b, 0, i)),
                   pl.BlockSpec((1, t, npair * 2 * LANES), lambda b, i: (b, i, 0)),
                   pl.BlockSpec((1, FOX_HEADS, 1, HEAD_DIM + ONES_ROWS, t),
                                lambda b, i: (b, 0, i, 0, 0)),
                   pl.BlockSpec((1, DIFF_W, t), lambda b, i: (b, 0, i)),
                   pl.BlockSpec((1, t, DIFF_W), lambda b, i: (b, i, 0)),
                   pl.BlockSpec((1, DIFF_HEADS, 1, 2 * HEAD_DIM + ONES_ROWS, t),
                                lambda b, i: (b, 0, i, 0, 0))],
        out_shape=[jax.ShapeDtypeStruct((bsz, FOX_W, s), BF16),
                   jax.ShapeDtypeStruct((bsz, npair * 32, s), BF16),
                   jax.ShapeDtypeStruct((bsz, s, npair * 2 * LANES), BF16),
                   jax.ShapeDtypeStruct((bsz, FOX_HEADS, nb, HEAD_DIM + ONES_ROWS, t), BF16),
                   jax.ShapeDtypeStruct((bsz, DIFF_W, s), BF16),
                   jax.ShapeDtypeStruct((bsz, s, DIFF_W), BF16),
                   jax.ShapeDtypeStruct((bsz, DIFF_HEADS, nb, 2 * HEAD_DIM + ONES_ROWS, t), BF16)],
        scratch_shapes=[pltpu.VMEM((1, LANES), F32)],
        compiler_params=_params("arbitrary", "arbitrary"),
        name="in_proj",
    )(x, sc1.reshape(bsz, 1, d), sh1.reshape(bsz, 1, d), ln_g.reshape(1, d), wt, wn, wfl, bfl,
      jnp.asarray(pk, BF16), jnp.asarray(pq, BF16), jnp.asarray(k_ones), jnp.asarray(q_ones))
    return outs


def _softmax_block(s, vt, m_ref, acc_ref, slot):
    m_old = m_ref[slot]
    m_new = jnp.maximum(m_old, jnp.max(s, axis=0, keepdims=True))
    alpha = jnp.exp2(m_old - m_new)
    p = jnp.exp2(s - m_new)
    acc_ref[slot] = alpha * acc_ref[slot] + _dot(vt, p.astype(BF16))
    m_ref[slot] = m_new


def _init_softmax_state(m_ref, acc_ref):
    m_ref[...] = jnp.full(m_ref.shape, NEG, F32)
    acc_ref[...] = jnp.zeros(acc_ref.shape, F32)


def _scan_key_blocks(i, n_tail, qk_into, process, bufs):
    n_plain = jnp.maximum(i + 1 - n_tail, 0)
    n_pairs = n_plain >> 1
    rem = n_plain & 1
    qk_into(bufs[0], 0)

    def pair(n, carry):
        j = 2 * n
        qk_into(bufs[1], j + 1)
        process(bufs[0], j, None)
        qk_into(bufs[0], j + 2)
        process(bufs[1], j + 1, None)
        return carry

    lax.fori_loop(0, n_pairs, pair, 0)
    first = 2 * n_pairs

    def tail(kinds):
        for n, kind in enumerate(kinds):
            if n + 1 < len(kinds):
                qk_into(bufs[(n + 1) % 2], first + n + 1)
            process(bufs[n % 2], first + n, kind)

    for n_last in range(1, n_tail + 1):
        kinds = list(range(n_last - 1, -1, -1))
        if n_last < n_tail:
            pl.when(i + 1 == n_last)(functools.partial(tail, kinds))
        else:
            for r in range(2):
                pl.when((i + 1 >= n_tail) & (rem == r))(functools.partial(tail, [None] * r + kinds))


def _fox_kernel(q_ref, fq_ref, k_ref, v_ref, o_ref, qbd_ref, sa_ref, sb_ref, m_ref, acc_ref, *, t):
    i = pl.program_id(2)
    d = HEAD_DIM
    qbd_ref[...] = jnp.zeros(qbd_ref.shape, BF16)
    qbd_ref[0:d, 0:t] = q_ref[0, 0:d, :]
    qbd_ref[d:2 * d, t:2 * t] = q_ref[0, d:2 * d, :]
    qbd_ref[2 * d:2 * d + 16, 0:t] = fq_ref[0, 0:16, :]
    qbd_ref[2 * d + 16:2 * d + 32, t:2 * t] = fq_ref[0, 16:32, :]
    _init_softmax_state(m_ref, acc_ref)

    def qk_into(s_ref, j):
        kblk = k_ref[0, pl.ds(pl.multiple_of(j * t, t), t), :]
        s_ref[...] = _dot(kblk, qbd_ref[...])

    def process(s_ref, j, kind):
        for hh in range(2):
            sh = s_ref[:, hh * t:(hh + 1) * t]
            if kind == 0:
                key = lax.broadcasted_iota(I32, (t, t), 0)
                qry = lax.broadcasted_iota(I32, (t, t), 1)
                sh = jnp.where(key <= qry, sh, NEG)
            _softmax_block(sh, v_ref[0, hh, j], m_ref, acc_ref, hh)

    _scan_key_blocks(i, 1, qk_into, process, (sa_ref, sb_ref))

    outs = [acc_ref[hh, 0:d] / acc_ref[hh, d:d + 1] for hh in range(2)]
    o_ref[0] = jnp.concatenate(outs, axis=0).T.astype(BF16)


def _fox_attention(qt, fqt, kp, vt, t):
    bsz, _, s = qt.shape
    nb = s // t
    npair = FOX_HEADS // 2
    return pl.pallas_call(
        functools.partial(_fox_kernel, t=t),
        grid=(bsz, npair, nb),
        in_specs=[pl.BlockSpec((1, 2 * HEAD_DIM, t), lambda b, p, i: (b, p, i)),
                  pl.BlockSpec((1, 32, t), lambda b, p, i: (b, p, i)),
                  pl.BlockSpec((1, s, 2 * LANES), lambda b, p, i: (b, 0, p)),
                  pl.BlockSpec((1, 2, nb, HEAD_DIM + ONES_ROWS, t), lambda b, p, i: (b, p, 0, 0, 0))],
        out_specs=pl.BlockSpec((1, t, 2 * HEAD_DIM), lambda b, p, i: (b, i, p)),
        out_shape=jax.ShapeDtypeStruct((bsz, s, FOX_W), BF16),
        scratch_shapes=[pltpu.VMEM((2 * LANES, 2 * t), BF16),
                        pltpu.VMEM((t, 2 * t), F32),
                        pltpu.VMEM((t, 2 * t), F32),
                        pltpu.VMEM((2, 1, t), F32),
                        pltpu.VMEM((2, HEAD_DIM + ONES_ROWS, t), F32)],
        compiler_params=_params("arbitrary", "arbitrary", "arbitrary"),
        name="fox_attn",
    )(qt, fqt, kp, vt)


def _diff_kernel(lam_ref, q_ref, k_ref, v_ref, bias_ref, g_ref, o_ref,
                 qbd_ref, sa_ref, sb_ref, m_ref, acc_ref, *, t):
    i = pl.program_id(2)
    d = HEAD_DIM
    dv = 2 * HEAD_DIM
    qbd_ref[...] = jnp.zeros(qbd_ref.shape, BF16)
    qbd_ref[0:d, 0:t] = q_ref[0, 0:d, :]
    qbd_ref[d:2 * d, t:2 * t] = q_ref[0, d:2 * d, :]
    _init_softmax_state(m_ref, acc_ref)

    def qk_into(s_ref, j):
        kblk = k_ref[0, pl.ds(pl.multiple_of(j * t, t), t), :]
        s_ref[...] = _dot(kblk, qbd_ref[...])

    def process(s_ref, j, kind):
        vt = v_ref[0, 0, j]
        for hh in range(2):
            sh = s_ref[:, hh * t:(hh + 1) * t]
            if kind is not None:
                sh = sh + bias_ref[0, kind]
            _softmax_block(sh, vt, m_ref, acc_ref, hh)

    _scan_key_blocks(i, 2, qk_into, process, (sa_ref, sb_ref))

    lam = (jnp.exp(jnp.sum(lam_ref[0:1, :] * lam_ref[1:2, :], axis=1, keepdims=True))
           - jnp.exp(jnp.sum(lam_ref[2:3, :] * lam_ref[3:4, :], axis=1, keepdims=True))
           + LAMBDA_INIT)
    out = (acc_ref[0, 0:dv] / acc_ref[0, dv:dv + 1]
           - lam * (acc_ref[1, 0:dv] / acc_ref[1, dv:dv + 1]))
    ms = jnp.mean(out * out, axis=0, keepdims=True)
    out = out * lax.rsqrt(ms + SUBLN_EPS) * g_ref[...] * (1.0 - LAMBDA_INIT)
    o_ref[0] = out.T.astype(BF16)


def _diff_attention(lam_vecs, qt, k, vt, bias_tiles, norm_g, t):
    bsz, _, s = qt.shape
    nb = s // t
    dv = 2 * HEAD_DIM
    return pl.pallas_call(
        functools.partial(_diff_kernel, t=t),
        grid=(bsz, DIFF_HEADS, nb),
        in_specs=[pl.BlockSpec((4, HEAD_DIM), lambda b, h, i: (0, 0)),
                  pl.BlockSpec((1, dv, t), lambda b, h, i: (b, h, i)),
                  pl.BlockSpec((1, s, dv), lambda b, h, i: (b, 0, h)),
                  pl.BlockSpec((1, 1, nb, dv + ONES_ROWS, t), lambda b, h, i: (b, h, 0, 0, 0)),
                  pl.BlockSpec((1, 2, t, t), lambda b, h, i: (h, 0, 0, 0)),
                  pl.BlockSpec((dv, 1), lambda b, h, i: (0, 0))],
        out_specs=pl.BlockSpec((1, t, dv), lambda b, h, i: (b, i, h)),
        out_shape=jax.ShapeDtypeStruct((bsz, s, DIFF_W), BF16),
        scratch_shapes=[pltpu.VMEM((dv, 2 * t), BF16),
                        pltpu.VMEM((t, 2 * t), F32),
                        pltpu.VMEM((t, 2 * t), F32),
                        pltpu.VMEM((2, 1, t), F32),
                        pltpu.VMEM((2, dv + ONES_ROWS, t), F32)],
        compiler_params=_params("arbitrary", "arbitrary", "arbitrary"),
        name="diff_attn",
    )(lam_vecs, qt, k, vt, bias_tiles, norm_g.reshape(dv, 1))


def _post_attn_kernel(x_ref, yf_ref, yd_ref, sc1_ref, sh1_ref, g1_ref, sc2_ref, sh2_ref,
                      ln1_ref, ln2_ref, wgl_ref, bm_ref, wbf_ref, wbd_ref, wo_ref, wr_ref, br_ref,
                      x1_ref, h2_ref, idx_ref, rank_ref, gate_ref, cnt_ref, carry_ref, *, tm):
    i = pl.program_id(0)

    @pl.when(i == 0)
    def _():
        carry_ref[...] = jnp.zeros_like(carry_ref)

    d = x_ref.shape[1]
    x = x_ref[...]
    h1 = _modulated_rmsnorm(x, ln1_ref[...], sc1_ref[0], sh1_ref[0]).astype(BF16)
    gl = _dot(h1, wgl_ref[...]) + bm_ref[...]
    gates = 1.0 / (1.0 + jnp.exp(-gl))
    merged = (gates[:, :d] * _dot(yf_ref[...], wbf_ref[...])
              + gates[:, d:] * _dot(yd_ref[...], wbd_ref[...]))
    x1 = x + g1_ref[0] * _dot(merged.astype(BF16), wo_ref[...])
    x1_ref[...] = x1
    h2 = _modulated_rmsnorm(x1, ln2_ref[...], sc2_ref[0], sh2_ref[0])
    h2_ref[...] = h2

    logits = _dot_split(wr_ref[...], h2, _dot_nt) + br_ref[...]
    row = lax.broadcasted_iota(I32, (N_EXPERTS, tm), 0)
    cur = logits
    vals, idxs = [], []
    for _ in range(TOP_K):
        mx = jnp.max(cur, axis=0, keepdims=True)
        ik = jnp.min(jnp.where(cur == mx, row, N_EXPERTS), axis=0, keepdims=True)
        vals.append(mx)
        idxs.append(ik)
        cur = jnp.where(row == ik, -jnp.inf, cur)
    exps = [jnp.exp(v - vals[0]) for v in vals]
    denom = exps[0] + exps[1] + exps[2] + exps[3]
    gate_rows = [e / denom for e in exps]

    onehots = [row == ik for ik in idxs]
    cnt = jnp.zeros((N_EXPERTS, tm), F32)
    for oh in onehots:
        cnt = cnt + jnp.where(oh, 1.0, 0.0)
    r = lax.broadcasted_iota(I32, (tm, tm), 0)
    c = lax.broadcasted_iota(I32, (tm, tm), 1)
    before = jnp.where(r < c, 1.0, 0.0).astype(BF16)
    prior = carry_ref[:, 0:1] + _dot(cnt.astype(BF16), before)
    for k in range(TOP_K):
        rk = jnp.sum(jnp.where(onehots[k], prior, 0.0), axis=0, keepdims=True)
        rank_ref[k:k + 1, :] = rk.astype(I32)
        idx_ref[k:k + 1, :] = idxs[k]
    carry_ref[...] = carry_ref[...] + jnp.sum(cnt, axis=1, keepdims=True)
    cnt_ref[...] = carry_ref[...]

    lrow = lax.broadcasted_iota(I32, (LANES, tm), 0)
    g_t = jnp.zeros((LANES, tm), F32)
    for k in range(TOP_K):
        g_t = jnp.where(lrow == k, gate_rows[k], g_t)
    gate_ref[...] = g_t.T


def _post_attn(x2, yf, yd, mods, ln1_g, ln2_g, b_merge, w_gl, w_brf, w_brd, w_o, w_r, b_r, s, tm):
    n, d = x2.shape
    nbb = s // tm
    bsz = n // s
    sc1, sh1, g1, sc2, sh2 = [m.reshape(bsz, 1, d) for m in mods]
    row_spec = lambda w: pl.BlockSpec((tm, w), lambda i: (i, 0))
    mod_spec = pl.BlockSpec((1, 1, d), lambda i: (i // nbb, 0, 0))
    full = lambda shape: pl.BlockSpec(shape, lambda i: (0,) * len(shape))
    tok_spec = pl.BlockSpec((TOP_K, tm), lambda i: (0, i))
    return pl.pallas_call(
        functools.partial(_post_attn_kernel, tm=tm),
        grid=(n // tm,),
        in_specs=[row_spec(d), row_spec(FOX_W), row_spec(DIFF_W),
                  mod_spec, mod_spec, mod_spec, mod_spec, mod_spec,
                  full((1, d)), full((1, d)), full(w_gl.shape), full((1, 2 * d)),
                  full(w_brf.shape), full(w_brd.shape), full(w_o.shape),
                  full((N_EXPERTS, d)), full((N_EXPERTS, 1))],
        out_specs=[row_spec(d), row_spec(d), tok_spec, tok_spec, row_spec(LANES),
                   full((N_EXPERTS, LANES))],
        out_shape=[jax.ShapeDtypeStruct((n, d), F32), jax.ShapeDtypeStruct((n, d), F32),
                   jax.ShapeDtypeStruct((TOP_K, n), I32), jax.ShapeDtypeStruct((TOP_K, n), I32),
                   jax.ShapeDtypeStruct((n, LANES), F32),
                   jax.ShapeDtypeStruct((N_EXPERTS, LANES), F32)],
        scratch_shapes=[pltpu.VMEM((N_EXPERTS, LANES), F32)],
        compiler_params=_params("arbitrary"),
        name="post_attn",
    )(x2, yf, yd, sc1, sh1, g1, sc2, sh2, ln1_g.reshape(1, d), ln2_g.reshape(1, d),
      w_gl.astype(BF16), b_merge.reshape(1, 2 * d), w_brf.astype(BF16), w_brd.astype(BF16),
      w_o.astype(BF16), w_r.T, b_r.reshape(N_EXPERTS, 1))


def _route_kernel(cnt_ref, idx_ref, rank_ref, dest_ref, be_ref, pend_ref, *, blk, nblk_lanes):
    shift = blk.bit_length() - 1
    idx = idx_ref[...]
    start_of = jnp.zeros(idx.shape, I32)
    blk_start = lax.broadcasted_iota(I32, (1, nblk_lanes), 1) * blk
    lane = lax.broadcasted_iota(I32, (1, LANES), 1)
    blk_e = jnp.zeros((1, nblk_lanes), I32)
    pends = jnp.zeros((1, LANES), I32)
    pend = jnp.int32(0)
    for e in range(N_EXPERTS):
        padded = ((cnt_ref[e] + (blk - 1)) >> shift) << shift
        start_of = jnp.where(idx == e, pend, start_of)
        pend = pend + padded
        blk_e = blk_e + jnp.where(blk_start >= pend, 1, 0)
        pends = jnp.where(lane == e, pend, pends)
    dest_ref[...] = start_of + rank_ref[...]
    be_ref[...] = jnp.minimum(blk_e, N_EXPERTS - 1)
    pend_ref[...] = pends


def _route(counts, idx, rank, blk):
    n = idx.shape[1]
    nblk = (n * TOP_K + blk - 1) // blk + N_EXPERTS
    nblk_lanes = -(-nblk // LANES) * LANES
    dest, blk_e, pends = pl.pallas_call(
        functools.partial(_route_kernel, blk=blk, nblk_lanes=nblk_lanes),
        in_specs=[pl.BlockSpec(memory_space=pltpu.SMEM),
                  pl.BlockSpec(memory_space=pltpu.VMEM),
                  pl.BlockSpec(memory_space=pltpu.VMEM)],
        out_specs=[pl.BlockSpec(memory_space=pltpu.VMEM)] * 3,
        out_shape=[jax.ShapeDtypeStruct((TOP_K, n), I32),
                   jax.ShapeDtypeStruct((1, nblk_lanes), I32),
                   jax.ShapeDtypeStruct((1, LANES), I32)],
        compiler_params=pltpu.CompilerParams(vmem_limit_bytes=VMEM_LIMIT),
        name="route",
    )(counts, idx, rank)
    return dest, blk_e[0, :nblk], pends[0, :N_EXPERTS], nblk


def _dispatch_kernel(pend_ref, dest_ref, h_ref, xs_ref, zero_ref, sem, *, tm, blk, nblk):
    shift = blk.bit_length() - 1

    @pl.when(pl.program_id(0) == 0)
    def _():
        zero_ref[...] = jnp.zeros(zero_ref.shape, F32)

        def zero_block(b):
            return pltpu.make_async_copy(
                zero_ref, xs_ref.at[pl.ds(pl.multiple_of(b * blk, blk), blk)], sem)

        def tails(fn):
            for e in range(N_EXPERTS):
                prev = pend_ref[e - 1] if e else 0

                @pl.when(pend_ref[e] > prev)
                def _():
                    fn(zero_block((pend_ref[e] >> shift) - 1))

        def rest(fn):
            def body(b, carry):
                fn(zero_block(b))
                return carry
            lax.fori_loop(pend_ref[N_EXPERTS - 1] >> shift, nblk, body, 0)

        tails(lambda cp: cp.start())
        rest(lambda cp: cp.start())
        tails(lambda cp: cp.wait())
        rest(lambda cp: cp.wait())

    def row_copy(t, k):
        return pltpu.make_async_copy(h_ref.at[pl.ds(t, 1)], xs_ref.at[pl.ds(dest_ref[k, t], 1)], sem)

    def start(t, carry):
        for k in range(TOP_K):
            row_copy(t, k).start(priority=k % 2)
        return carry

    def wait(t, carry):
        for k in range(TOP_K):
            row_copy(t, k).wait()
        return carry

    lax.fori_loop(0, tm, start, 0)
    lax.fori_loop(0, tm, wait, 0)


def _dispatch(pends, dest, h2, tm, blk, nblk):
    n, d = h2.shape
    return pl.pallas_call(
        functools.partial(_dispatch_kernel, tm=tm, blk=blk, nblk=nblk),
        grid=(n // tm,),
        in_specs=[pl.BlockSpec(memory_space=pltpu.SMEM),
                  pl.BlockSpec((TOP_K, tm), lambda i: (0, i), memory_space=pltpu.SMEM),
                  pl.BlockSpec((tm, d), lambda i: (i, 0))],
        out_specs=pl.BlockSpec(memory_space=pl.ANY),
        out_shape=jax.ShapeDtypeStruct((nblk * blk, d), F32),
        scratch_shapes=[pltpu.VMEM((blk, d), F32), pltpu.SemaphoreType.DMA(())],
        compiler_params=_params("arbitrary"),
        name="dispatch",
    )(pends, dest, h2)


UP_CHUNK = 512


def _expert_kernel(be_ref, pend_ref, x_ref, wu_ref, bu_ref, wd_ref, bd_ref, o_ref,
                   wu_sc, wd_sc, *, blk):
    b = pl.program_id(0)
    live = b * blk < pend_ref[N_EXPERTS - 1]
    new_expert = (b == 0) | (be_ref[b] != be_ref[jnp.maximum(b - 1, 0)])

    @pl.when(live & new_expert)
    def _():
        wu_sc[...] = wu_ref[0].astype(BF16)
        wd_sc[...] = wd_ref[0].astype(BF16)

    @pl.when(live)
    def _():
        x = x_ref[...].astype(BF16)
        half = UP_CHUNK // 2
        r = lax.broadcasted_iota(I32, (UP_CHUNK, half), 0)
        c = lax.broadcasted_iota(I32, (UP_CHUNK, half), 1)
        even = jnp.where(r == 2 * c, 1.0, 0.0).astype(BF16)
        acts = []
        for n in range(wu_sc.shape[1] // UP_CHUNK):
            cols = slice(n * UP_CHUNK, (n + 1) * UP_CHUNK)
            u = _dot(x, wu_sc[:, cols]) + bu_ref[0, :, cols]
            u_glu = jnp.minimum(u, SWIGLU_LIMIT)
            u_lin = jnp.clip(u, -SWIGLU_LIMIT, SWIGLU_LIMIT) + 1.0
            z = (u_glu * (1.0 / (1.0 + jnp.exp(-SWIGLU_ALPHA * u_glu)))
                 * pltpu.roll(u_lin, UP_CHUNK - 1, 1))
            acts.append(_dot(z.astype(BF16), even).astype(BF16))
        o_ref[...] = _dot(jnp.concatenate(acts, axis=1), wd_sc[...]) + bd_ref[0]

    @pl.when(jnp.logical_not(live))
    def _():
        o_ref[...] = jnp.zeros(o_ref.shape, F32)


def _experts(blk_e, pends, xs, w_up, b_up, w_down, b_down, blk):
    p, d = xs.shape
    ne, _, f2 = w_up.shape
    f = w_down.shape[1]
    shift = blk.bit_length() - 1

    def live(b, be, pe):
        return jnp.minimum(b, (pe[N_EXPERTS - 1] >> shift) - 1)

    x_spec = pl.BlockSpec((blk, d), lambda b, be, pe: (live(b, be, pe), 0))
    w_spec = lambda r, c: pl.BlockSpec((1, r, c), lambda b, be, pe: (be[live(b, be, pe)], 0, 0))
    return pl.pallas_call(
        functools.partial(_expert_kernel, blk=blk),
        grid_spec=pltpu.PrefetchScalarGridSpec(
            num_scalar_prefetch=2,
            grid=(p // blk,),
            in_specs=[x_spec, w_spec(d, f2), w_spec(1, f2), w_spec(f, d), w_spec(1, d)],
            out_specs=pl.BlockSpec((blk, d), lambda b, be, pe: (b, 0)),
            scratch_shapes=[pltpu.VMEM((d, f2), BF16), pltpu.VMEM((f, d), BF16)]),
        out_shape=jax.ShapeDtypeStruct((p, d), F32),
        compiler_params=_params("arbitrary"),
        name="experts",
    )(blk_e, pends, xs, w_up, b_up.reshape(ne, 1, f2), w_down, b_down.reshape(ne, 1, d))


def _combine_kernel(dest_ref, x1_ref, gate_ref, g2_ref, fg_ref, ys_ref, o_ref, ybuf_ref, sem, *, tm):
    def row_copy(t, k):
        return pltpu.make_async_copy(ys_ref.at[pl.ds(dest_ref[k, t], 1)],
                                     ybuf_ref.at[k, pl.ds(t, 1)], sem)

    def start(t, carry):
        for k in range(TOP_K):
            row_copy(t, k).start(priority=k % 2)
        return carry

    def wait(t, carry):
        for k in range(TOP_K):
            row_copy(t, k).wait()
        return carry

    lax.fori_loop(0, tm, start, 0)
    lax.fori_loop(0, tm, wait, 0)

    g = gate_ref[...]
    moe = g[:, 0:1] * ybuf_ref[0]
    for k in range(1, TOP_K):
        moe = moe + g[:, k:k + 1] * ybuf_ref[k]
    xo = x1_ref[...] + g2_ref[0] * moe
    o_ref[...] = xo * lax.rsqrt(jnp.mean(xo * xo, axis=-1, keepdims=True) + NORM_EPS) * fg_ref[...]


def _combine(dest, x1, gates, g2, final_g, ys, s, tm):
    n, d = x1.shape
    nbb = s // tm
    bsz = n // s
    return pl.pallas_call(
        functools.partial(_combine_kernel, tm=tm),
        grid=(n // tm,),
        in_specs=[pl.BlockSpec((TOP_K, tm), lambda i: (0, i), memory_space=pltpu.SMEM),
                  pl.BlockSpec((tm, d), lambda i: (i, 0)),
                  pl.BlockSpec((tm, LANES), lambda i: (i, 0)),
                  pl.BlockSpec((1, 1, d), lambda i: (i // nbb, 0, 0)),
                  pl.BlockSpec((1, d), lambda i: (0, 0)),
                  pl.BlockSpec(memory_space=pl.ANY)],
        out_specs=pl.BlockSpec((tm, d), lambda i: (i, 0)),
        out_shape=jax.ShapeDtypeStruct((n, d), F32),
        scratch_shapes=[pltpu.VMEM((TOP_K, tm, d), F32), pltpu.SemaphoreType.DMA(())],
        compiler_params=_params("arbitrary"),
        name="combine",
    )(dest, x1, gates, g2.reshape(bsz, 1, d), final_g.reshape(1, d), ys)


def kernel(x, c, w_ada, b_ada, ln1_g, w_in, b_fgate, b_merge, lam_q1, lam_k1, lam_q2, lam_k2,
           diff_norm_g, t5_bias, w_br_fox, w_br_diff, w_o, ln2_g, w_router, b_router,
           w_up, b_up, w_down, b_down, final_g):
    bsz, s, d = x.shape
    n = bsz * s
    t = min(512, s)
    tr = min(256, s)
    assert s % t == 0 and w_ada.shape[0] == 1

    mod = _adaln_mod(c, w_ada[0], b_ada[0])
    sh1, sc1, g1, sh2, sc2, g2 = jnp.split(mod, 6, axis=-1)

    qtf, fqt, kpf, vtf, qtd, kd, vtd = _in_proj(x, sc1, sh1, ln1_g[0], w_in[0], b_fgate[0], t)
    y_fox = _fox_attention(qtf, fqt, kpf, vtf, t)
    lam_vecs = jnp.stack([lam_q1[0], lam_k1[0], lam_q2[0], lam_k2[0]])
    y_diff = _diff_attention(lam_vecs, qtd, kd, vtd, _bias_tiles(t5_bias, t), diff_norm_g[0], t)

    w_gl = w_in[0][:, w_in.shape[2] - 2 * d:]
    x1, h2, idx, rank, gates, counts = _post_attn(
        x.reshape(n, d), y_fox.reshape(n, FOX_W), y_diff.reshape(n, DIFF_W),
        (sc1, sh1, g1, sc2, sh2), ln1_g[0], ln2_g[0], b_merge[0], w_gl,
        w_br_fox[0], w_br_diff[0], w_o[0], w_router[0], b_router[0], s, t)

    dest, blk_e, pends, nblk = _route(counts[:, 0].astype(I32), idx, rank, EXPERT_BLOCK)
    xs = _dispatch(pends, dest, h2, tr, EXPERT_BLOCK, nblk)
    ys = _experts(blk_e, pends, xs, w_up[0], b_up[0], w_down[0], b_down[0], EXPERT_BLOCK)
    out = _combine(dest, x1, gates, g2, final_g, ys, s, tr)
    return out.reshape(bsz, s, d)
```

```python
import functools
import math

import numpy as np
import jax
import jax.numpy as jnp
from jax import lax
from jax.experimental import pallas as pl
from jax.experimental.pallas import tpu as pltpu

F32 = jnp.float32
BF16 = jnp.bfloat16
I32 = jnp.int32

HEAD_DIM = 64
FOX_HEADS = 8
DIFF_HEADS = 4
FOX_W = FOX_HEADS * HEAD_DIM
DIFF_W = DIFF_HEADS * 2 * HEAD_DIM
T5_BUCKETS = 32
T5_MAX_DIST = 128
N_EXPERTS = 32
TOP_K = 4
EXPERT_BLOCK = 512
SWIGLU_ALPHA = 1.702
SWIGLU_LIMIT = 7.0
NORM_EPS = 1e-6
SUBLN_EPS = 1e-5
LAMBDA_INIT = 0.8 - 0.6 * math.exp(-0.3 * 0)

LANES = 128
NEG = -1e30
LOG2E = math.log2(math.e)
ONES_ROWS = 16
VMEM_LIMIT = 56 * 1024 * 1024

NT_DIMS = (((1,), (1,)), ((), ()))


def _dot(a, b):
    return jnp.dot(a, b, preferred_element_type=F32)


def _dot_nt(a, b):
    return lax.dot_general(a, b, NT_DIMS, preferred_element_type=F32)


def _split2(v):
    hi = v.astype(BF16)
    return hi, (v - hi.astype(F32)).astype(BF16)


def _dot_split(a, b, dot=_dot):
    a_hi, a_lo = _split2(a)
    b_hi, b_lo = _split2(b)
    return dot(a_hi, b_hi) + (dot(a_hi, b_lo) + dot(a_lo, b_hi))


def _params(*sem):
    return pltpu.CompilerParams(dimension_semantics=sem, vmem_limit_bytes=VMEM_LIMIT)


def _modulated_rmsnorm(x, g, scale, shift):
    y = x * lax.rsqrt(jnp.mean(x * x, axis=-1, keepdims=True) + NORM_EPS)
    return (y * g) * (1.0 + scale) + shift


ROW_TILE = 8


def _tiled_shape(n, d):
    assert d == ROW_TILE * LANES
    return (n * ROW_TILE, LANES)


def _row_tile(ref, r):
    return ref.at[pl.ds(pl.multiple_of(r * ROW_TILE, ROW_TILE), ROW_TILE)]


def _store_rows_tiled(ref, rows):
    n = rows.shape[0]
    for a in range(ROW_TILE):
        ref[pl.ds(a, n, stride=ROW_TILE), :] = rows[:, a * LANES:(a + 1) * LANES]


def _load_rows_tiled(ref):
    n = ref.shape[0] // ROW_TILE
    return jnp.concatenate([ref[pl.ds(a, n, stride=ROW_TILE), :] for a in range(ROW_TILE)], axis=1)


def _split3(v):
    hi = v.astype(BF16)
    r1 = v - hi.astype(F32)
    mid = r1.astype(BF16)
    lo = (r1 - mid.astype(F32)).astype(BF16)
    return hi, mid, lo


def _adaln_kernel(c_ref, w_ref, b_ref, o_ref):
    c = c_ref[...]
    s = c / (1.0 + jnp.exp(-c))
    o_ref[...] = _dot_split(s, w_ref[...]) + b_ref[...]


def _adaln_mod(c, w, b):
    bsz, d = c.shape
    n = w.shape[1]
    rows = 8
    tn = 1536
    c8 = jnp.zeros((rows, d), F32).at[:bsz].set(c)
    out = pl.pallas_call(
        _adaln_kernel,
        grid=(n // tn,),
        in_specs=[pl.BlockSpec((rows, d), lambda j: (0, 0)),
                  pl.BlockSpec((d, tn), lambda j: (0, j)),
                  pl.BlockSpec((1, tn), lambda j: (0, j))],
        out_specs=pl.BlockSpec((rows, tn), lambda j: (0, j)),
        out_shape=jax.ShapeDtypeStruct((rows, n), F32),
        compiler_params=_params("arbitrary"),
        name="adaln_mod",
    )(c8, w, b.reshape(1, n))
    return out[:bsz]


def _bias_tile_kernel(tb_ref, o_ref, *, t):
    h = pl.program_id(0)
    key = lax.broadcasted_iota(I32, (t, t), 0)
    qry = lax.broadcasted_iota(I32, (t, t), 1)
    far = tb_ref[h, T5_BUCKETS - 1]
    max_exact = T5_BUCKETS // 2
    for which in range(2):
        rel = qry - key + which * t
        n = jnp.maximum(rel, 0)
        nf = jnp.maximum(n, max_exact).astype(F32)
        large = max_exact + (jnp.log(nf / max_exact) / math.log(T5_MAX_DIST / max_exact)
                             * (T5_BUCKETS - max_exact)).astype(I32)
        large = jnp.minimum(large, T5_BUCKETS - 1)
        bucket = jnp.where(n < max_exact, n, large)
        bias = jnp.zeros((t, t), F32)
        for b in range(T5_BUCKETS):
            bias = jnp.where(bucket == b, tb_ref[h, b] - far, bias)
        o_ref[0, which] = jnp.where(rel >= 0, bias * LOG2E, NEG)


def _bias_tiles(t5_bias, t):
    tb = t5_bias.T.astype(F32)
    return pl.pallas_call(
        functools.partial(_bias_tile_kernel, t=t),
        grid=(DIFF_HEADS,),
        in_specs=[pl.BlockSpec(memory_space=pltpu.SMEM)],
        out_specs=pl.BlockSpec((1, 2, t, t), lambda h: (h, 0, 0, 0)),
        out_shape=jax.ShapeDtypeStruct((DIFF_HEADS, 2, t, t), F32),
        compiler_params=_params("arbitrary"),
        name="bias_tiles",
    )(tb)


def _fparts_perm():
    npair = FOX_HEADS // 2
    pk = np.zeros((3 * LANES, npair * LANES), np.float32)
    pq = np.zeros((npair * 32, 3 * LANES), np.float32)
    k_ones = np.zeros((1, npair * LANES), np.float32)
    q_ones = np.zeros((npair * 32, 1), np.float32)
    for p in range(npair):
        for hh in range(2):
            head = 2 * p + hh
            for part in range(3):
                pk[part * LANES + head, p * LANES + hh * 16 + 3 + part] = -1.0
                k_ones[0, p * LANES + hh * 16 + part] = 1.0
                pq[p * 32 + hh * 16 + part, part * LANES + head] = 1.0
                q_ones[p * 32 + hh * 16 + 3 + part, 0] = 1.0
    return pk, pq, k_ones, q_ones


def _in_proj_kernel(x_ref, sc_ref, sh_ref, g_ref, wt_ref, wn_ref, wfl_ref, bf_ref,
                    pk_ref, pq_ref, kone_ref, qone_ref,
                    qtf_ref, fqt_ref, kpf_ref, vtf_ref, qtd_ref, kd_ref, vtd_ref,
                    carry_ref, *, tm):
    i = pl.program_id(1)

    @pl.when(i == 0)
    def _():
        carry_ref[...] = jnp.zeros_like(carry_ref)

    h = _modulated_rmsnorm(x_ref[0], g_ref[...], sc_ref[0], sh_ref[0])
    hb = h.astype(BF16)

    h_lo = (h - hb.astype(F32)).astype(BF16)
    wfl_hi, wfl_lo = _split2(wfl_ref[...])
    fl = _dot(hb, wfl_hi) + (_dot(hb, wfl_lo) + _dot(h_lo, wfl_hi)) + bf_ref[...]
    logf = jnp.minimum(fl, 0.0) - jnp.log(1.0 + jnp.exp(-jnp.abs(fl)))
    r = lax.broadcasted_iota(I32, (tm, tm), 0)
    c = lax.broadcasted_iota(I32, (tm, tm), 1)
    tri = jnp.where(c <= r, 1.0, 0.0).astype(BF16)
    psum = _dot(tri, jnp.concatenate(_split3(logf), axis=1))
    fsum = (psum[:, 0:LANES] + (psum[:, LANES:2 * LANES] + psum[:, 2 * LANES:3 * LANES])
            + carry_ref[...])
    carry_ref[...] = fsum[tm - 1:tm, :]
    parts = jnp.concatenate(_split3(fsum * LOG2E), axis=1)

    scale = HEAD_DIM ** -0.5 * LOG2E
    yt = _dot_nt(wt_ref[...], hb)
    yn = _dot(hb, wn_ref[...])

    ones_rows = jnp.where(lax.broadcasted_iota(I32, (ONES_ROWS, tm), 0) == 0, 1.0, 0.0).astype(BF16)
    qtf_ref[0] = (yt[0:FOX_W] * scale).astype(BF16)
    vt = yt[FOX_W:2 * FOX_W].astype(BF16)
    for hd in range(FOX_HEADS):
        vtf_ref[0, hd, 0, 0:HEAD_DIM] = vt[hd * HEAD_DIM:(hd + 1) * HEAD_DIM]
        vtf_ref[0, hd, 0, HEAD_DIM:HEAD_DIM + ONES_ROWS] = ones_rows
    qtd_ref[0] = (yt[2 * FOX_W:2 * FOX_W + DIFF_W] * scale).astype(BF16)
    vtd = yt[2 * FOX_W + DIFF_W:2 * FOX_W + 2 * DIFF_W].astype(BF16)
    dv = 2 * HEAD_DIM
    for hd in range(DIFF_HEADS):
        vtd_ref[0, hd, 0, 0:dv] = vtd[hd * dv:(hd + 1) * dv]
        vtd_ref[0, hd, 0, dv:dv + ONES_ROWS] = ones_rows

    fqt_ref[0] = (_dot_nt(pq_ref[...], parts) + qone_ref[...]).astype(BF16)
    kext = (_dot(parts, pk_ref[...]) + kone_ref[...]).astype(BF16)
    kf = yn[:, 0:FOX_W].astype(BF16)
    for p in range(FOX_HEADS // 2):
        kpf_ref[0, :, 2 * p * LANES:(2 * p + 1) * LANES] = kf[:, p * LANES:(p + 1) * LANES]
        kpf_ref[0, :, (2 * p + 1) * LANES:(2 * p + 2) * LANES] = kext[:, p * LANES:(p + 1) * LANES]
    kd_ref[0] = yn[:, FOX_W:FOX_W + DIFF_W].astype(BF16)


def _in_proj(x, sc1, sh1, ln_g, w_in, b_fgate, t):
    bsz, s, d = x.shape
    nb = s // t
    cuts = np.cumsum([FOX_W, FOX_W, FOX_W, FOX_HEADS, DIFF_W, DIFF_W, DIFF_W])
    w_fq, w_fk, w_fv, w_fl, w_dq, w_dk, w_dv = [
        w_in[:, a:b] for a, b in zip([0, *cuts[:-1]], cuts)]
    wt = jnp.concatenate([w_fq, w_fv, w_dq, w_dv], axis=1).T.astype(BF16)
    wn = jnp.concatenate([w_fk, w_dk], axis=1).astype(BF16)
    wfl = jnp.zeros((d, LANES), F32).at[:, :FOX_HEADS].set(w_fl)
    bfl = jnp.zeros((1, LANES), F32).at[0, :FOX_HEADS].set(b_fgate)
    pk, pq, k_ones, q_ones = _fparts_perm()
    npair = FOX_HEADS // 2

    full = lambda shape: pl.BlockSpec(shape, lambda b, i: (0,) * len(shape))
    outs = pl.pallas_call(
        functools.partial(_in_proj_kernel, tm=t),
        grid=(bsz, nb),
        in_specs=[pl.BlockSpec((1, t, d), lambda b, i: (b, i, 0)),
                  pl.BlockSpec((1, 1, d), lambda b, i: (b, 0, 0)),
                  pl.BlockSpec((1, 1, d), lambda b, i: (b, 0, 0)),
                  full((1, d)), full(wt.shape), full(wn.shape), full(wfl.shape), full(bfl.shape),
                  full(pk.shape), full(pq.shape), full(k_ones.shape), full(q_ones.shape)],
        out_specs=[pl.BlockSpec((1, FOX_W, t), lambda b, i: (b, 0, i)),
                   pl.BlockSpec((1, npair * 32, t), lambda b, i: (b, 0, i)),
                   pl.BlockSpec((1, t, npair * 2 * LANES), lambda b, i: (b, i, 0)),
                   pl.BlockSpec((1, FOX_HEADS, 1, HEAD_DIM + ONES_ROWS, t),
                                lambda b, i: (b, 0, i, 0, 0)),
                   pl.BlockSpec((1, DIFF_W, t), lambda b, i: (b, 0, i)),
                   pl.BlockSpec((1, t, DIFF_W), lambda b, i: (b, i, 0)),
                   pl.BlockSpec((1, DIFF_HEADS, 1, 2 * HEAD_DIM + ONES_ROWS, t),
                                lambda b, i: (b, 0, i, 0, 0))],
        out_shape=[jax.ShapeDtypeStruct((bsz, FOX_W, s), BF16),
                   jax.ShapeDtypeStruct((bsz, npair * 32, s), BF16),
                   jax.ShapeDtypeStruct((bsz, s, npair * 2 * LANES), BF16),
                   jax.ShapeDtypeStruct((bsz, FOX_HEADS, nb, HEAD_DIM + ONES_ROWS, t), BF16),
                   jax.ShapeDtypeStruct((bsz, DIFF_W, s), BF16),
                   jax.ShapeDtypeStruct((bsz, s, DIFF_W), BF16),
                   jax.ShapeDtypeStruct((bsz, DIFF_HEADS, nb, 2 * HEAD_DIM + ONES_ROWS, t), BF16)],
        scratch_shapes=[pltpu.VMEM((1, LANES), F32)],
        compiler_params=_params("arbitrary", "arbitrary"),
        name="in_proj",
    )(x, sc1.reshape(bsz, 1, d), sh1.reshape(bsz, 1, d), ln_g.reshape(1, d), wt, wn, wfl, bfl,
      jnp.asarray(pk, BF16), jnp.asarray(pq, BF16), jnp.asarray(k_ones), jnp.asarray(q_ones))
    return outs


def _softmax_block(s, vt, m_ref, acc_ref, slot):
    m_old = m_ref[slot]
    m_new = jnp.maximum(m_old, jnp.max(s, axis=0, keepdims=True))
    alpha = jnp.exp2(m_old - m_new)
    p = jnp.exp2(s - m_new)
    acc_ref[slot] = alpha * acc_ref[slot] + _dot(vt, p.astype(BF16))
    m_ref[slot] = m_new


def _init_softmax_state(m_ref, acc_ref):
    m_ref[...] = jnp.full(m_ref.shape, NEG, F32)
    acc_ref[...] = jnp.zeros(acc_ref.shape, F32)


def _scan_key_blocks(i, n_tail, qk_into, process, bufs):
    n_plain = jnp.maximum(i + 1 - n_tail, 0)
    n_pairs = n_plain >> 1
    rem = n_plain & 1
    qk_into(bufs[0], 0)

    def pair(n, carry):
        j = 2 * n
        qk_into(bufs[1], j + 1)
        process(bufs[0], j, None)
        qk_into(bufs[0], j + 2)
        process(bufs[1], j + 1, None)
        return carry

    lax.fori_loop(0, n_pairs, pair, 0)
    first = 2 * n_pairs

    def tail(kinds):
        for n, kind in enumerate(kinds):
            if n + 1 < len(kinds):
                qk_into(bufs[(n + 1) % 2], first + n + 1)
            process(bufs[n % 2], first + n, kind)

    for n_last in range(1, n_tail + 1):
        kinds = list(range(n_last - 1, -1, -1))
        if n_last < n_tail:
            pl.when(i + 1 == n_last)(functools.partial(tail, kinds))
        else:
            for r in range(2):
                pl.when((i + 1 >= n_tail) & (rem == r))(functools.partial(tail, [None] * r + kinds))


def _fox_kernel(q_ref, fq_ref, k_ref, v_ref, o_ref, qbd_ref, sa_ref, sb_ref, m_ref, acc_ref, *, t):
    i = pl.program_id(2)
    d = HEAD_DIM
    qbd_ref[...] = jnp.zeros(qbd_ref.shape, BF16)
    qbd_ref[0:d, 0:t] = q_ref[0, 0:d, :]
    qbd_ref[d:2 * d, t:2 * t] = q_ref[0, d:2 * d, :]
    qbd_ref[2 * d:2 * d + 16, 0:t] = fq_ref[0, 0:16, :]
    qbd_ref[2 * d + 16:2 * d + 32, t:2 * t] = fq_ref[0, 16:32, :]
    _init_softmax_state(m_ref, acc_ref)

    def qk_into(s_ref, j):
        kblk = k_ref[0, pl.ds(pl.multiple_of(j * t, t), t), :]
        s_ref[...] = _dot(kblk, qbd_ref[...])

    def process(s_ref, j, kind):
        for hh in range(2):
            sh = s_ref[:, hh * t:(hh + 1) * t]
            if kind == 0:
                key = lax.broadcasted_iota(I32, (t, t), 0)
                qry = lax.broadcasted_iota(I32, (t, t), 1)
                sh = jnp.where(key <= qry, sh, NEG)
            _softmax_block(sh, v_ref[0, hh, j], m_ref, acc_ref, hh)

    _scan_key_blocks(i, 1, qk_into, process, (sa_ref, sb_ref))

    outs = [acc_ref[hh, 0:d] / acc_ref[hh, d:d + 1] for hh in range(2)]
    o_ref[0] = jnp.concatenate(outs, axis=0).T.astype(BF16)


def _fox_attention(qt, fqt, kp, vt, t):
    bsz, _, s = qt.shape
    nb = s // t
    npair = FOX_HEADS // 2
    return pl.pallas_call(
        functools.partial(_fox_kernel, t=t),
        grid=(bsz, npair, nb),
        in_specs=[pl.BlockSpec((1, 2 * HEAD_DIM, t), lambda b, p, i: (b, p, i)),
                  pl.BlockSpec((1, 32, t), lambda b, p, i: (b, p, i)),
                  pl.BlockSpec((1, s, 2 * LANES), lambda b, p, i: (b, 0, p)),
                  pl.BlockSpec((1, 2, nb, HEAD_DIM + ONES_ROWS, t), lambda b, p, i: (b, p, 0, 0, 0))],
        out_specs=pl.BlockSpec((1, t, 2 * HEAD_DIM), lambda b, p, i: (b, i, p)),
        out_shape=jax.ShapeDtypeStruct((bsz, s, FOX_W), BF16),
        scratch_shapes=[pltpu.VMEM((2 * LANES, 2 * t), BF16),
                        pltpu.VMEM((t, 2 * t), F32),
                        pltpu.VMEM((t, 2 * t), F32),
                        pltpu.VMEM((2, 1, t), F32),
                        pltpu.VMEM((2, HEAD_DIM + ONES_ROWS, t), F32)],
        compiler_params=_params("arbitrary", "arbitrary", "arbitrary"),
        name="fox_attn",
    )(qt, fqt, kp, vt)


def _diff_kernel(lam_ref, q_ref, k_ref, v_ref, bias_ref, g_ref, o_ref,
                 qbd_ref, sa_ref, sb_ref, m_ref, acc_ref, *, t):
    i = pl.program_id(2)
    d = HEAD_DIM
    dv = 2 * HEAD_DIM
    qbd_ref[...] = jnp.zeros(qbd_ref.shape, BF16)
    qbd_ref[0:d, 0:t] = q_ref[0, 0:d, :]
    qbd_ref[d:2 * d, t:2 * t] = q_ref[0, d:2 * d, :]
    _init_softmax_state(m_ref, acc_ref)

    def qk_into(s_ref, j):
        kblk = k_ref[0, pl.ds(pl.multiple_of(j * t, t), t), :]
        s_ref[...] = _dot(kblk, qbd_ref[...])

    def process(s_ref, j, kind):
        vt = v_ref[0, 0, j]
        for hh in range(2):
            sh = s_ref[:, hh * t:(hh + 1) * t]
            if kind is not None:
                sh = sh + bias_ref[0, kind]
            _softmax_block(sh, vt, m_ref, acc_ref, hh)

    _scan_key_blocks(i, 2, qk_into, process, (sa_ref, sb_ref))

    lam = (jnp.exp(jnp.sum(lam_ref[0:1, :] * lam_ref[1:2, :], axis=1, keepdims=True))
           - jnp.exp(jnp.sum(lam_ref[2:3, :] * lam_ref[3:4, :], axis=1, keepdims=True))
           + LAMBDA_INIT)
    out = (acc_ref[0, 0:dv] / acc_ref[0, dv:dv + 1]
           - lam * (acc_ref[1, 0:dv] / acc_ref[1, dv:dv + 1]))
    ms = jnp.mean(out * out, axis=0, keepdims=True)
    out = out * lax.rsqrt(ms + SUBLN_EPS) * g_ref[...] * (1.0 - LAMBDA_INIT)
    o_ref[0] = out.T.astype(BF16)


def _diff_attention(lam_vecs, qt, k, vt, bias_tiles, norm_g, t):
    bsz, _, s = qt.shape
    nb = s // t
    dv = 2 * HEAD_DIM
    return pl.pallas_call(
        functools.partial(_diff_kernel, t=t),
        grid=(bsz, DIFF_HEADS, nb),
        in_specs=[pl.BlockSpec((4, HEAD_DIM), lambda b, h, i: (0, 0)),
                  pl.BlockSpec((1, dv, t), lambda b, h, i: (b, h, i)),
                  pl.BlockSpec((1, s, dv), lambda b, h, i: (b, 0, h)),
                  pl.BlockSpec((1, 1, nb, dv + ONES_ROWS, t), lambda b, h, i: (b, h, 0, 0, 0)),
                  pl.BlockSpec((1, 2, t, t), lambda b, h, i: (h, 0, 0, 0)),
                  pl.BlockSpec((dv, 1), lambda b, h, i: (0, 0))],
        out_specs=pl.BlockSpec((1, t, dv), lambda b, h, i: (b, i, h)),
        out_shape=jax.ShapeDtypeStruct((bsz, s, DIFF_W), BF16),
        scratch_shapes=[pltpu.VMEM((dv, 2 * t), BF16),
                        pltpu.VMEM((t, 2 * t), F32),
                        pltpu.VMEM((t, 2 * t), F32),
                        pltpu.VMEM((2, 1, t), F32),
                        pltpu.VMEM((2, dv + ONES_ROWS, t), F32)],
        compiler_params=_params("arbitrary", "arbitrary", "arbitrary"),
        name="diff_attn",
    )(lam_vecs, qt, k, vt, bias_tiles, norm_g.reshape(dv, 1))


def _post_attn_kernel(x_ref, yf_ref, yd_ref, sc1_ref, sh1_ref, g1_ref, sc2_ref, sh2_ref,
                      ln1_ref, ln2_ref, wgl_ref, bm_ref, wbf_ref, wbd_ref, wo_ref, wr_ref, br_ref,
                      x1_ref, h2_ref, idx_ref, rank_ref, gate_ref, cnt_ref, carry_ref, *, tm):
    i = pl.program_id(0)

    @pl.when(i == 0)
    def _():
        carry_ref[...] = jnp.zeros_like(carry_ref)

    d = x_ref.shape[1]
    x = x_ref[...]
    h1 = _modulated_rmsnorm(x, ln1_ref[...], sc1_ref[0], sh1_ref[0]).astype(BF16)
    gl = _dot(h1, wgl_ref[...]) + bm_ref[...]
    gates = 1.0 / (1.0 + jnp.exp(-gl))
    merged = (gates[:, :d] * _dot(yf_ref[...], wbf_ref[...])
              + gates[:, d:] * _dot(yd_ref[...], wbd_ref[...]))
    x1 = x + g1_ref[0] * _dot(merged.astype(BF16), wo_ref[...])
    x1_ref[...] = x1
    h2 = _modulated_rmsnorm(x1, ln2_ref[...], sc2_ref[0], sh2_ref[0])
    _store_rows_tiled(h2_ref, h2)

    logits = _dot_split(wr_ref[...], h2, _dot_nt) + br_ref[...]
    row = lax.broadcasted_iota(I32, (N_EXPERTS, tm), 0)
    cur = logits
    vals, idxs = [], []
    for _ in range(TOP_K):
        mx = jnp.max(cur, axis=0, keepdims=True)
        ik = jnp.min(jnp.where(cur == mx, row, N_EXPERTS), axis=0, keepdims=True)
        vals.append(mx)
        idxs.append(ik)
        cur = jnp.where(row == ik, -jnp.inf, cur)
    exps = [jnp.exp(v - vals[0]) for v in vals]
    denom = exps[0] + exps[1] + exps[2] + exps[3]
    gate_rows = [e / denom for e in exps]

    onehots = [row == ik for ik in idxs]
    cnt = jnp.zeros((N_EXPERTS, tm), F32)
    for oh in onehots:
        cnt = cnt + jnp.where(oh, 1.0, 0.0)
    r = lax.broadcasted_iota(I32, (tm, tm), 0)
    c = lax.broadcasted_iota(I32, (tm, tm), 1)
    before = jnp.where(r < c, 1.0, 0.0).astype(BF16)
    prior = carry_ref[:, 0:1] + _dot(cnt.astype(BF16), before)
    for k in range(TOP_K):
        rk = jnp.sum(jnp.where(onehots[k], prior, 0.0), axis=0, keepdims=True)
        rank_ref[k:k + 1, :] = rk.astype(I32)
        idx_ref[k:k + 1, :] = idxs[k]
    carry_ref[...] = carry_ref[...] + jnp.sum(cnt, axis=1, keepdims=True)
    cnt_ref[...] = carry_ref[...]

    lrow = lax.broadcasted_iota(I32, (LANES, tm), 0)
    g_t = jnp.zeros((LANES, tm), F32)
    for k in range(TOP_K):
        g_t = jnp.where(lrow == k, gate_rows[k], g_t)
    gate_ref[...] = g_t.T


def _post_attn(x2, yf, yd, mods, ln1_g, ln2_g, b_merge, w_gl, w_brf, w_brd, w_o, w_r, b_r, s, tm):
    n, d = x2.shape
    nbb = s // tm
    bsz = n // s
    sc1, sh1, g1, sc2, sh2 = [m.reshape(bsz, 1, d) for m in mods]
    row_spec = lambda w: pl.BlockSpec((tm, w), lambda i: (i, 0))
    mod_spec = pl.BlockSpec((1, 1, d), lambda i: (i // nbb, 0, 0))
    full = lambda shape: pl.BlockSpec(shape, lambda i: (0,) * len(shape))
    tok_spec = pl.BlockSpec((TOP_K, tm), lambda i: (0, i))
    return pl.pallas_call(
        functools.partial(_post_attn_kernel, tm=tm),
        grid=(n // tm,),
        in_specs=[row_spec(d), row_spec(FOX_W), row_spec(DIFF_W),
                  mod_spec, mod_spec, mod_spec, mod_spec, mod_spec,
                  full((1, d)), full((1, d)), full(w_gl.shape), full((1, 2 * d)),
                  full(w_brf.shape), full(w_brd.shape), full(w_o.shape),
                  full((N_EXPERTS, d)), full((N_EXPERTS, 1))],
        out_specs=[row_spec(d), pl.BlockSpec(_tiled_shape(tm, d), lambda i: (i, 0)),
                   tok_spec, tok_spec, row_spec(LANES), full((N_EXPERTS, LANES))],
        out_shape=[jax.ShapeDtypeStruct((n, d), F32), jax.ShapeDtypeStruct(_tiled_shape(n, d), F32),
                   jax.ShapeDtypeStruct((TOP_K, n), I32), jax.ShapeDtypeStruct((TOP_K, n), I32),
                   jax.ShapeDtypeStruct((n, LANES), F32),
                   jax.ShapeDtypeStruct((N_EXPERTS, LANES), F32)],
        scratch_shapes=[pltpu.VMEM((N_EXPERTS, LANES), F32)],
        compiler_params=_params("arbitrary"),
        name="post_attn",
    )(x2, yf, yd, sc1, sh1, g1, sc2, sh2, ln1_g.reshape(1, d), ln2_g.reshape(1, d),
      w_gl.astype(BF16), b_merge.reshape(1, 2 * d), w_brf.astype(BF16), w_brd.astype(BF16),
      w_o.astype(BF16), w_r.T, b_r.reshape(N_EXPERTS, 1))


def _route_kernel(cnt_ref, idx_ref, rank_ref, dest_ref, be_ref, pend_ref, *, blk, nblk_lanes):
    shift = blk.bit_length() - 1
    idx = idx_ref[...]
    start_of = jnp.zeros(idx.shape, I32)
    blk_start = lax.broadcasted_iota(I32, (1, nblk_lanes), 1) * blk
    lane = lax.broadcasted_iota(I32, (1, LANES), 1)
    blk_e = jnp.zeros((1, nblk_lanes), I32)
    pends = jnp.zeros((1, LANES), I32)
    pend = jnp.int32(0)
    for e in range(N_EXPERTS):
        padded = ((cnt_ref[e] + (blk - 1)) >> shift) << shift
        start_of = jnp.where(idx == e, pend, start_of)
        pend = pend + padded
        blk_e = blk_e + jnp.where(blk_start >= pend, 1, 0)
        pends = jnp.where(lane == e, pend, pends)
    dest_ref[...] = start_of + rank_ref[...]
    be_ref[...] = jnp.minimum(blk_e, N_EXPERTS - 1)
    pend_ref[...] = pends


def _route(counts, idx, rank, blk):
    n = idx.shape[1]
    nblk = (n * TOP_K + blk - 1) // blk + N_EXPERTS
    nblk_lanes = -(-nblk // LANES) * LANES
    dest, blk_e, pends = pl.pallas_call(
        functools.partial(_route_kernel, blk=blk, nblk_lanes=nblk_lanes),
        in_specs=[pl.BlockSpec(memory_space=pltpu.SMEM),
                  pl.BlockSpec(memory_space=pltpu.VMEM),
                  pl.BlockSpec(memory_space=pltpu.VMEM)],
        out_specs=[pl.BlockSpec(memory_space=pltpu.VMEM)] * 3,
        out_shape=[jax.ShapeDtypeStruct((TOP_K, n), I32),
                   jax.ShapeDtypeStruct((1, nblk_lanes), I32),
                   jax.ShapeDtypeStruct((1, LANES), I32)],
        compiler_params=pltpu.CompilerParams(vmem_limit_bytes=VMEM_LIMIT),
        name="route",
    )(counts, idx, rank)
    return dest, blk_e[0, :nblk], pends[0, :N_EXPERTS], nblk


def _dispatch_kernel(pend_ref, dest_ref, h_ref, xs_ref, zero_ref, sem, *, tm, blk, nblk):
    shift = blk.bit_length() - 1

    @pl.when(pl.program_id(0) == 0)
    def _():
        zero_ref[...] = jnp.zeros(zero_ref.shape, F32)

        def zero_block(b):
            lines = blk * ROW_TILE
            return pltpu.make_async_copy(
                zero_ref, xs_ref.at[pl.ds(pl.multiple_of(b * lines, lines), lines)], sem)

        def tails(fn):
            for e in range(N_EXPERTS):
                prev = pend_ref[e - 1] if e else 0

                @pl.when(pend_ref[e] > prev)
                def _():
                    fn(zero_block((pend_ref[e] >> shift) - 1))

        def rest(fn):
            def body(b, carry):
                fn(zero_block(b))
                return carry
            lax.fori_loop(pend_ref[N_EXPERTS - 1] >> shift, nblk, body, 0)

        tails(lambda cp: cp.start())
        rest(lambda cp: cp.start())
        tails(lambda cp: cp.wait())
        rest(lambda cp: cp.wait())

    def row_copy(t, k):
        return pltpu.make_async_copy(_row_tile(h_ref, t), _row_tile(xs_ref, dest_ref[k, t]), sem)

    def start(t, carry):
        for k in range(TOP_K):
            row_copy(t, k).start()
        return carry

    def wait(t, carry):
        for k in range(TOP_K):
            row_copy(t, k).wait()
        return carry

    lax.fori_loop(0, tm, start, 0, unroll=ROW_DMA_UNROLL)
    lax.fori_loop(0, tm, wait, 0, unroll=ROW_DMA_UNROLL)


def _dispatch(pends, dest, h2, tm, blk, nblk):
    n = h2.shape[0] // ROW_TILE
    d = ROW_TILE * LANES
    return pl.pallas_call(
        functools.partial(_dispatch_kernel, tm=tm, blk=blk, nblk=nblk),
        grid=(n // tm,),
        in_specs=[pl.BlockSpec(memory_space=pltpu.SMEM),
                  pl.BlockSpec((TOP_K, tm), lambda i: (0, i), memory_space=pltpu.SMEM),
                  pl.BlockSpec(_tiled_shape(tm, d), lambda i: (i, 0))],
        out_specs=pl.BlockSpec(memory_space=pl.ANY),
        out_shape=jax.ShapeDtypeStruct(_tiled_shape(nblk * blk, d), F32),
        scratch_shapes=[pltpu.VMEM(_tiled_shape(blk, d), F32), pltpu.SemaphoreType.DMA(())],
        compiler_params=_params("arbitrary"),
        name="dispatch",
    )(pends, dest, h2)


ROW_DMA_UNROLL = 4
UP_CHUNK = 2048
PICK_GROUP = 512


def _expert_kernel(be_ref, pend_ref, x_ref, wu_ref, bu_ref, wd_ref, bd_ref, o_ref,
                   wu_sc, wd_sc, *, blk):
    b = pl.program_id(0)
    live = b * blk < pend_ref[N_EXPERTS - 1]
    new_expert = (b == 0) | (be_ref[b] != be_ref[jnp.maximum(b - 1, 0)])

    @pl.when(live & new_expert)
    def _():
        wu_sc[...] = wu_ref[0].astype(BF16)
        wd_sc[...] = wd_ref[0].astype(BF16)

    @pl.when(live)
    def _():
        x = _load_rows_tiled(x_ref).astype(BF16)
        r = lax.broadcasted_iota(I32, (PICK_GROUP, PICK_GROUP // 2), 0)
        c = lax.broadcasted_iota(I32, (PICK_GROUP, PICK_GROUP // 2), 1)
        even = jnp.where(r == 2 * c, 1.0, 0.0).astype(BF16)
        acts = []
        for n in range(wu_sc.shape[1] // UP_CHUNK):
            cols = slice(n * UP_CHUNK, (n + 1) * UP_CHUNK)
            u = _dot(x, wu_sc[:, cols]) + bu_ref[0, :, cols]
            u_glu = jnp.minimum(u, SWIGLU_LIMIT)
            u_lin = jnp.clip(u, -SWIGLU_LIMIT, SWIGLU_LIMIT) + 1.0
            z = (u_glu * (1.0 / (1.0 + jnp.exp(-SWIGLU_ALPHA * u_glu)))
                 * pltpu.roll(u_lin, UP_CHUNK - 1, 1)).astype(BF16)
            for g in range(UP_CHUNK // PICK_GROUP):
                acts.append(_dot(z[:, g * PICK_GROUP:(g + 1) * PICK_GROUP], even).astype(BF16))
        _store_rows_tiled(o_ref, _dot(jnp.concatenate(acts, axis=1), wd_sc[...]) + bd_ref[0])

    @pl.when(jnp.logical_not(live))
    def _():
        o_ref[...] = jnp.zeros(o_ref.shape, F32)


def _experts(blk_e, pends, xs, w_up, b_up, w_down, b_down, blk):
    p = xs.shape[0] // ROW_TILE
    ne, d, f2 = w_up.shape
    f = w_down.shape[1]
    shift = blk.bit_length() - 1

    def live(b, be, pe):
        return jnp.minimum(b, (pe[N_EXPERTS - 1] >> shift) - 1)

    x_spec = pl.BlockSpec(_tiled_shape(blk, d), lambda b, be, pe: (live(b, be, pe), 0))
    w_spec = lambda r, c: pl.BlockSpec((1, r, c), lambda b, be, pe: (be[live(b, be, pe)], 0, 0))
    return pl.pallas_call(
        functools.partial(_expert_kernel, blk=blk),
        grid_spec=pltpu.PrefetchScalarGridSpec(
            num_scalar_prefetch=2,
            grid=(p // blk,),
            in_specs=[x_spec, w_spec(d, f2), w_spec(1, f2), w_spec(f, d), w_spec(1, d)],
            out_specs=pl.BlockSpec(_tiled_shape(blk, d), lambda b, be, pe: (b, 0)),
            scratch_shapes=[pltpu.VMEM((d, f2), BF16), pltpu.VMEM((f, d), BF16)]),
        out_shape=jax.ShapeDtypeStruct(_tiled_shape(p, d), F32),
        compiler_params=_params("arbitrary"),
        name="experts",
    )(blk_e, pends, xs, w_up, b_up.reshape(ne, 1, f2), w_down, b_down.reshape(ne, 1, d))


def _combine_kernel(dest_ref, x1_ref, gate_ref, g2_ref, fg_ref, ys_ref, o_ref, ybuf_ref, sem, *, tm):
    def row_copy(t, k):
        return pltpu.make_async_copy(_row_tile(ys_ref, dest_ref[k, t]), _row_tile(ybuf_ref.at[k], t), sem)

    def start(t, carry):
        for k in range(TOP_K):
            row_copy(t, k).start()
        return carry

    def wait(t, carry):
        for k in range(TOP_K):
            row_copy(t, k).wait()
        return carry

    lax.fori_loop(0, tm, start, 0, unroll=ROW_DMA_UNROLL)
    lax.fori_loop(0, tm, wait, 0, unroll=ROW_DMA_UNROLL)

    g = gate_ref[...]
    moe = g[:, 0:1] * _load_rows_tiled(ybuf_ref.at[0])
    for k in range(1, TOP_K):
        moe = moe + g[:, k:k + 1] * _load_rows_tiled(ybuf_ref.at[k])
    xo = x1_ref[...] + g2_ref[0] * moe
    o_ref[...] = xo * lax.rsqrt(jnp.mean(xo * xo, axis=-1, keepdims=True) + NORM_EPS) * fg_ref[...]


def _combine(dest, x1, gates, g2, final_g, ys, s, tm):
    n, d = x1.shape
    nbb = s // tm
    bsz = n // s
    return pl.pallas_call(
        functools.partial(_combine_kernel, tm=tm),
        grid=(n // tm,),
        in_specs=[pl.BlockSpec((TOP_K, tm), lambda i: (0, i), memory_space=pltpu.SMEM),
                  pl.BlockSpec((tm, d), lambda i: (i, 0)),
                  pl.BlockSpec((tm, LANES), lambda i: (i, 0)),
                  pl.BlockSpec((1, 1, d), lambda i: (i // nbb, 0, 0)),
                  pl.BlockSpec((1, d), lambda i: (0, 0)),
                  pl.BlockSpec(memory_space=pl.ANY)],
        out_specs=pl.BlockSpec((tm, d), lambda i: (i, 0)),
        out_shape=jax.ShapeDtypeStruct((n, d), F32),
        scratch_shapes=[pltpu.VMEM((TOP_K, *_tiled_shape(tm, d)), F32), pltpu.SemaphoreType.DMA(())],
        compiler_params=_params("arbitrary"),
        name="combine",
    )(dest, x1, gates, g2.reshape(bsz, 1, d), final_g.reshape(1, d), ys)


def kernel(x, c, w_ada, b_ada, ln1_g, w_in, b_fgate, b_merge, lam_q1, lam_k1, lam_q2, lam_k2,
           diff_norm_g, t5_bias, w_br_fox, w_br_diff, w_o, ln2_g, w_router, b_router,
           w_up, b_up, w_down, b_down, final_g):
    bsz, s, d = x.shape
    n = bsz * s
    t = min(512, s)
    tr = min(256, s)
    assert s % t == 0 and w_ada.shape[0] == 1

    mod = _adaln_mod(c, w_ada[0], b_ada[0])
    sh1, sc1, g1, sh2, sc2, g2 = jnp.split(mod, 6, axis=-1)

    qtf, fqt, kpf, vtf, qtd, kd, vtd = _in_proj(x, sc1, sh1, ln1_g[0], w_in[0], b_fgate[0], t)
    y_fox = _fox_attention(qtf, fqt, kpf, vtf, t)
    lam_vecs = jnp.stack([lam_q1[0], lam_k1[0], lam_q2[0], lam_k2[0]])
    y_diff = _diff_attention(lam_vecs, qtd, kd, vtd, _bias_tiles(t5_bias, t), diff_norm_g[0], t)

    w_gl = w_in[0][:, w_in.shape[2] - 2 * d:]
    x1, h2, idx, rank, gates, counts = _post_attn(
        x.reshape(n, d), y_fox.reshape(n, FOX_W), y_diff.reshape(n, DIFF_W),
        (sc1, sh1, g1, sc2, sh2), ln1_g[0], ln2_g[0], b_merge[0], w_gl,
        w_br_fox[0], w_br_diff[0], w_o[0], w_router[0], b_router[0], s, t)

    dest, blk_e, pends, nblk = _route(counts[:, 0].astype(I32), idx, rank, EXPERT_BLOCK)
    xs = _dispatch(pends, dest, h2, tr, EXPERT_BLOCK, nblk)
    ys = _experts(blk_e, pends, xs, w_up[0], b_up[0], w_down[0], b_down[0], EXPERT_BLOCK)
    out = _combine(dest, x1, gates, g2, final_g, ys, s, tr)
    return out.reshape(bsz, s, d)
```

```python
import functools
import math

import numpy as np
import jax
import jax.numpy as jnp
from jax import lax
from jax.experimental import pallas as pl
from jax.experimental.pallas import tpu as pltpu

F32 = jnp.float32
BF16 = jnp.bfloat16
I32 = jnp.int32

HEAD_DIM = 64
FOX_HEADS = 8
DIFF_HEADS = 4
FOX_W = FOX_HEADS * HEAD_DIM
DIFF_W = DIFF_HEADS * 2 * HEAD_DIM
T5_BUCKETS = 32
T5_MAX_DIST = 128
N_EXPERTS = 32
TOP_K = 4
EXPERT_BLOCK = 512
SWIGLU_ALPHA = 1.702
SWIGLU_LIMIT = 7.0
NORM_EPS = 1e-6
SUBLN_EPS = 1e-5
LAMBDA_INIT = 0.8 - 0.6 * math.exp(-0.3 * 0)

LANES = 128
NEG = -1e30
LOG2E = math.log2(math.e)
ONES_ROWS = 16
VMEM_LIMIT = 56 * 1024 * 1024

NT_DIMS = (((1,), (1,)), ((), ()))


def _dot(a, b):
    return jnp.dot(a, b, preferred_element_type=F32)


def _dot_nt(a, b):
    return lax.dot_general(a, b, NT_DIMS, preferred_element_type=F32)


def _split2(v):
    hi = v.astype(BF16)
    return hi, (v - hi.astype(F32)).astype(BF16)


def _dot_split(a, b, dot=_dot):
    a_hi, a_lo = _split2(a)
    b_hi, b_lo = _split2(b)
    return dot(a_hi, b_hi) + (dot(a_hi, b_lo) + dot(a_lo, b_hi))


def _params(*sem):
    return pltpu.CompilerParams(dimension_semantics=sem, vmem_limit_bytes=VMEM_LIMIT)


def _modulated_rmsnorm(x, g, scale, shift):
    y = x * lax.rsqrt(jnp.mean(x * x, axis=-1, keepdims=True) + NORM_EPS)
    return (y * g) * (1.0 + scale) + shift


ROW_TILE = 8


def _tiled_shape(n, d):
    assert d == ROW_TILE * LANES
    return (n * ROW_TILE, LANES)


def _row_tile(ref, r):
    return ref.at[pl.ds(pl.multiple_of(r * ROW_TILE, ROW_TILE), ROW_TILE)]


def _store_rows_tiled(ref, rows):
    n = rows.shape[0]
    for a in range(ROW_TILE):
        ref[pl.ds(a, n, stride=ROW_TILE), :] = rows[:, a * LANES:(a + 1) * LANES]


def _load_rows_tiled(ref):
    n = ref.shape[0] // ROW_TILE
    return jnp.concatenate([ref[pl.ds(a, n, stride=ROW_TILE), :] for a in range(ROW_TILE)], axis=1)


def _split3(v):
    hi = v.astype(BF16)
    r1 = v - hi.astype(F32)
    mid = r1.astype(BF16)
    lo = (r1 - mid.astype(F32)).astype(BF16)
    return hi, mid, lo


def _adaln_kernel(c_ref, w_ref, b_ref, o_ref):
    c = c_ref[...]
    s = c / (1.0 + jnp.exp(-c))
    o_ref[...] = _dot_split(s, w_ref[...]) + b_ref[...]


def _adaln_mod(c, w, b):
    bsz, d = c.shape
    n = w.shape[1]
    rows = 8
    tn = 1536
    c8 = jnp.zeros((rows, d), F32).at[:bsz].set(c)
    out = pl.pallas_call(
        _adaln_kernel,
        grid=(n // tn,),
        in_specs=[pl.BlockSpec((rows, d), lambda j: (0, 0)),
                  pl.BlockSpec((d, tn), lambda j: (0, j)),
                  pl.BlockSpec((1, tn), lambda j: (0, j))],
        out_specs=pl.BlockSpec((rows, tn), lambda j: (0, j)),
        out_shape=jax.ShapeDtypeStruct((rows, n), F32),
        compiler_params=_params("arbitrary"),
        name="adaln_mod",
    )(c8, w, b.reshape(1, n))
    return out[:bsz]


def _bias_tile_kernel(tb_ref, o_ref, *, t):
    h = pl.program_id(0)
    key = lax.broadcasted_iota(I32, (t, t), 0)
    qry = lax.broadcasted_iota(I32, (t, t), 1)
    far = tb_ref[h, T5_BUCKETS - 1]
    max_exact = T5_BUCKETS // 2
    for which in range(2):
        rel = qry - key + which * t
        n = jnp.maximum(rel, 0)
        nf = jnp.maximum(n, max_exact).astype(F32)
        large = max_exact + (jnp.log(nf / max_exact) / math.log(T5_MAX_DIST / max_exact)
                             * (T5_BUCKETS - max_exact)).astype(I32)
        large = jnp.minimum(large, T5_BUCKETS - 1)
        bucket = jnp.where(n < max_exact, n, large)
        bias = jnp.zeros((t, t), F32)
        for b in range(T5_BUCKETS):
            bias = jnp.where(bucket == b, tb_ref[h, b] - far, bias)
        o_ref[0, which] = jnp.where(rel >= 0, bias * LOG2E, NEG)


def _bias_tiles(t5_bias, t):
    tb = t5_bias.T.astype(F32)
    return pl.pallas_call(
        functools.partial(_bias_tile_kernel, t=t),
        grid=(DIFF_HEADS,),
        in_specs=[pl.BlockSpec(memory_space=pltpu.SMEM)],
        out_specs=pl.BlockSpec((1, 2, t, t), lambda h: (h, 0, 0, 0)),
        out_shape=jax.ShapeDtypeStruct((DIFF_HEADS, 2, t, t), F32),
        compiler_params=_params("arbitrary"),
        name="bias_tiles",
    )(tb)


def _fparts_perm():
    npair = FOX_HEADS // 2
    pk = np.zeros((3 * LANES, npair * LANES), np.float32)
    pq = np.zeros((npair * 32, 3 * LANES), np.float32)
    k_ones = np.zeros((1, npair * LANES), np.float32)
    q_ones = np.zeros((npair * 32, 1), np.float32)
    for p in range(npair):
        for hh in range(2):
            head = 2 * p + hh
            for part in range(3):
                pk[part * LANES + head, p * LANES + hh * 16 + 3 + part] = -1.0
                k_ones[0, p * LANES + hh * 16 + part] = 1.0
                pq[p * 32 + hh * 16 + part, part * LANES + head] = 1.0
                q_ones[p * 32 + hh * 16 + 3 + part, 0] = 1.0
    return pk, pq, k_ones, q_ones


def _in_proj_kernel(x_ref, sc_ref, sh_ref, g_ref, wt_ref, wn_ref, wfl_ref, bf_ref,
                    pk_ref, pq_ref, kone_ref, qone_ref,
                    qtf_ref, fqt_ref, kpf_ref, vtf_ref, qtd_ref, kd_ref, vtd_ref,
                    carry_ref, *, tm):
    i = pl.program_id(1)

    @pl.when(i == 0)
    def _():
        carry_ref[...] = jnp.zeros_like(carry_ref)

    h = _modulated_rmsnorm(x_ref[0], g_ref[...], sc_ref[0], sh_ref[0])
    hb = h.astype(BF16)

    h_lo = (h - hb.astype(F32)).astype(BF16)
    wfl_hi, wfl_lo = _split2(wfl_ref[...])
    fl = _dot(hb, wfl_hi) + (_dot(hb, wfl_lo) + _dot(h_lo, wfl_hi)) + bf_ref[...]
    logf = jnp.minimum(fl, 0.0) - jnp.log(1.0 + jnp.exp(-jnp.abs(fl)))
    r = lax.broadcasted_iota(I32, (tm, tm), 0)
    c = lax.broadcasted_iota(I32, (tm, tm), 1)
    tri = jnp.where(c <= r, 1.0, 0.0).astype(BF16)
    psum = _dot(tri, jnp.concatenate(_split3(logf), axis=1))
    fsum = (psum[:, 0:LANES] + (psum[:, LANES:2 * LANES] + psum[:, 2 * LANES:3 * LANES])
            + carry_ref[...])
    carry_ref[...] = fsum[tm - 1:tm, :]
    parts = jnp.concatenate(_split3(fsum * LOG2E), axis=1)

    scale = HEAD_DIM ** -0.5 * LOG2E
    yt = _dot_nt(wt_ref[...], hb)
    yn = _dot(hb, wn_ref[...])

    ones_rows = jnp.where(lax.broadcasted_iota(I32, (ONES_ROWS, tm), 0) == 0, 1.0, 0.0).astype(BF16)
    qtf_ref[0] = (yt[0:FOX_W] * scale).astype(BF16)
    vt = yt[FOX_W:2 * FOX_W].astype(BF16)
    for hd in range(FOX_HEADS):
        vtf_ref[0, hd, 0, 0:HEAD_DIM] = vt[hd * HEAD_DIM:(hd + 1) * HEAD_DIM]
        vtf_ref[0, hd, 0, HEAD_DIM:HEAD_DIM + ONES_ROWS] = ones_rows
    qtd_ref[0] = (yt[2 * FOX_W:2 * FOX_W + DIFF_W] * scale).astype(BF16)
    vtd = yt[2 * FOX_W + DIFF_W:2 * FOX_W + 2 * DIFF_W].astype(BF16)
    dv = 2 * HEAD_DIM
    for hd in range(DIFF_HEADS):
        vtd_ref[0, hd, 0, 0:dv] = vtd[hd * dv:(hd + 1) * dv]
        vtd_ref[0, hd, 0, dv:dv + ONES_ROWS] = ones_rows

    fqt_ref[0] = (_dot_nt(pq_ref[...], parts) + qone_ref[...]).astype(BF16)
    kext = (_dot(parts, pk_ref[...]) + kone_ref[...]).astype(BF16)
    kf = yn[:, 0:FOX_W].astype(BF16)
    for p in range(FOX_HEADS // 2):
        kpf_ref[0, :, 2 * p * LANES:(2 * p + 1) * LANES] = kf[:, p * LANES:(p + 1) * LANES]
        kpf_ref[0, :, (2 * p + 1) * LANES:(2 * p + 2) * LANES] = kext[:, p * LANES:(p + 1) * LANES]
    kd_ref[0] = yn[:, FOX_W:FOX_W + DIFF_W].astype(BF16)


def _in_proj(x, sc1, sh1, ln_g, w_in, b_fgate, t):
    bsz, s, d = x.shape
    nb = s // t
    cuts = np.cumsum([FOX_W, FOX_W, FOX_W, FOX_HEADS, DIFF_W, DIFF_W, DIFF_W])
    w_fq, w_fk, w_fv, w_fl, w_dq, w_dk, w_dv = [
        w_in[:, a:b] for a, b in zip([0, *cuts[:-1]], cuts)]
    wt = jnp.concatenate([w_fq, w_fv, w_dq, w_dv], axis=1).T.astype(BF16)
    wn = jnp.concatenate([w_fk, w_dk], axis=1).astype(BF16)
    wfl = jnp.zeros((d, LANES), F32).at[:, :FOX_HEADS].set(w_fl)
    bfl = jnp.zeros((1, LANES), F32).at[0, :FOX_HEADS].set(b_fgate)
    pk, pq, k_ones, q_ones = _fparts_perm()
    npair = FOX_HEADS // 2

    full = lambda shape: pl.BlockSpec(shape, lambda b, i: (0,) * len(shape))
    outs = pl.pallas_call(
        functools.partial(_in_proj_kernel, tm=t),
        grid=(bsz, nb),
        in_specs=[pl.BlockSpec((1, t, d), lambda b, i: (b, i, 0)),
                  pl.BlockSpec((1, 1, d), lambda b, i: (b, 0, 0)),
                  pl.BlockSpec((1, 1, d), lambda b, i: (b, 0, 0)),
                  full((1, d)), full(wt.shape), full(wn.shape), full(wfl.shape), full(bfl.shape),
                  full(pk.shape), full(pq.shape), full(k_ones.shape), full(q_ones.shape)],
        out_specs=[pl.BlockSpec((1, FOX_W, t), lambda b, i: (b, 0, i)),
                   pl.BlockSpec((1, npair * 32, t), lambda b, i: (b, 0, i)),
                   pl.BlockSpec((1, t, npair * 2 * LANES), lambda b, i: (b, i, 0)),
                   pl.BlockSpec((1, FOX_HEADS, 1, HEAD_DIM + ONES_ROWS, t),
                                lambda b, i: (b, 0, i, 0, 0)),
                   pl.BlockSpec((1, DIFF_W, t), lambda b, i: (b, 0, i)),
                   pl.BlockSpec((1, t, DIFF_W), lambda b, i: (b, i, 0)),
                   pl.BlockSpec((1, DIFF_HEADS, 1, 2 * HEAD_DIM + ONES_ROWS, t),
                                lambda b, i: (b, 0, i, 0, 0))],
        out_shape=[jax.ShapeDtypeStruct((bsz, FOX_W, s), BF16),
                   jax.ShapeDtypeStruct((bsz, npair * 32, s), BF16),
                   jax.ShapeDtypeStruct((bsz, s, npair * 2 * LANES), BF16),
                   jax.ShapeDtypeStruct((bsz, FOX_HEADS, nb, HEAD_DIM + ONES_ROWS, t), BF16),
                   jax.ShapeDtypeStruct((bsz, DIFF_W, s), BF16),
                   jax.ShapeDtypeStruct((bsz, s, DIFF_W), BF16),
                   jax.ShapeDtypeStruct((bsz, DIFF_HEADS, nb, 2 * HEAD_DIM + ONES_ROWS, t), BF16)],
        scratch_shapes=[pltpu.VMEM((1, LANES), F32)],
        compiler_params=_params("arbitrary", "arbitrary"),
        name="in_proj",
    )(x, sc1.reshape(bsz, 1, d), sh1.reshape(bsz, 1, d), ln_g.reshape(1, d), wt, wn, wfl, bfl,
      jnp.asarray(pk, BF16), jnp.asarray(pq, BF16), jnp.asarray(k_ones), jnp.asarray(q_ones))
    return outs


def _softmax_block(s, vt, m_ref, acc_ref, slot):
    m_old = m_ref[slot]
    m_new = jnp.maximum(m_old, jnp.max(s, axis=0, keepdims=True))
    alpha = jnp.exp2(m_old - m_new)
    p = jnp.exp2(s - m_new)
    acc_ref[slot] = alpha * acc_ref[slot] + _dot(vt, p.astype(BF16))
    m_ref[slot] = m_new


def _init_softmax_state(m_ref, acc_ref):
    m_ref[...] = jnp.full(m_ref.shape, NEG, F32)
    acc_ref[...] = jnp.zeros(acc_ref.shape, F32)


SCAN_BLOCKS_PER_TRIP = 4


def _scan_key_blocks(i, n_tail, qk_into, process, bufs):
    n_plain = jnp.maximum(i + 1 - n_tail, 0)
    n_trips = n_plain // SCAN_BLOCKS_PER_TRIP
    rem = n_plain % SCAN_BLOCKS_PER_TRIP
    qk_into(bufs[0], 0)

    def trip(n, carry):
        j = SCAN_BLOCKS_PER_TRIP * n
        for m in range(SCAN_BLOCKS_PER_TRIP):
            qk_into(bufs[(m + 1) % 2], j + m + 1)
            process(bufs[m % 2], j + m, None)
        return carry

    lax.fori_loop(0, n_trips, trip, 0)
    first = SCAN_BLOCKS_PER_TRIP * n_trips

    def tail(kinds):
        for n, kind in enumerate(kinds):
            if n + 1 < len(kinds):
                qk_into(bufs[(n + 1) % 2], first + n + 1)
            process(bufs[n % 2], first + n, kind)

    for n_last in range(1, n_tail + 1):
        kinds = list(range(n_last - 1, -1, -1))
        if n_last < n_tail:
            pl.when(i + 1 == n_last)(functools.partial(tail, kinds))
        else:
            for r in range(SCAN_BLOCKS_PER_TRIP):
                pl.when((i + 1 >= n_tail) & (rem == r))(functools.partial(tail, [None] * r + kinds))


def _fox_kernel(q_ref, fq_ref, k_ref, v_ref, o_ref, qbd_ref, sa_ref, sb_ref, m_ref, acc_ref, *, t):
    i = pl.program_id(2)
    d = HEAD_DIM
    qbd_ref[...] = jnp.zeros(qbd_ref.shape, BF16)
    qbd_ref[0:d, 0:t] = q_ref[0, 0:d, :]
    qbd_ref[d:2 * d, t:2 * t] = q_ref[0, d:2 * d, :]
    qbd_ref[2 * d:2 * d + 16, 0:t] = fq_ref[0, 0:16, :]
    qbd_ref[2 * d + 16:2 * d + 32, t:2 * t] = fq_ref[0, 16:32, :]
    _init_softmax_state(m_ref, acc_ref)

    def qk_into(s_ref, j):
        kblk = k_ref[0, pl.ds(pl.multiple_of(j * t, t), t), :]
        s_ref[...] = _dot(kblk, qbd_ref[...])

    def process(s_ref, j, kind):
        for hh in range(2):
            sh = s_ref[:, hh * t:(hh + 1) * t]
            if kind == 0:
                key = lax.broadcasted_iota(I32, (t, t), 0)
                qry = lax.broadcasted_iota(I32, (t, t), 1)
                sh = jnp.where(key <= qry, sh, NEG)
            _softmax_block(sh, v_ref[0, hh, j], m_ref, acc_ref, hh)

    _scan_key_blocks(i, 1, qk_into, process, (sa_ref, sb_ref))

    outs = [acc_ref[hh, 0:d] / acc_ref[hh, d:d + 1] for hh in range(2)]
    o_ref[0] = jnp.concatenate(outs, axis=0).T.astype(BF16)


def _fox_attention(qt, fqt, kp, vt, t):
    bsz, _, s = qt.shape
    nb = s // t
    npair = FOX_HEADS // 2
    return pl.pallas_call(
        functools.partial(_fox_kernel, t=t),
        grid=(bsz, npair, nb),
        in_specs=[pl.BlockSpec((1, 2 * HEAD_DIM, t), lambda b, p, i: (b, p, i)),
                  pl.BlockSpec((1, 32, t), lambda b, p, i: (b, p, i)),
                  pl.BlockSpec((1, s, 2 * LANES), lambda b, p, i: (b, 0, p)),
                  pl.BlockSpec((1, 2, nb, HEAD_DIM + ONES_ROWS, t), lambda b, p, i: (b, p, 0, 0, 0))],
        out_specs=pl.BlockSpec((1, t, 2 * HEAD_DIM), lambda b, p, i: (b, i, p)),
        out_shape=jax.ShapeDtypeStruct((bsz, s, FOX_W), BF16),
        scratch_shapes=[pltpu.VMEM((2 * LANES, 2 * t), BF16),
                        pltpu.VMEM((t, 2 * t), F32),
                        pltpu.VMEM((t, 2 * t), F32),
                        pltpu.VMEM((2, 1, t), F32),
                        pltpu.VMEM((2, HEAD_DIM + ONES_ROWS, t), F32)],
        compiler_params=_params("arbitrary", "arbitrary", "arbitrary"),
        name="fox_attn",
    )(qt, fqt, kp, vt)


def _diff_kernel(lam_ref, q_ref, k_ref, v_ref, bias_ref, g_ref, o_ref,
                 qbd_ref, sa_ref, sb_ref, m_ref, acc_ref, *, t):
    i = pl.program_id(2)
    d = HEAD_DIM
    dv = 2 * HEAD_DIM
    qbd_ref[...] = jnp.zeros(qbd_ref.shape, BF16)
    qbd_ref[0:d, 0:t] = q_ref[0, 0:d, :]
    qbd_ref[d:2 * d, t:2 * t] = q_ref[0, d:2 * d, :]
    _init_softmax_state(m_ref, acc_ref)

    def qk_into(s_ref, j):
        kblk = k_ref[0, pl.ds(pl.multiple_of(j * t, t), t), :]
        s_ref[...] = _dot(kblk, qbd_ref[...])

    def process(s_ref, j, kind):
        vt = v_ref[0, 0, j]
        for hh in range(2):
            sh = s_ref[:, hh * t:(hh + 1) * t]
            if kind is not None:
                sh = sh + bias_ref[0, kind]
            _softmax_block(sh, vt, m_ref, acc_ref, hh)

    _scan_key_blocks(i, 2, qk_into, process, (sa_ref, sb_ref))

    lam = (jnp.exp(jnp.sum(lam_ref[0:1, :] * lam_ref[1:2, :], axis=1, keepdims=True))
           - jnp.exp(jnp.sum(lam_ref[2:3, :] * lam_ref[3:4, :], axis=1, keepdims=True))
           + LAMBDA_INIT)
    out = (acc_ref[0, 0:dv] / acc_ref[0, dv:dv + 1]
           - lam * (acc_ref[1, 0:dv] / acc_ref[1, dv:dv + 1]))
    ms = jnp.mean(out * out, axis=0, keepdims=True)
    out = out * lax.rsqrt(ms + SUBLN_EPS) * g_ref[...] * (1.0 - LAMBDA_INIT)
    o_ref[0] = out.T.astype(BF16)


def _diff_attention(lam_vecs, qt, k, vt, bias_tiles, norm_g, t):
    bsz, _, s = qt.shape
    nb = s // t
    dv = 2 * HEAD_DIM
    return pl.pallas_call(
        functools.partial(_diff_kernel, t=t),
        grid=(bsz, DIFF_HEADS, nb),
        in_specs=[pl.BlockSpec((4, HEAD_DIM), lambda b, h, i: (0, 0)),
                  pl.BlockSpec((1, dv, t), lambda b, h, i: (b, h, i)),
                  pl.BlockSpec((1, s, dv), lambda b, h, i: (b, 0, h)),
                  pl.BlockSpec((1, 1, nb, dv + ONES_ROWS, t), lambda b, h, i: (b, h, 0, 0, 0)),
                  pl.BlockSpec((1, 2, t, t), lambda b, h, i: (h, 0, 0, 0)),
                  pl.BlockSpec((dv, 1), lambda b, h, i: (0, 0))],
        out_specs=pl.BlockSpec((1, t, dv), lambda b, h, i: (b, i, h)),
        out_shape=jax.ShapeDtypeStruct((bsz, s, DIFF_W), BF16),
        scratch_shapes=[pltpu.VMEM((dv, 2 * t), BF16),
                        pltpu.VMEM((t, 2 * t), F32),
                        pltpu.VMEM((t, 2 * t), F32),
                        pltpu.VMEM((2, 1, t), F32),
                        pltpu.VMEM((2, dv + ONES_ROWS, t), F32)],
        compiler_params=_params("arbitrary", "arbitrary", "arbitrary"),
        name="diff_attn",
    )(lam_vecs, qt, k, vt, bias_tiles, norm_g.reshape(dv, 1))


def _post_attn_kernel(x_ref, yf_ref, yd_ref, sc1_ref, sh1_ref, g1_ref, sc2_ref, sh2_ref,
                      ln1_ref, ln2_ref, wgl_ref, bm_ref, wbf_ref, wbd_ref, wo_ref, wr_ref, br_ref,
                      x1_ref, h2_ref, idx_ref, rank_ref, gate_ref, cnt_ref, carry_ref, *, tm):
    i = pl.program_id(0)

    @pl.when(i == 0)
    def _():
        carry_ref[...] = jnp.zeros_like(carry_ref)

    d = x_ref.shape[1]
    x = x_ref[...]
    h1 = _modulated_rmsnorm(x, ln1_ref[...], sc1_ref[0], sh1_ref[0]).astype(BF16)
    gl = _dot(h1, wgl_ref[...]) + bm_ref[...]
    gates = 1.0 / (1.0 + jnp.exp(-gl))
    merged = (gates[:, :d] * _dot(yf_ref[...], wbf_ref[...])
              + gates[:, d:] * _dot(yd_ref[...], wbd_ref[...]))
    x1 = x + g1_ref[0] * _dot(merged.astype(BF16), wo_ref[...])
    x1_ref[...] = x1
    h2 = _modulated_rmsnorm(x1, ln2_ref[...], sc2_ref[0], sh2_ref[0])
    _store_rows_tiled(h2_ref, h2)

    logits = _dot_split(wr_ref[...], h2, _dot_nt) + br_ref[...]
    row = lax.broadcasted_iota(I32, (N_EXPERTS, tm), 0)
    cur = logits
    vals, idxs = [], []
    for _ in range(TOP_K):
        mx = jnp.max(cur, axis=0, keepdims=True)
        ik = jnp.min(jnp.where(cur == mx, row, N_EXPERTS), axis=0, keepdims=True)
        vals.append(mx)
        idxs.append(ik)
        cur = jnp.where(row == ik, -jnp.inf, cur)
    exps = [jnp.exp(v - vals[0]) for v in vals]
    denom = exps[0] + exps[1] + exps[2] + exps[3]
    gate_rows = [e / denom for e in exps]

    onehots = [row == ik for ik in idxs]
    cnt = jnp.zeros((N_EXPERTS, tm), F32)
    for oh in onehots:
        cnt = cnt + jnp.where(oh, 1.0, 0.0)
    r = lax.broadcasted_iota(I32, (tm, tm), 0)
    c = lax.broadcasted_iota(I32, (tm, tm), 1)
    before = jnp.where(r < c, 1.0, 0.0).astype(BF16)
    prior = carry_ref[:, 0:1] + _dot(cnt.astype(BF16), before)
    for k in range(TOP_K):
        rk = jnp.sum(jnp.where(onehots[k], prior, 0.0), axis=0, keepdims=True)
        rank_ref[k:k + 1, :] = rk.astype(I32)
        idx_ref[k:k + 1, :] = idxs[k]
    carry_ref[...] = carry_ref[...] + jnp.sum(cnt, axis=1, keepdims=True)
    cnt_ref[...] = carry_ref[...]

    lrow = lax.broadcasted_iota(I32, (LANES, tm), 0)
    g_t = jnp.zeros((LANES, tm), F32)
    for k in range(TOP_K):
        g_t = jnp.where(lrow == k, gate_rows[k], g_t)
    gate_ref[...] = g_t.T


def _post_attn(x2, yf, yd, mods, ln1_g, ln2_g, b_merge, w_gl, w_brf, w_brd, w_o, w_r, b_r, s, tm):
    n, d = x2.shape
    nbb = s // tm
    bsz = n // s
    sc1, sh1, g1, sc2, sh2 = [m.reshape(bsz, 1, d) for m in mods]
    row_spec = lambda w: pl.BlockSpec((tm, w), lambda i: (i, 0))
    mod_spec = pl.BlockSpec((1, 1, d), lambda i: (i // nbb, 0, 0))
    full = lambda shape: pl.BlockSpec(shape, lambda i: (0,) * len(shape))
    tok_spec = pl.BlockSpec((TOP_K, tm), lambda i: (0, i))
    return pl.pallas_call(
        functools.partial(_post_attn_kernel, tm=tm),
        grid=(n // tm,),
        in_specs=[row_spec(d), row_spec(FOX_W), row_spec(DIFF_W),
                  mod_spec, mod_spec, mod_spec, mod_spec, mod_spec,
                  full((1, d)), full((1, d)), full(w_gl.shape), full((1, 2 * d)),
                  full(w_brf.shape), full(w_brd.shape), full(w_o.shape),
                  full((N_EXPERTS, d)), full((N_EXPERTS, 1))],
        out_specs=[row_spec(d), pl.BlockSpec(_tiled_shape(tm, d), lambda i: (i, 0)),
                   tok_spec, tok_spec, row_spec(LANES), full((N_EXPERTS, LANES))],
        out_shape=[jax.ShapeDtypeStruct((n, d), F32), jax.ShapeDtypeStruct(_tiled_shape(n, d), F32),
                   jax.ShapeDtypeStruct((TOP_K, n), I32), jax.ShapeDtypeStruct((TOP_K, n), I32),
                   jax.ShapeDtypeStruct((n, LANES), F32),
                   jax.ShapeDtypeStruct((N_EXPERTS, LANES), F32)],
        scratch_shapes=[pltpu.VMEM((N_EXPERTS, LANES), F32)],
        compiler_params=_params("arbitrary"),
        name="post_attn",
    )(x2, yf, yd, sc1, sh1, g1, sc2, sh2, ln1_g.reshape(1, d), ln2_g.reshape(1, d),
      w_gl.astype(BF16), b_merge.reshape(1, 2 * d), w_brf.astype(BF16), w_brd.astype(BF16),
      w_o.astype(BF16), w_r.T, b_r.reshape(N_EXPERTS, 1))


def _route_kernel(cnt_ref, idx_ref, rank_ref, dest_ref, be_ref, pend_ref, *, blk, nblk_lanes):
    shift = blk.bit_length() - 1
    idx = idx_ref[...]
    start_of = jnp.zeros(idx.shape, I32)
    blk_start = lax.broadcasted_iota(I32, (1, nblk_lanes), 1) * blk
    lane = lax.broadcasted_iota(I32, (1, LANES), 1)
    blk_e = jnp.zeros((1, nblk_lanes), I32)
    pends = jnp.zeros((1, LANES), I32)
    pend = jnp.int32(0)
    for e in range(N_EXPERTS):
        padded = ((cnt_ref[e] + (blk - 1)) >> shift) << shift
        start_of = jnp.where(idx == e, pend, start_of)
        pend = pend + padded
        blk_e = blk_e + jnp.where(blk_start >= pend, 1, 0)
        pends = jnp.where(lane == e, pend, pends)
    dest_ref[...] = start_of + rank_ref[...]
    be_ref[...] = jnp.minimum(blk_e, N_EXPERTS - 1)
    pend_ref[...] = pends


def _route(counts, idx, rank, blk):
    n = idx.shape[1]
    nblk = (n * TOP_K + blk - 1) // blk + N_EXPERTS
    nblk_lanes = -(-nblk // LANES) * LANES
    dest, blk_e, pends = pl.pallas_call(
        functools.partial(_route_kernel, blk=blk, nblk_lanes=nblk_lanes),
        in_specs=[pl.BlockSpec(memory_space=pltpu.SMEM),
                  pl.BlockSpec(memory_space=pltpu.VMEM),
                  pl.BlockSpec(memory_space=pltpu.VMEM)],
        out_specs=[pl.BlockSpec(memory_space=pltpu.VMEM)] * 3,
        out_shape=[jax.ShapeDtypeStruct((TOP_K, n), I32),
                   jax.ShapeDtypeStruct((1, nblk_lanes), I32),
                   jax.ShapeDtypeStruct((1, LANES), I32)],
        compiler_params=pltpu.CompilerParams(vmem_limit_bytes=VMEM_LIMIT),
        name="route",
    )(counts, idx, rank)
    return dest, blk_e[0, :nblk], pends[0, :N_EXPERTS], nblk


def _dispatch_kernel(pend_ref, dest_ref, h_ref, xs_ref, zero_ref, sem, *, tm, blk, nblk):
    shift = blk.bit_length() - 1

    @pl.when(pl.program_id(0) == 0)
    def _():
        zero_ref[...] = jnp.zeros(zero_ref.shape, F32)

        def zero_block(b):
            lines = blk * ROW_TILE
            return pltpu.make_async_copy(
                zero_ref, xs_ref.at[pl.ds(pl.multiple_of(b * lines, lines), lines)], sem)

        def tails(fn):
            for e in range(N_EXPERTS):
                prev = pend_ref[e - 1] if e else 0

                @pl.when(pend_ref[e] > prev)
                def _():
                    fn(zero_block((pend_ref[e] >> shift) - 1))

        def rest(fn):
            def body(b, carry):
                fn(zero_block(b))
                return carry
            lax.fori_loop(pend_ref[N_EXPERTS - 1] >> shift, nblk, body, 0)

        tails(lambda cp: cp.start())
        rest(lambda cp: cp.start())
        tails(lambda cp: cp.wait())
        rest(lambda cp: cp.wait())

    def row_copy(t, k):
        return pltpu.make_async_copy(_row_tile(h_ref, t), _row_tile(xs_ref, dest_ref[k, t]), sem)

    def start(t, carry):
        for k in range(TOP_K):
            row_copy(t, k).start(priority=k % 2)
        return carry

    def wait(t, carry):
        for k in range(TOP_K):
            row_copy(t, k).wait()
        return carry

    lax.fori_loop(0, tm, start, 0, unroll=ROW_DMA_UNROLL)
    lax.fori_loop(0, tm, wait, 0, unroll=ROW_DMA_UNROLL)


def _dispatch(pends, dest, h2, tm, blk, nblk):
    n = h2.shape[0] // ROW_TILE
    d = ROW_TILE * LANES
    return pl.pallas_call(
        functools.partial(_dispatch_kernel, tm=tm, blk=blk, nblk=nblk),
        grid=(n // tm,),
        in_specs=[pl.BlockSpec(memory_space=pltpu.SMEM),
                  pl.BlockSpec((TOP_K, tm), lambda i: (0, i), memory_space=pltpu.SMEM),
                  pl.BlockSpec(_tiled_shape(tm, d), lambda i: (i, 0))],
        out_specs=pl.BlockSpec(memory_space=pl.ANY),
        out_shape=jax.ShapeDtypeStruct(_tiled_shape(nblk * blk, d), F32),
        scratch_shapes=[pltpu.VMEM(_tiled_shape(blk, d), F32), pltpu.SemaphoreType.DMA(())],
        compiler_params=_params("arbitrary"),
        name="dispatch",
    )(pends, dest, h2)


ROW_DMA_UNROLL = 4
UP_CHUNK = 2048
PICK_GROUP = 512


def _expert_kernel(be_ref, pend_ref, x_ref, wu_ref, bu_ref, wd_ref, bd_ref, o_ref,
                   wu_sc, wd_sc, *, blk):
    b = pl.program_id(0)
    live = b * blk < pend_ref[N_EXPERTS - 1]
    new_expert = (b == 0) | (be_ref[b] != be_ref[jnp.maximum(b - 1, 0)])

    @pl.when(live & new_expert)
    def _():
        wu_sc[...] = wu_ref[0].astype(BF16)
        wd_sc[...] = wd_ref[0].astype(BF16)

    @pl.when(live)
    def _():
        x = _load_rows_tiled(x_ref).astype(BF16)
        r = lax.broadcasted_iota(I32, (PICK_GROUP, PICK_GROUP // 2), 0)
        c = lax.broadcasted_iota(I32, (PICK_GROUP, PICK_GROUP // 2), 1)
        even = jnp.where(r == 2 * c, 1.0, 0.0).astype(BF16)
        acts = []
        for n in range(wu_sc.shape[1] // UP_CHUNK):
            cols = slice(n * UP_CHUNK, (n + 1) * UP_CHUNK)
            u = _dot(x, wu_sc[:, cols]) + bu_ref[0, :, cols]
            u_glu = jnp.minimum(u, SWIGLU_LIMIT)
            u_lin = jnp.clip(u, -SWIGLU_LIMIT, SWIGLU_LIMIT) + 1.0
            z = (u_glu * (1.0 / (1.0 + jnp.exp(-SWIGLU_ALPHA * u_glu)))
                 * pltpu.roll(u_lin, UP_CHUNK - 1, 1)).astype(BF16)
            for g in range(UP_CHUNK // PICK_GROUP):
                acts.append(_dot(z[:, g * PICK_GROUP:(g + 1) * PICK_GROUP], even).astype(BF16))
        _store_rows_tiled(o_ref, _dot(jnp.concatenate(acts, axis=1), wd_sc[...]) + bd_ref[0])

    @pl.when(jnp.logical_not(live))
    def _():
        o_ref[...] = jnp.zeros(o_ref.shape, F32)


def _experts(blk_e, pends, xs, w_up, b_up, w_down, b_down, blk):
    p = xs.shape[0] // ROW_TILE
    ne, d, f2 = w_up.shape
    f = w_down.shape[1]
    shift = blk.bit_length() - 1

    def live(b, be, pe):
        return jnp.minimum(b, (pe[N_EXPERTS - 1] >> shift) - 1)

    x_spec = pl.BlockSpec(_tiled_shape(blk, d), lambda b, be, pe: (live(b, be, pe), 0))
    w_spec = lambda r, c: pl.BlockSpec((1, r, c), lambda b, be, pe: (be[live(b, be, pe)], 0, 0))
    return pl.pallas_call(
        functools.partial(_expert_kernel, blk=blk),
        grid_spec=pltpu.PrefetchScalarGridSpec(
            num_scalar_prefetch=2,
            grid=(p // blk,),
            in_specs=[x_spec, w_spec(d, f2), w_spec(1, f2), w_spec(f, d), w_spec(1, d)],
            out_specs=pl.BlockSpec(_tiled_shape(blk, d), lambda b, be, pe: (b, 0)),
            scratch_shapes=[pltpu.VMEM((d, f2), BF16), pltpu.VMEM((f, d), BF16)]),
        out_shape=jax.ShapeDtypeStruct(_tiled_shape(p, d), F32),
        compiler_params=_params("arbitrary"),
        name="experts",
    )(blk_e, pends, xs, w_up, b_up.reshape(ne, 1, f2), w_down, b_down.reshape(ne, 1, d))


def _combine_kernel(dest_ref, x1_ref, gate_ref, g2_ref, fg_ref, ys_ref, o_ref, ybuf_ref, sem, *, tm):
    def row_copy(t, k):
        return pltpu.make_async_copy(_row_tile(ys_ref, dest_ref[k, t]), _row_tile(ybuf_ref.at[k], t), sem)

    def start(t, carry):
        for k in range(TOP_K):
            row_copy(t, k).start(priority=k % 2)
        return carry

    def wait(t, carry):
        for k in range(TOP_K):
            row_copy(t, k).wait()
        return carry

    lax.fori_loop(0, tm, start, 0, unroll=ROW_DMA_UNROLL)
    lax.fori_loop(0, tm, wait, 0, unroll=ROW_DMA_UNROLL)

    g = gate_ref[...]
    moe = g[:, 0:1] * _load_rows_tiled(ybuf_ref.at[0])
    for k in range(1, TOP_K):
        moe = moe + g[:, k:k + 1] * _load_rows_tiled(ybuf_ref.at[k])
    xo = x1_ref[...] + g2_ref[0] * moe
    o_ref[...] = xo * lax.rsqrt(jnp.mean(xo * xo, axis=-1, keepdims=True) + NORM_EPS) * fg_ref[...]


def _combine(dest, x1, gates, g2, final_g, ys, s, tm):
    n, d = x1.shape
    nbb = s // tm
    bsz = n // s
    return pl.pallas_call(
        functools.partial(_combine_kernel, tm=tm),
        grid=(n // tm,),
        in_specs=[pl.BlockSpec((TOP_K, tm), lambda i: (0, i), memory_space=pltpu.SMEM),
                  pl.BlockSpec((tm, d), lambda i: (i, 0)),
                  pl.BlockSpec((tm, LANES), lambda i: (i, 0)),
                  pl.BlockSpec((1, 1, d), lambda i: (i // nbb, 0, 0)),
                  pl.BlockSpec((1, d), lambda i: (0, 0)),
                  pl.BlockSpec(memory_space=pl.ANY)],
        out_specs=pl.BlockSpec((tm, d), lambda i: (i, 0)),
        out_shape=jax.ShapeDtypeStruct((n, d), F32),
        scratch_shapes=[pltpu.VMEM((TOP_K, *_tiled_shape(tm, d)), F32), pltpu.SemaphoreType.DMA(())],
        compiler_params=_params("arbitrary"),
        name="combine",
    )(dest, x1, gates, g2.reshape(bsz, 1, d), final_g.reshape(1, d), ys)


def kernel(x, c, w_ada, b_ada, ln1_g, w_in, b_fgate, b_merge, lam_q1, lam_k1, lam_q2, lam_k2,
           diff_norm_g, t5_bias, w_br_fox, w_br_diff, w_o, ln2_g, w_router, b_router,
           w_up, b_up, w_down, b_down, final_g):
    bsz, s, d = x.shape
    n = bsz * s
    t = min(512, s)
    tr = min(256, s)
    assert s % t == 0 and w_ada.shape[0] == 1

    mod = _adaln_mod(c, w_ada[0], b_ada[0])
    sh1, sc1, g1, sh2, sc2, g2 = jnp.split(mod, 6, axis=-1)

    qtf, fqt, kpf, vtf, qtd, kd, vtd = _in_proj(x, sc1, sh1, ln1_g[0], w_in[0], b_fgate[0], t)
    y_fox = _fox_attention(qtf, fqt, kpf, vtf, t)
    lam_vecs = jnp.stack([lam_q1[0], lam_k1[0], lam_q2[0], lam_k2[0]])
    y_diff = _diff_attention(lam_vecs, qtd, kd, vtd, _bias_tiles(t5_bias, t), diff_norm_g[0], t)

    w_gl = w_in[0][:, w_in.shape[2] - 2 * d:]
    x1, h2, idx, rank, gates, counts = _post_attn(
        x.reshape(n, d), y_fox.reshape(n, FOX_W), y_diff.reshape(n, DIFF_W),
        (sc1, sh1, g1, sc2, sh2), ln1_g[0], ln2_g[0], b_merge[0], w_gl,
        w_br_fox[0], w_br_diff[0], w_o[0], w_router[0], b_router[0], s, t)

    dest, blk_e, pends, nblk = _route(counts[:, 0].astype(I32), idx, rank, EXPERT_BLOCK)
    xs = _dispatch(pends, dest, h2, tr, EXPERT_BLOCK, nblk)
    ys = _experts(blk_e, pends, xs, w_up[0], b_up[0], w_down[0], b_down[0], EXPERT_BLOCK)
    out = _combine(dest, x1, gates, g2, final_g, ys, s, tr)
    return out.reshape(bsz, s, d)
```

```python
import functools
import math

import numpy as np
import jax
import jax.numpy as jnp
from jax import lax
from jax.experimental import pallas as pl
from jax.experimental.pallas import tpu as pltpu

F32 = jnp.float32
BF16 = jnp.bfloat16
I32 = jnp.int32

HEAD_DIM = 64
FOX_HEADS = 8
DIFF_HEADS = 4
FOX_W = FOX_HEADS * HEAD_DIM
DIFF_W = DIFF_HEADS * 2 * HEAD_DIM
T5_BUCKETS = 32
T5_MAX_DIST = 128
N_EXPERTS = 32
TOP_K = 4
EXPERT_BLOCK = 512
SWIGLU_ALPHA = 1.702
SWIGLU_LIMIT = 7.0
NORM_EPS = 1e-6
SUBLN_EPS = 1e-5
LAMBDA_INIT = 0.8 - 0.6 * math.exp(-0.3 * 0)

LANES = 128
NEG = -1e30
LOG2E = math.log2(math.e)
ONES_ROWS = 16
VMEM_LIMIT = 56 * 1024 * 1024

NT_DIMS = (((1,), (1,)), ((), ()))


def _dot(a, b):
    return jnp.dot(a, b, preferred_element_type=F32)


def _dot_nt(a, b):
    return lax.dot_general(a, b, NT_DIMS, preferred_element_type=F32)


def _split2(v):
    hi = v.astype(BF16)
    return hi, (v - hi.astype(F32)).astype(BF16)


def _dot_split(a, b, dot=_dot):
    a_hi, a_lo = _split2(a)
    b_hi, b_lo = _split2(b)
    return dot(a_hi, b_hi) + (dot(a_hi, b_lo) + dot(a_lo, b_hi))


def _params(*sem):
    return pltpu.CompilerParams(dimension_semantics=sem, vmem_limit_bytes=VMEM_LIMIT)


def _modulated_rmsnorm(x, g, scale, shift):
    y = x * lax.rsqrt(jnp.mean(x * x, axis=-1, keepdims=True) + NORM_EPS)
    return (y * g) * (1.0 + scale) + shift


ROW_TILE = 8


def _tiled_shape(n, d):
    assert d == ROW_TILE * LANES
    return (n * ROW_TILE, LANES)


def _row_tile(ref, r):
    return ref.at[pl.ds(pl.multiple_of(r * ROW_TILE, ROW_TILE), ROW_TILE)]


def _store_rows_tiled(ref, rows):
    n = rows.shape[0]
    for a in range(ROW_TILE):
        ref[pl.ds(a, n, stride=ROW_TILE), :] = rows[:, a * LANES:(a + 1) * LANES]


def _load_rows_tiled(ref):
    n = ref.shape[0] // ROW_TILE
    return jnp.concatenate([ref[pl.ds(a, n, stride=ROW_TILE), :] for a in range(ROW_TILE)], axis=1)


def _split3(v):
    hi = v.astype(BF16)
    r1 = v - hi.astype(F32)
    mid = r1.astype(BF16)
    lo = (r1 - mid.astype(F32)).astype(BF16)
    return hi, mid, lo


def _adaln_kernel(c_ref, w_ref, b_ref, o_ref):
    c = c_ref[...]
    s = c / (1.0 + jnp.exp(-c))
    o_ref[...] = _dot_split(s, w_ref[...]) + b_ref[...]


def _adaln_mod(c, w, b):
    bsz, d = c.shape
    n = w.shape[1]
    rows = 8
    tn = 1536
    c8 = jnp.zeros((rows, d), F32).at[:bsz].set(c)
    out = pl.pallas_call(
        _adaln_kernel,
        grid=(n // tn,),
        in_specs=[pl.BlockSpec((rows, d), lambda j: (0, 0)),
                  pl.BlockSpec((d, tn), lambda j: (0, j)),
                  pl.BlockSpec((1, tn), lambda j: (0, j))],
        out_specs=pl.BlockSpec((rows, tn), lambda j: (0, j)),
        out_shape=jax.ShapeDtypeStruct((rows, n), F32),
        compiler_params=_params("arbitrary"),
        name="adaln_mod",
    )(c8, w, b.reshape(1, n))
    return out[:bsz]


def _bias_tile_kernel(tb_ref, o_ref, *, t):
    h = pl.program_id(0)
    key = lax.broadcasted_iota(I32, (t, t), 0)
    qry = lax.broadcasted_iota(I32, (t, t), 1)
    far = tb_ref[h, T5_BUCKETS - 1]
    max_exact = T5_BUCKETS // 2
    for which in range(2):
        rel = qry - key + which * t
        n = jnp.maximum(rel, 0)
        nf = jnp.maximum(n, max_exact).astype(F32)
        large = max_exact + (jnp.log(nf / max_exact) / math.log(T5_MAX_DIST / max_exact)
                             * (T5_BUCKETS - max_exact)).astype(I32)
        large = jnp.minimum(large, T5_BUCKETS - 1)
        bucket = jnp.where(n < max_exact, n, large)
        bias = jnp.zeros((t, t), F32)
        for b in range(T5_BUCKETS):
            bias = jnp.where(bucket == b, tb_ref[h, b] - far, bias)
        o_ref[0, which] = jnp.where(rel >= 0, bias * LOG2E, NEG)


def _bias_tiles(t5_bias, t):
    tb = t5_bias.T.astype(F32)
    return pl.pallas_call(
        functools.partial(_bias_tile_kernel, t=t),
        grid=(DIFF_HEADS,),
        in_specs=[pl.BlockSpec(memory_space=pltpu.SMEM)],
        out_specs=pl.BlockSpec((1, 2, t, t), lambda h: (h, 0, 0, 0)),
        out_shape=jax.ShapeDtypeStruct((DIFF_HEADS, 2, t, t), F32),
        compiler_params=_params("arbitrary"),
        name="bias_tiles",
    )(tb)


def _fparts_perm():
    npair = FOX_HEADS // 2
    pk = np.zeros((3 * LANES, npair * LANES), np.float32)
    pq = np.zeros((npair * 32, 3 * LANES), np.float32)
    k_ones = np.zeros((1, npair * LANES), np.float32)
    q_ones = np.zeros((npair * 32, 1), np.float32)
    for p in range(npair):
        for hh in range(2):
            head = 2 * p + hh
            for part in range(3):
                pk[part * LANES + head, p * LANES + hh * 16 + 3 + part] = -1.0
                k_ones[0, p * LANES + hh * 16 + part] = 1.0
                pq[p * 32 + hh * 16 + part, part * LANES + head] = 1.0
                q_ones[p * 32 + hh * 16 + 3 + part, 0] = 1.0
    return pk, pq, k_ones, q_ones


def _in_proj_kernel(x_ref, sc_ref, sh_ref, g_ref, wt_ref, wn_ref, wfl_ref, bf_ref,
                    pk_ref, pq_ref, kone_ref, qone_ref,
                    qtf_ref, fqt_ref, kpf_ref, vtf_ref, qtd_ref, kd_ref, vtd_ref,
                    carry_ref, *, tm):
    i = pl.program_id(1)

    @pl.when(i == 0)
    def _():
        carry_ref[...] = jnp.zeros_like(carry_ref)

    h = _modulated_rmsnorm(x_ref[0], g_ref[...], sc_ref[0], sh_ref[0])
    hb = h.astype(BF16)

    h_lo = (h - hb.astype(F32)).astype(BF16)
    wfl_hi, wfl_lo = _split2(wfl_ref[...])
    fl = _dot(hb, wfl_hi) + (_dot(hb, wfl_lo) + _dot(h_lo, wfl_hi)) + bf_ref[...]
    logf = jnp.minimum(fl, 0.0) - jnp.log(1.0 + jnp.exp(-jnp.abs(fl)))
    r = lax.broadcasted_iota(I32, (tm, tm), 0)
    c = lax.broadcasted_iota(I32, (tm, tm), 1)
    tri = jnp.where(c <= r, 1.0, 0.0).astype(BF16)
    psum = _dot(tri, jnp.concatenate(_split3(logf), axis=1))
    fsum = (psum[:, 0:LANES] + (psum[:, LANES:2 * LANES] + psum[:, 2 * LANES:3 * LANES])
            + carry_ref[...])
    carry_ref[...] = fsum[tm - 1:tm, :]
    parts = jnp.concatenate(_split3(fsum * LOG2E), axis=1)

    scale = HEAD_DIM ** -0.5 * LOG2E
    yt = _dot_nt(wt_ref[...], hb)
    yn = _dot(hb, wn_ref[...])

    ones_rows = jnp.where(lax.broadcasted_iota(I32, (ONES_ROWS, tm), 0) == 0, 1.0, 0.0).astype(BF16)
    qtf_ref[0, 0] = (yt[0:FOX_W] * scale).astype(BF16)
    vt = yt[FOX_W:2 * FOX_W].astype(BF16)
    for hd in range(FOX_HEADS):
        vtf_ref[0, hd, 0, 0:HEAD_DIM] = vt[hd * HEAD_DIM:(hd + 1) * HEAD_DIM]
        vtf_ref[0, hd, 0, HEAD_DIM:HEAD_DIM + ONES_ROWS] = ones_rows
    qtd_ref[0, 0] = (yt[2 * FOX_W:2 * FOX_W + DIFF_W] * scale).astype(BF16)
    vtd = yt[2 * FOX_W + DIFF_W:2 * FOX_W + 2 * DIFF_W].astype(BF16)
    dv = 2 * HEAD_DIM
    for hd in range(DIFF_HEADS):
        vtd_ref[0, hd, 0, 0:dv] = vtd[hd * dv:(hd + 1) * dv]
        vtd_ref[0, hd, 0, dv:dv + ONES_ROWS] = ones_rows

    fqt_ref[0, 0] = (_dot_nt(pq_ref[...], parts) + qone_ref[...]).astype(BF16)
    kext = (_dot(parts, pk_ref[...]) + kone_ref[...]).astype(BF16)
    kf = yn[:, 0:FOX_W].astype(BF16)
    for p in range(FOX_HEADS // 2):
        kpf_ref[0, :, 2 * p * LANES:(2 * p + 1) * LANES] = kf[:, p * LANES:(p + 1) * LANES]
        kpf_ref[0, :, (2 * p + 1) * LANES:(2 * p + 2) * LANES] = kext[:, p * LANES:(p + 1) * LANES]
    kd_ref[0] = yn[:, FOX_W:FOX_W + DIFF_W].astype(BF16)


def _in_proj(x, sc1, sh1, ln_g, w_in, b_fgate, t):
    bsz, s, d = x.shape
    nb = s // t
    cuts = np.cumsum([FOX_W, FOX_W, FOX_W, FOX_HEADS, DIFF_W, DIFF_W, DIFF_W])
    w_fq, w_fk, w_fv, w_fl, w_dq, w_dk, w_dv = [
        w_in[:, a:b] for a, b in zip([0, *cuts[:-1]], cuts)]
    wt = jnp.concatenate([w_fq, w_fv, w_dq, w_dv], axis=1).T.astype(BF16)
    wn = jnp.concatenate([w_fk, w_dk], axis=1).astype(BF16)
    wfl = jnp.zeros((d, LANES), F32).at[:, :FOX_HEADS].set(w_fl)
    bfl = jnp.zeros((1, LANES), F32).at[0, :FOX_HEADS].set(b_fgate)
    pk, pq, k_ones, q_ones = _fparts_perm()
    npair = FOX_HEADS // 2

    full = lambda shape: pl.BlockSpec(shape, lambda b, i: (0,) * len(shape))
    outs = pl.pallas_call(
        functools.partial(_in_proj_kernel, tm=t),
        grid=(bsz, nb),
        in_specs=[pl.BlockSpec((1, t, d), lambda b, i: (b, i, 0)),
                  pl.BlockSpec((1, 1, d), lambda b, i: (b, 0, 0)),
                  pl.BlockSpec((1, 1, d), lambda b, i: (b, 0, 0)),
                  full((1, d)), full(wt.shape), full(wn.shape), full(wfl.shape), full(bfl.shape),
                  full(pk.shape), full(pq.shape), full(k_ones.shape), full(q_ones.shape)],
        out_specs=[pl.BlockSpec((1, 1, FOX_W, t), lambda b, i: (b, i, 0, 0)),
                   pl.BlockSpec((1, 1, npair * 32, t), lambda b, i: (b, i, 0, 0)),
                   pl.BlockSpec((1, t, npair * 2 * LANES), lambda b, i: (b, i, 0)),
                   pl.BlockSpec((1, FOX_HEADS, 1, HEAD_DIM + ONES_ROWS, t),
                                lambda b, i: (b, 0, i, 0, 0)),
                   pl.BlockSpec((1, 1, DIFF_W, t), lambda b, i: (b, i, 0, 0)),
                   pl.BlockSpec((1, t, DIFF_W), lambda b, i: (b, i, 0)),
                   pl.BlockSpec((1, DIFF_HEADS, 1, 2 * HEAD_DIM + ONES_ROWS, t),
                                lambda b, i: (b, 0, i, 0, 0))],
        out_shape=[jax.ShapeDtypeStruct((bsz, nb, FOX_W, t), BF16),
                   jax.ShapeDtypeStruct((bsz, nb, npair * 32, t), BF16),
                   jax.ShapeDtypeStruct((bsz, s, npair * 2 * LANES), BF16),
                   jax.ShapeDtypeStruct((bsz, FOX_HEADS, nb, HEAD_DIM + ONES_ROWS, t), BF16),
                   jax.ShapeDtypeStruct((bsz, nb, DIFF_W, t), BF16),
                   jax.ShapeDtypeStruct((bsz, s, DIFF_W), BF16),
                   jax.ShapeDtypeStruct((bsz, DIFF_HEADS, nb, 2 * HEAD_DIM + ONES_ROWS, t), BF16)],
        scratch_shapes=[pltpu.VMEM((1, LANES), F32)],
        compiler_params=_params("arbitrary", "arbitrary"),
        name="in_proj",
    )(x, sc1.reshape(bsz, 1, d), sh1.reshape(bsz, 1, d), ln_g.reshape(1, d), wt, wn, wfl, bfl,
      jnp.asarray(pk, BF16), jnp.asarray(pq, BF16), jnp.asarray(k_ones), jnp.asarray(q_ones))
    return outs


def _softmax_block(s, vt, m_ref, acc_ref, slot):
    m_old = m_ref[slot]
    m_new = jnp.maximum(m_old, jnp.max(s, axis=0, keepdims=True))
    alpha = jnp.exp2(m_old - m_new)
    p = jnp.exp2(s - m_new)
    acc_ref[slot] = alpha * acc_ref[slot] + _dot(vt, p.astype(BF16))
    m_ref[slot] = m_new


def _init_softmax_state(m_ref, acc_ref):
    m_ref[...] = jnp.full(m_ref.shape, NEG, F32)
    acc_ref[...] = jnp.zeros(acc_ref.shape, F32)


SCAN_BLOCKS_PER_TRIP = 4


def _attend_all_query_blocks(nb, n_tail, prepare_q, qk_into, process, begin, finish, bufs):
    ring, first_buf = bufs[:2], bufs[2]
    prepare_q(0, 0)
    qk_into(first_buf, 0, 0)

    def query_block(i, carry):
        slot = i & 1

        def start_next_query_block():
            @pl.when(i + 1 < nb)
            def _():
                prepare_q(i + 1, 1 - slot)
                qk_into(first_buf, 1 - slot, 0)

        def run(items, ends):
            for n, (buf, j, kind) in enumerate(items):
                last = n + 1 == len(items)
                if not last:
                    qk_into(items[n + 1][0], slot, items[n + 1][1])
                elif ends and buf is not first_buf:
                    start_next_query_block()
                process(buf, i, j, kind)
                if last and ends and buf is first_buf:
                    start_next_query_block()

        begin()
        for small in range(n_tail):
            items = [(first_buf, 0, small)] + [(ring[n % 2], n + 1, small - 1 - n) for n in range(small)]
            pl.when(i == small)(functools.partial(run, items, True))

        @pl.when(i >= n_tail)
        def _():
            n_plain = i - n_tail
            n_trips = n_plain // SCAN_BLOCKS_PER_TRIP
            rem = n_plain % SCAN_BLOCKS_PER_TRIP
            qk_into(ring[0], slot, 1)
            process(first_buf, i, 0, None)

            def trip(n, carry):
                j = 1 + SCAN_BLOCKS_PER_TRIP * n
                for m in range(SCAN_BLOCKS_PER_TRIP):
                    qk_into(ring[(m + 1) % 2], slot, j + m + 1)
                    process(ring[m % 2], i, j + m, None)
                return carry

            lax.fori_loop(0, n_trips, trip, 0)
            first = 1 + SCAN_BLOCKS_PER_TRIP * n_trips
            for r in range(SCAN_BLOCKS_PER_TRIP):
                kinds = [None] * r + list(range(n_tail - 1, -1, -1))
                items = [(ring[n % 2], first + n, kind) for n, kind in enumerate(kinds)]
                pl.when(rem == r)(functools.partial(run, items, True))

        finish(i)
        return carry

    lax.fori_loop(0, nb, query_block, 0)


def _fox_kernel(q_ref, fq_ref, k_ref, v_ref, o_ref, qbd_ref, sa_ref, sb_ref, sc_ref, m_ref, acc_ref,
                *, t, nb):
    d = HEAD_DIM
    qbd_ref[...] = jnp.zeros(qbd_ref.shape, BF16)

    def prepare_q(i, slot):
        qbd_ref[slot, 0:d, 0:t] = q_ref[0, i, 0:d, :]
        qbd_ref[slot, d:2 * d, t:2 * t] = q_ref[0, i, d:2 * d, :]
        qbd_ref[slot, 2 * d:2 * d + 16, 0:t] = fq_ref[0, i, 0:16, :]
        qbd_ref[slot, 2 * d + 16:2 * d + 32, t:2 * t] = fq_ref[0, i, 16:32, :]

    def qk_into(s_ref, slot, j):
        kblk = k_ref[0, pl.ds(pl.multiple_of(j * t, t), t), :]
        s_ref[...] = _dot(kblk, qbd_ref[slot])

    def process(s_ref, i, j, kind):
        for hh in range(2):
            sh = s_ref[:, hh * t:(hh + 1) * t]
            if kind == 0:
                key = lax.broadcasted_iota(I32, (t, t), 0)
                qry = lax.broadcasted_iota(I32, (t, t), 1)
                sh = jnp.where(key <= qry, sh, NEG)
            _softmax_block(sh, v_ref[0, hh, j], m_ref, acc_ref, hh)

    def finish(i):
        outs = [acc_ref[hh, 0:d] / acc_ref[hh, d:d + 1] for hh in range(2)]
        o_ref[0, pl.ds(pl.multiple_of(i * t, t), t), :] = (
            jnp.concatenate(outs, axis=0).T.astype(BF16))

    _attend_all_query_blocks(nb, 1, prepare_q, qk_into, process,
                             functools.partial(_init_softmax_state, m_ref, acc_ref), finish,
                             (sa_ref, sb_ref, sc_ref))


def _fox_attention(qt, fqt, kp, vt, t):
    bsz, nb = qt.shape[:2]
    s = nb * t
    npair = FOX_HEADS // 2
    return pl.pallas_call(
        functools.partial(_fox_kernel, t=t, nb=nb),
        grid=(bsz, npair),
        in_specs=[pl.BlockSpec((1, nb, 2 * HEAD_DIM, t), lambda b, p: (b, 0, p, 0)),
                  pl.BlockSpec((1, nb, 32, t), lambda b, p: (b, 0, p, 0)),
                  pl.BlockSpec((1, s, 2 * LANES), lambda b, p: (b, 0, p)),
                  pl.BlockSpec((1, 2, nb, HEAD_DIM + ONES_ROWS, t), lambda b, p: (b, p, 0, 0, 0))],
        out_specs=pl.BlockSpec((1, s, 2 * HEAD_DIM), lambda b, p: (b, 0, p)),
        out_shape=jax.ShapeDtypeStruct((bsz, s, FOX_W), BF16),
        scratch_shapes=[pltpu.VMEM((2, 2 * LANES, 2 * t), BF16)]
        + [pltpu.VMEM((t, 2 * t), F32)] * 3
        + [pltpu.VMEM((2, 1, t), F32), pltpu.VMEM((2, HEAD_DIM + ONES_ROWS, t), F32)],
        compiler_params=_params("arbitrary", "arbitrary"),
        name="fox_attn",
    )(qt, fqt, kp, vt)


def _diff_kernel(lam_ref, q_ref, k_ref, v_ref, bias_ref, g_ref, o_ref,
                 qbd_ref, sa_ref, sb_ref, sc_ref, m_ref, acc_ref, *, t, nb):
    d = HEAD_DIM
    dv = 2 * HEAD_DIM
    qbd_ref[...] = jnp.zeros(qbd_ref.shape, BF16)

    def prepare_q(i, slot):
        qbd_ref[slot, 0:d, 0:t] = q_ref[0, i, 0:d, :]
        qbd_ref[slot, d:2 * d, t:2 * t] = q_ref[0, i, d:2 * d, :]

    def qk_into(s_ref, slot, j):
        kblk = k_ref[0, pl.ds(pl.multiple_of(j * t, t), t), :]
        s_ref[...] = _dot(kblk, qbd_ref[slot])

    def process(s_ref, i, j, kind):
        vt = v_ref[0, 0, j]
        for hh in range(2):
            sh = s_ref[:, hh * t:(hh + 1) * t]
            if kind is not None:
                sh = sh + bias_ref[0, kind]
            _softmax_block(sh, vt, m_ref, acc_ref, hh)

    def finish(i):
        lam = (jnp.exp(jnp.sum(lam_ref[0:1, :] * lam_ref[1:2, :], axis=1, keepdims=True))
               - jnp.exp(jnp.sum(lam_ref[2:3, :] * lam_ref[3:4, :], axis=1, keepdims=True))
               + LAMBDA_INIT)
        out = (acc_ref[0, 0:dv] / acc_ref[0, dv:dv + 1]
               - lam * (acc_ref[1, 0:dv] / acc_ref[1, dv:dv + 1]))
        ms = jnp.mean(out * out, axis=0, keepdims=True)
        out = out * lax.rsqrt(ms + SUBLN_EPS) * g_ref[...] * (1.0 - LAMBDA_INIT)
        o_ref[0, pl.ds(pl.multiple_of(i * t, t), t), :] = out.T.astype(BF16)

    _attend_all_query_blocks(nb, 2, prepare_q, qk_into, process,
                             functools.partial(_init_softmax_state, m_ref, acc_ref), finish,
                             (sa_ref, sb_ref, sc_ref))


def _diff_attention(lam_vecs, qt, k, vt, bias_tiles, norm_g, t):
    bsz, nb = qt.shape[:2]
    s = nb * t
    dv = 2 * HEAD_DIM
    return pl.pallas_call(
        functools.partial(_diff_kernel, t=t, nb=nb),
        grid=(bsz, DIFF_HEADS),
        in_specs=[pl.BlockSpec((4, HEAD_DIM), lambda b, h: (0, 0)),
                  pl.BlockSpec((1, nb, dv, t), lambda b, h: (b, 0, h, 0)),
                  pl.BlockSpec((1, s, dv), lambda b, h: (b, 0, h)),
                  pl.BlockSpec((1, 1, nb, dv + ONES_ROWS, t), lambda b, h: (b, h, 0, 0, 0)),
                  pl.BlockSpec((1, 2, t, t), lambda b, h: (h, 0, 0, 0)),
                  pl.BlockSpec((dv, 1), lambda b, h: (0, 0))],
        out_specs=pl.BlockSpec((1, s, dv), lambda b, h: (b, 0, h)),
        out_shape=jax.ShapeDtypeStruct((bsz, s, DIFF_W), BF16),
        scratch_shapes=[pltpu.VMEM((2, dv, 2 * t), BF16)]
        + [pltpu.VMEM((t, 2 * t), F32)] * 3
        + [pltpu.VMEM((2, 1, t), F32), pltpu.VMEM((2, dv + ONES_ROWS, t), F32)],
        compiler_params=_params("arbitrary", "arbitrary"),
        name="diff_attn",
    )(lam_vecs, qt, k, vt, bias_tiles, norm_g.reshape(dv, 1))


def _post_attn_kernel(x_ref, yf_ref, yd_ref, sc1_ref, sh1_ref, g1_ref, sc2_ref, sh2_ref,
                      ln1_ref, ln2_ref, wgl_ref, bm_ref, wbf_ref, wbd_ref, wo_ref, wr_ref, br_ref,
                      x1_ref, h2_ref, idx_ref, rank_ref, gate_ref, cnt_ref, carry_ref, *, tm):
    i = pl.program_id(0)

    @pl.when(i == 0)
    def _():
        carry_ref[...] = jnp.zeros_like(carry_ref)

    d = x_ref.shape[1]
    x = x_ref[...]
    h1 = _modulated_rmsnorm(x, ln1_ref[...], sc1_ref[0], sh1_ref[0]).astype(BF16)
    gl = _dot(h1, wgl_ref[...]) + bm_ref[...]
    gates = 1.0 / (1.0 + jnp.exp(-gl))
    merged = (gates[:, :d] * _dot(yf_ref[...], wbf_ref[...])
              + gates[:, d:] * _dot(yd_ref[...], wbd_ref[...]))
    x1 = x + g1_ref[0] * _dot(merged.astype(BF16), wo_ref[...])
    x1_ref[...] = x1
    h2 = _modulated_rmsnorm(x1, ln2_ref[...], sc2_ref[0], sh2_ref[0])
    _store_rows_tiled(h2_ref, h2)

    logits = _dot_split(wr_ref[...], h2, _dot_nt) + br_ref[...]
    row = lax.broadcasted_iota(I32, (N_EXPERTS, tm), 0)
    cur = logits
    vals, idxs = [], []
    for _ in range(TOP_K):
        mx = jnp.max(cur, axis=0, keepdims=True)
        ik = jnp.min(jnp.where(cur == mx, row, N_EXPERTS), axis=0, keepdims=True)
        vals.append(mx)
        idxs.append(ik)
        cur = jnp.where(row == ik, -jnp.inf, cur)
    exps = [jnp.exp(v - vals[0]) for v in vals]
    denom = exps[0] + exps[1] + exps[2] + exps[3]
    gate_rows = [e / denom for e in exps]

    onehots = [row == ik for ik in idxs]
    cnt = jnp.zeros((N_EXPERTS, tm), F32)
    for oh in onehots:
        cnt = cnt + jnp.where(oh, 1.0, 0.0)
    r = lax.broadcasted_iota(I32, (tm, tm), 0)
    c = lax.broadcasted_iota(I32, (tm, tm), 1)
    before = jnp.where(r < c, 1.0, 0.0).astype(BF16)
    prior = carry_ref[:, 0:1] + _dot(cnt.astype(BF16), before)
    for k in range(TOP_K):
        rk = jnp.sum(jnp.where(onehots[k], prior, 0.0), axis=0, keepdims=True)
        rank_ref[k:k + 1, :] = rk.astype(I32)
        idx_ref[k:k + 1, :] = idxs[k]
    carry_ref[...] = carry_ref[...] + jnp.sum(cnt, axis=1, keepdims=True)
    cnt_ref[...] = carry_ref[...]

    lrow = lax.broadcasted_iota(I32, (LANES, tm), 0)
    g_t = jnp.zeros((LANES, tm), F32)
    for k in range(TOP_K):
        g_t = jnp.where(lrow == k, gate_rows[k], g_t)
    gate_ref[...] = g_t.T


def _post_attn(x2, yf, yd, mods, ln1_g, ln2_g, b_merge, w_gl, w_brf, w_brd, w_o, w_r, b_r, s, tm):
    n, d = x2.shape
    nbb = s // tm
    bsz = n // s
    sc1, sh1, g1, sc2, sh2 = [m.reshape(bsz, 1, d) for m in mods]
    row_spec = lambda w: pl.BlockSpec((tm, w), lambda i: (i, 0))
    mod_spec = pl.BlockSpec((1, 1, d), lambda i: (i // nbb, 0, 0))
    full = lambda shape: pl.BlockSpec(shape, lambda i: (0,) * len(shape))
    tok_spec = pl.BlockSpec((TOP_K, tm), lambda i: (0, i))
    return pl.pallas_call(
        functools.partial(_post_attn_kernel, tm=tm),
        grid=(n // tm,),
        in_specs=[row_spec(d), row_spec(FOX_W), row_spec(DIFF_W),
                  mod_spec, mod_spec, mod_spec, mod_spec, mod_spec,
                  full((1, d)), full((1, d)), full(w_gl.shape), full((1, 2 * d)),
                  full(w_brf.shape), full(w_brd.shape), full(w_o.shape),
                  full((N_EXPERTS, d)), full((N_EXPERTS, 1))],
        out_specs=[row_spec(d), pl.BlockSpec(_tiled_shape(tm, d), lambda i: (i, 0)),
                   tok_spec, tok_spec, row_spec(LANES), full((N_EXPERTS, LANES))],
        out_shape=[jax.ShapeDtypeStruct((n, d), F32), jax.ShapeDtypeStruct(_tiled_shape(n, d), F32),
                   jax.ShapeDtypeStruct((TOP_K, n), I32), jax.ShapeDtypeStruct((TOP_K, n), I32),
                   jax.ShapeDtypeStruct((n, LANES), F32),
                   jax.ShapeDtypeStruct((N_EXPERTS, LANES), F32)],
        scratch_shapes=[pltpu.VMEM((N_EXPERTS, LANES), F32)],
        compiler_params=_params("arbitrary"),
        name="post_attn",
    )(x2, yf, yd, sc1, sh1, g1, sc2, sh2, ln1_g.reshape(1, d), ln2_g.reshape(1, d),
      w_gl.astype(BF16), b_merge.reshape(1, 2 * d), w_brf.astype(BF16), w_brd.astype(BF16),
      w_o.astype(BF16), w_r.T, b_r.reshape(N_EXPERTS, 1))


def _route_kernel(cnt_ref, idx_ref, rank_ref, dest_ref, be_ref, pend_ref, *, blk, nblk_lanes):
    shift = blk.bit_length() - 1
    idx = idx_ref[...]
    start_of = jnp.zeros(idx.shape, I32)
    blk_start = lax.broadcasted_iota(I32, (1, nblk_lanes), 1) * blk
    lane = lax.broadcasted_iota(I32, (1, LANES), 1)
    blk_e = jnp.zeros((1, nblk_lanes), I32)
    pends = jnp.zeros((1, LANES), I32)
    pend = jnp.int32(0)
    for e in range(N_EXPERTS):
        padded = ((cnt_ref[e] + (blk - 1)) >> shift) << shift
        start_of = jnp.where(idx == e, pend, start_of)
        pend = pend + padded
        blk_e = blk_e + jnp.where(blk_start >= pend, 1, 0)
        pends = jnp.where(lane == e, pend, pends)
    dest_ref[...] = start_of + rank_ref[...]
    be_ref[...] = jnp.minimum(blk_e, N_EXPERTS - 1)
    pend_ref[...] = pends


def _route(counts, idx, rank, blk):
    n = idx.shape[1]
    nblk = (n * TOP_K + blk - 1) // blk + N_EXPERTS
    nblk_lanes = -(-nblk // LANES) * LANES
    dest, blk_e, pends = pl.pallas_call(
        functools.partial(_route_kernel, blk=blk, nblk_lanes=nblk_lanes),
        in_specs=[pl.BlockSpec(memory_space=pltpu.SMEM),
                  pl.BlockSpec(memory_space=pltpu.VMEM),
                  pl.BlockSpec(memory_space=pltpu.VMEM)],
        out_specs=[pl.BlockSpec(memory_space=pltpu.VMEM)] * 3,
        out_shape=[jax.ShapeDtypeStruct((TOP_K, n), I32),
                   jax.ShapeDtypeStruct((1, nblk_lanes), I32),
                   jax.ShapeDtypeStruct((1, LANES), I32)],
        compiler_params=pltpu.CompilerParams(vmem_limit_bytes=VMEM_LIMIT),
        name="route",
    )(counts, idx, rank)
    return dest, blk_e[0, :nblk], pends[0, :N_EXPERTS], nblk


def _dispatch_kernel(pend_ref, dest_ref, h_ref, xs_ref, zero_ref, sem, *, tm, blk, nblk):
    shift = blk.bit_length() - 1

    @pl.when(pl.program_id(0) == 0)
    def _():
        zero_ref[...] = jnp.zeros(zero_ref.shape, F32)

        def zero_block(b):
            lines = blk * ROW_TILE
            return pltpu.make_async_copy(
                zero_ref, xs_ref.at[pl.ds(pl.multiple_of(b * lines, lines), lines)], sem)

        def tails(fn):
            for e in range(N_EXPERTS):
                prev = pend_ref[e - 1] if e else 0

                @pl.when(pend_ref[e] > prev)
                def _():
                    fn(zero_block((pend_ref[e] >> shift) - 1))

        def rest(fn):
            def body(b, carry):
                fn(zero_block(b))
                return carry
            lax.fori_loop(pend_ref[N_EXPERTS - 1] >> shift, nblk, body, 0)

        tails(lambda cp: cp.start())
        rest(lambda cp: cp.start())
        tails(lambda cp: cp.wait())
        rest(lambda cp: cp.wait())

    def row_copy(t, k):
        return pltpu.make_async_copy(_row_tile(h_ref, t), _row_tile(xs_ref, dest_ref[k, t]), sem)

    def start(t, carry):
        for k in range(TOP_K):
            row_copy(t, k).start(priority=k % 2)
        return carry

    def wait(t, carry):
        for k in range(TOP_K):
            row_copy(t, k).wait()
        return carry

    lax.fori_loop(0, tm, start, 0, unroll=ROW_DMA_UNROLL)
    lax.fori_loop(0, tm, wait, 0, unroll=ROW_DMA_UNROLL)


def _dispatch(pends, dest, h2, tm, blk, nblk):
    n = h2.shape[0] // ROW_TILE
    d = ROW_TILE * LANES
    return pl.pallas_call(
        functools.partial(_dispatch_kernel, tm=tm, blk=blk, nblk=nblk),
        grid=(n // tm,),
        in_specs=[pl.BlockSpec(memory_space=pltpu.SMEM),
                  pl.BlockSpec((TOP_K, tm), lambda i: (0, i), memory_space=pltpu.SMEM),
                  pl.BlockSpec(_tiled_shape(tm, d), lambda i: (i, 0))],
        out_specs=pl.BlockSpec(memory_space=pl.ANY),
        out_shape=jax.ShapeDtypeStruct(_tiled_shape(nblk * blk, d), F32),
        scratch_shapes=[pltpu.VMEM(_tiled_shape(blk, d), F32), pltpu.SemaphoreType.DMA(())],
        compiler_params=_params("arbitrary"),
        name="dispatch",
    )(pends, dest, h2)


ROW_DMA_UNROLL = 4
UP_CHUNK = 2048
PICK_GROUP = 512


def _expert_kernel(be_ref, pend_ref, x_ref, wu_ref, bu_ref, wd_ref, bd_ref, o_ref,
                   wu_sc, wd_sc, *, blk):
    b = pl.program_id(0)
    live = b * blk < pend_ref[N_EXPERTS - 1]
    new_expert = (b == 0) | (be_ref[b] != be_ref[jnp.maximum(b - 1, 0)])

    @pl.when(live & new_expert)
    def _():
        wu_sc[...] = wu_ref[0].astype(BF16)
        wd_sc[...] = wd_ref[0].astype(BF16)

    @pl.when(live)
    def _():
        x = _load_rows_tiled(x_ref).astype(BF16)
        r = lax.broadcasted_iota(I32, (PICK_GROUP, PICK_GROUP // 2), 0)
        c = lax.broadcasted_iota(I32, (PICK_GROUP, PICK_GROUP // 2), 1)
        even = jnp.where(r == 2 * c, 1.0, 0.0).astype(BF16)
        acts = []
        for n in range(wu_sc.shape[1] // UP_CHUNK):
            cols = slice(n * UP_CHUNK, (n + 1) * UP_CHUNK)
            u = _dot(x, wu_sc[:, cols]) + bu_ref[0, :, cols]
            u_glu = jnp.minimum(u, SWIGLU_LIMIT)
            u_lin = jnp.clip(u, -SWIGLU_LIMIT, SWIGLU_LIMIT) + 1.0
            z = (u_glu * (1.0 / (1.0 + jnp.exp(-SWIGLU_ALPHA * u_glu)))
                 * pltpu.roll(u_lin, UP_CHUNK - 1, 1)).astype(BF16)
            for g in range(UP_CHUNK // PICK_GROUP):
                acts.append(_dot(z[:, g * PICK_GROUP:(g + 1) * PICK_GROUP], even).astype(BF16))
        _store_rows_tiled(o_ref, _dot(jnp.concatenate(acts, axis=1), wd_sc[...]) + bd_ref[0])

    @pl.when(jnp.logical_not(live))
    def _():
        o_ref[...] = jnp.zeros(o_ref.shape, F32)


def _experts(blk_e, pends, xs, w_up, b_up, w_down, b_down, blk):
    p = xs.shape[0] // ROW_TILE
    ne, d, f2 = w_up.shape
    f = w_down.shape[1]
    shift = blk.bit_length() - 1

    def live(b, be, pe):
        return jnp.minimum(b, (pe[N_EXPERTS - 1] >> shift) - 1)

    x_spec = pl.BlockSpec(_tiled_shape(blk, d), lambda b, be, pe: (live(b, be, pe), 0))
    w_spec = lambda r, c: pl.BlockSpec((1, r, c), lambda b, be, pe: (be[live(b, be, pe)], 0, 0))
    return pl.pallas_call(
        functools.partial(_expert_kernel, blk=blk),
        grid_spec=pltpu.PrefetchScalarGridSpec(
            num_scalar_prefetch=2,
            grid=(p // blk,),
            in_specs=[x_spec, w_spec(d, f2), w_spec(1, f2), w_spec(f, d), w_spec(1, d)],
            out_specs=pl.BlockSpec(_tiled_shape(blk, d), lambda b, be, pe: (b, 0)),
            scratch_shapes=[pltpu.VMEM((d, f2), BF16), pltpu.VMEM((f, d), BF16)]),
        out_shape=jax.ShapeDtypeStruct(_tiled_shape(p, d), F32),
        compiler_params=_params("arbitrary"),
        name="experts",
    )(blk_e, pends, xs, w_up, b_up.reshape(ne, 1, f2), w_down, b_down.reshape(ne, 1, d))


def _combine_kernel(dest_ref, x1_ref, gate_ref, g2_ref, fg_ref, ys_ref, o_ref, ybuf_ref, sem, *, tm):
    def row_copy(t, k):
        return pltpu.make_async_copy(_row_tile(ys_ref, dest_ref[k, t]), _row_tile(ybuf_ref.at[k], t), sem)

    def start(t, carry):
        for k in range(TOP_K):
            row_copy(t, k).start(priority=k % 2)
        return carry

    def wait(t, carry):
        for k in range(TOP_K):
            row_copy(t, k).wait()
        return carry

    lax.fori_loop(0, tm, start, 0, unroll=ROW_DMA_UNROLL)
    lax.fori_loop(0, tm, wait, 0, unroll=ROW_DMA_UNROLL)

    g = gate_ref[...]
    moe = g[:, 0:1] * _load_rows_tiled(ybuf_ref.at[0])
    for k in range(1, TOP_K):
        moe = moe + g[:, k:k + 1] * _load_rows_tiled(ybuf_ref.at[k])
    xo = x1_ref[...] + g2_ref[0] * moe
    o_ref[...] = xo * lax.rsqrt(jnp.mean(xo * xo, axis=-1, keepdims=True) + NORM_EPS) * fg_ref[...]


def _combine(dest, x1, gates, g2, final_g, ys, s, tm):
    n, d = x1.shape
    nbb = s // tm
    bsz = n // s
    return pl.pallas_call(
        functools.partial(_combine_kernel, tm=tm),
        grid=(n // tm,),
        in_specs=[pl.BlockSpec((TOP_K, tm), lambda i: (0, i), memory_space=pltpu.SMEM),
                  pl.BlockSpec((tm, d), lambda i: (i, 0)),
                  pl.BlockSpec((tm, LANES), lambda i: (i, 0)),
                  pl.BlockSpec((1, 1, d), lambda i: (i // nbb, 0, 0)),
                  pl.BlockSpec((1, d), lambda i: (0, 0)),
                  pl.BlockSpec(memory_space=pl.ANY)],
        out_specs=pl.BlockSpec((tm, d), lambda i: (i, 0)),
        out_shape=jax.ShapeDtypeStruct((n, d), F32),
        scratch_shapes=[pltpu.VMEM((TOP_K, *_tiled_shape(tm, d)), F32), pltpu.SemaphoreType.DMA(())],
        compiler_params=_params("arbitrary"),
        name="combine",
    )(dest, x1, gates, g2.reshape(bsz, 1, d), final_g.reshape(1, d), ys)


def kernel(x, c, w_ada, b_ada, ln1_g, w_in, b_fgate, b_merge, lam_q1, lam_k1, lam_q2, lam_k2,
           diff_norm_g, t5_bias, w_br_fox, w_br_diff, w_o, ln2_g, w_router, b_router,
           w_up, b_up, w_down, b_down, final_g):
    bsz, s, d = x.shape
    n = bsz * s
    t = min(512, s)
    tr = min(256, s)
    assert s % t == 0 and w_ada.shape[0] == 1

    mod = _adaln_mod(c, w_ada[0], b_ada[0])
    sh1, sc1, g1, sh2, sc2, g2 = jnp.split(mod, 6, axis=-1)

    qtf, fqt, kpf, vtf, qtd, kd, vtd = _in_proj(x, sc1, sh1, ln1_g[0], w_in[0], b_fgate[0], t)
    y_fox = _fox_attention(qtf, fqt, kpf, vtf, t)
    lam_vecs = jnp.stack([lam_q1[0], lam_k1[0], lam_q2[0], lam_k2[0]])
    y_diff = _diff_attention(lam_vecs, qtd, kd, vtd, _bias_tiles(t5_bias, t), diff_norm_g[0], t)

    w_gl = w_in[0][:, w_in.shape[2] - 2 * d:]
    x1, h2, idx, rank, gates, counts = _post_attn(
        x.reshape(n, d), y_fox.reshape(n, FOX_W), y_diff.reshape(n, DIFF_W),
        (sc1, sh1, g1, sc2, sh2), ln1_g[0], ln2_g[0], b_merge[0], w_gl,
        w_br_fox[0], w_br_diff[0], w_o[0], w_router[0], b_router[0], s, t)

    dest, blk_e, pends, nblk = _route(counts[:, 0].astype(I32), idx, rank, EXPERT_BLOCK)
    xs = _dispatch(pends, dest, h2, tr, EXPERT_BLOCK, nblk)
    ys = _experts(blk_e, pends, xs, w_up[0], b_up[0], w_down[0], b_down[0], EXPERT_BLOCK)
    out = _combine(dest, x1, gates, g2, final_g, ys, s, tr)
    return out.reshape(bsz, s, d)
```

```python
import functools
import math

import numpy as np
import jax
import jax.numpy as jnp
from jax import lax
from jax.experimental import pallas as pl
from jax.experimental.pallas import tpu as pltpu

F32 = jnp.float32
BF16 = jnp.bfloat16
I32 = jnp.int32

HEAD_DIM = 64
FOX_HEADS = 8
DIFF_HEADS = 4
FOX_W = FOX_HEADS * HEAD_DIM
DIFF_W = DIFF_HEADS * 2 * HEAD_DIM
T5_BUCKETS = 32
T5_MAX_DIST = 128
N_EXPERTS = 32
TOP_K = 4
EXPERT_BLOCK = 512
SWIGLU_ALPHA = 1.702
SWIGLU_LIMIT = 7.0
NORM_EPS = 1e-6
SUBLN_EPS = 1e-5
LAMBDA_INIT = 0.8 - 0.6 * math.exp(-0.3 * 0)

LANES = 128
NEG = -1e30
LOG2E = math.log2(math.e)
ONES_ROWS = 16
VMEM_LIMIT = 56 * 1024 * 1024

NT_DIMS = (((1,), (1,)), ((), ()))


def _dot(a, b):
    return jnp.dot(a, b, preferred_element_type=F32)


def _dot_nt(a, b):
    return lax.dot_general(a, b, NT_DIMS, preferred_element_type=F32)


def _split2(v):
    hi = v.astype(BF16)
    return hi, (v - hi.astype(F32)).astype(BF16)


def _dot_split(a, b, dot=_dot):
    a_hi, a_lo = _split2(a)
    b_hi, b_lo = _split2(b)
    return dot(a_hi, b_hi) + (dot(a_hi, b_lo) + dot(a_lo, b_hi))


def _params(*sem):
    return pltpu.CompilerParams(dimension_semantics=sem, vmem_limit_bytes=VMEM_LIMIT)


def _modulated_rmsnorm(x, g, scale, shift):
    y = x * lax.rsqrt(jnp.mean(x * x, axis=-1, keepdims=True) + NORM_EPS)
    return (y * g) * (1.0 + scale) + shift


ROW_TILE = 8


def _tiled_shape(n, d):
    assert d == ROW_TILE * LANES
    return (n * ROW_TILE, LANES)


def _row_tile(ref, r):
    return ref.at[pl.ds(pl.multiple_of(r * ROW_TILE, ROW_TILE), ROW_TILE)]


def _store_rows_tiled(ref, rows):
    n = rows.shape[0]
    for a in range(ROW_TILE):
        ref[pl.ds(a, n, stride=ROW_TILE), :] = rows[:, a * LANES:(a + 1) * LANES]


def _load_rows_tiled(ref):
    n = ref.shape[0] // ROW_TILE
    return jnp.concatenate([ref[pl.ds(a, n, stride=ROW_TILE), :] for a in range(ROW_TILE)], axis=1)


def _split3(v):
    hi = v.astype(BF16)
    r1 = v - hi.astype(F32)
    mid = r1.astype(BF16)
    lo = (r1 - mid.astype(F32)).astype(BF16)
    return hi, mid, lo


def _adaln_kernel(c_ref, w_ref, b_ref, o_ref):
    c = c_ref[...]
    s = c / (1.0 + jnp.exp(-c))
    o_ref[...] = _dot_split(s, w_ref[...]) + b_ref[...]


def _adaln_mod(c, w, b):
    bsz, d = c.shape
    n = w.shape[1]
    rows = 8
    tn = 1536
    c8 = jnp.zeros((rows, d), F32).at[:bsz].set(c)
    out = pl.pallas_call(
        _adaln_kernel,
        grid=(n // tn,),
        in_specs=[pl.BlockSpec((rows, d), lambda j: (0, 0)),
                  pl.BlockSpec((d, tn), lambda j: (0, j)),
                  pl.BlockSpec((1, tn), lambda j: (0, j))],
        out_specs=pl.BlockSpec((rows, tn), lambda j: (0, j)),
        out_shape=jax.ShapeDtypeStruct((rows, n), F32),
        compiler_params=_params("arbitrary"),
        name="adaln_mod",
    )(c8, w, b.reshape(1, n))
    return out[:bsz]


def _bias_tile_kernel(tb_ref, o_ref, *, t):
    h = pl.program_id(0)
    key = lax.broadcasted_iota(I32, (t, t), 0)
    qry = lax.broadcasted_iota(I32, (t, t), 1)
    far = tb_ref[h, T5_BUCKETS - 1]
    max_exact = T5_BUCKETS // 2
    for which in range(2):
        rel = qry - key + which * t
        n = jnp.maximum(rel, 0)
        nf = jnp.maximum(n, max_exact).astype(F32)
        large = max_exact + (jnp.log(nf / max_exact) / math.log(T5_MAX_DIST / max_exact)
                             * (T5_BUCKETS - max_exact)).astype(I32)
        large = jnp.minimum(large, T5_BUCKETS - 1)
        bucket = jnp.where(n < max_exact, n, large)
        bias = jnp.zeros((t, t), F32)
        for b in range(T5_BUCKETS):
            bias = jnp.where(bucket == b, tb_ref[h, b] - far, bias)
        o_ref[0, which] = jnp.where(rel >= 0, bias * LOG2E, NEG)


def _bias_tiles(t5_bias, t):
    tb = t5_bias.T.astype(F32)
    return pl.pallas_call(
        functools.partial(_bias_tile_kernel, t=t),
        grid=(DIFF_HEADS,),
        in_specs=[pl.BlockSpec(memory_space=pltpu.SMEM)],
        out_specs=pl.BlockSpec((1, 2, t, t), lambda h: (h, 0, 0, 0)),
        out_shape=jax.ShapeDtypeStruct((DIFF_HEADS, 2, t, t), F32),
        compiler_params=_params("arbitrary"),
        name="bias_tiles",
    )(tb)


def _fparts_perm():
    npair = FOX_HEADS // 2
    pk = np.zeros((3 * LANES, npair * LANES), np.float32)
    pq = np.zeros((npair * 32, 3 * LANES), np.float32)
    k_ones = np.zeros((1, npair * LANES), np.float32)
    q_ones = np.zeros((npair * 32, 1), np.float32)
    for p in range(npair):
        for hh in range(2):
            head = 2 * p + hh
            for part in range(3):
                pk[part * LANES + head, p * LANES + hh * 16 + 3 + part] = -1.0
                k_ones[0, p * LANES + hh * 16 + part] = 1.0
                pq[p * 32 + hh * 16 + part, part * LANES + head] = 1.0
                q_ones[p * 32 + hh * 16 + 3 + part, 0] = 1.0
    return pk, pq, k_ones, q_ones


def _in_proj_kernel(x_ref, sc_ref, sh_ref, g_ref, wt_ref, wn_ref, wfl_ref, bf_ref,
                    pk_ref, pq_ref, kone_ref, qone_ref,
                    qtf_ref, fqt_ref, kpf_ref, vtf_ref, qtd_ref, kd_ref, vtd_ref,
                    carry_ref, *, tm):
    i = pl.program_id(1)

    @pl.when(i == 0)
    def _():
        carry_ref[...] = jnp.zeros_like(carry_ref)

    h = _modulated_rmsnorm(x_ref[0], g_ref[...], sc_ref[0], sh_ref[0])
    hb = h.astype(BF16)

    h_lo = (h - hb.astype(F32)).astype(BF16)
    wfl_hi, wfl_lo = _split2(wfl_ref[...])
    fl = _dot(hb, wfl_hi) + (_dot(hb, wfl_lo) + _dot(h_lo, wfl_hi)) + bf_ref[...]
    logf = jnp.minimum(fl, 0.0) - jnp.log(1.0 + jnp.exp(-jnp.abs(fl)))
    r = lax.broadcasted_iota(I32, (tm, tm), 0)
    c = lax.broadcasted_iota(I32, (tm, tm), 1)
    tri = jnp.where(c <= r, 1.0, 0.0).astype(BF16)
    psum = _dot(tri, jnp.concatenate(_split3(logf), axis=1))
    fsum = (psum[:, 0:LANES] + (psum[:, LANES:2 * LANES] + psum[:, 2 * LANES:3 * LANES])
            + carry_ref[...])
    carry_ref[...] = fsum[tm - 1:tm, :]
    parts = jnp.concatenate(_split3(fsum * LOG2E), axis=1)

    scale = HEAD_DIM ** -0.5 * LOG2E
    yt = _dot_nt(wt_ref[...], hb)
    yn = _dot(hb, wn_ref[...])

    ones_rows = jnp.where(lax.broadcasted_iota(I32, (ONES_ROWS, tm), 0) == 0, 1.0, 0.0).astype(BF16)
    qtf_ref[0, 0] = (yt[0:FOX_W] * scale).astype(BF16)
    vt = yt[FOX_W:2 * FOX_W].astype(BF16)
    for hd in range(FOX_HEADS):
        vtf_ref[0, hd, 0, 0:HEAD_DIM] = vt[hd * HEAD_DIM:(hd + 1) * HEAD_DIM]
        vtf_ref[0, hd, 0, HEAD_DIM:HEAD_DIM + ONES_ROWS] = ones_rows
    qtd_ref[0, 0] = (yt[2 * FOX_W:2 * FOX_W + DIFF_W] * scale).astype(BF16)
    vtd = yt[2 * FOX_W + DIFF_W:2 * FOX_W + 2 * DIFF_W].astype(BF16)
    dv = 2 * HEAD_DIM
    for hd in range(DIFF_HEADS):
        vtd_ref[0, hd, 0, 0:dv] = vtd[hd * dv:(hd + 1) * dv]
        vtd_ref[0, hd, 0, dv:dv + ONES_ROWS] = ones_rows

    fqt_ref[0, 0] = (_dot_nt(pq_ref[...], parts) + qone_ref[...]).astype(BF16)
    kext = (_dot(parts, pk_ref[...]) + kone_ref[...]).astype(BF16)
    kf = yn[:, 0:FOX_W].astype(BF16)
    for p in range(FOX_HEADS // 2):
        kpf_ref[0, :, 2 * p * LANES:(2 * p + 1) * LANES] = kf[:, p * LANES:(p + 1) * LANES]
        kpf_ref[0, :, (2 * p + 1) * LANES:(2 * p + 2) * LANES] = kext[:, p * LANES:(p + 1) * LANES]
    kd_ref[0] = yn[:, FOX_W:FOX_W + DIFF_W].astype(BF16)


def _in_proj(x, sc1, sh1, ln_g, w_in, b_fgate, t):
    bsz, s, d = x.shape
    nb = s // t
    cuts = np.cumsum([FOX_W, FOX_W, FOX_W, FOX_HEADS, DIFF_W, DIFF_W, DIFF_W])
    w_fq, w_fk, w_fv, w_fl, w_dq, w_dk, w_dv = [
        w_in[:, a:b] for a, b in zip([0, *cuts[:-1]], cuts)]
    wt = jnp.concatenate([w_fq, w_fv, w_dq, w_dv], axis=1).T.astype(BF16)
    wn = jnp.concatenate([w_fk, w_dk], axis=1).astype(BF16)
    wfl = jnp.zeros((d, LANES), F32).at[:, :FOX_HEADS].set(w_fl)
    bfl = jnp.zeros((1, LANES), F32).at[0, :FOX_HEADS].set(b_fgate)
    pk, pq, k_ones, q_ones = _fparts_perm()
    npair = FOX_HEADS // 2

    full = lambda shape: pl.BlockSpec(shape, lambda b, i: (0,) * len(shape))
    outs = pl.pallas_call(
        functools.partial(_in_proj_kernel, tm=t),
        grid=(bsz, nb),
        in_specs=[pl.BlockSpec((1, t, d), lambda b, i: (b, i, 0)),
                  pl.BlockSpec((1, 1, d), lambda b, i: (b, 0, 0)),
                  pl.BlockSpec((1, 1, d), lambda b, i: (b, 0, 0)),
                  full((1, d)), full(wt.shape), full(wn.shape), full(wfl.shape), full(bfl.shape),
                  full(pk.shape), full(pq.shape), full(k_ones.shape), full(q_ones.shape)],
        out_specs=[pl.BlockSpec((1, 1, FOX_W, t), lambda b, i: (b, i, 0, 0)),
                   pl.BlockSpec((1, 1, npair * 32, t), lambda b, i: (b, i, 0, 0)),
                   pl.BlockSpec((1, t, npair * 2 * LANES), lambda b, i: (b, i, 0)),
                   pl.BlockSpec((1, FOX_HEADS, 1, HEAD_DIM + ONES_ROWS, t),
                                lambda b, i: (b, 0, i, 0, 0)),
                   pl.BlockSpec((1, 1, DIFF_W, t), lambda b, i: (b, i, 0, 0)),
                   pl.BlockSpec((1, t, DIFF_W), lambda b, i: (b, i, 0)),
                   pl.BlockSpec((1, DIFF_HEADS, 1, 2 * HEAD_DIM + ONES_ROWS, t),
                                lambda b, i: (b, 0, i, 0, 0))],
        out_shape=[jax.ShapeDtypeStruct((bsz, nb, FOX_W, t), BF16),
                   jax.ShapeDtypeStruct((bsz, nb, npair * 32, t), BF16),
                   jax.ShapeDtypeStruct((bsz, s, npair * 2 * LANES), BF16),
                   jax.ShapeDtypeStruct((bsz, FOX_HEADS, nb, HEAD_DIM + ONES_ROWS, t), BF16),
                   jax.ShapeDtypeStruct((bsz, nb, DIFF_W, t), BF16),
                   jax.ShapeDtypeStruct((bsz, s, DIFF_W), BF16),
                   jax.ShapeDtypeStruct((bsz, DIFF_HEADS, nb, 2 * HEAD_DIM + ONES_ROWS, t), BF16)],
        scratch_shapes=[pltpu.VMEM((1, LANES), F32)],
        compiler_params=_params("arbitrary", "arbitrary"),
        name="in_proj",
    )(x, sc1.reshape(bsz, 1, d), sh1.reshape(bsz, 1, d), ln_g.reshape(1, d), wt, wn, wfl, bfl,
      jnp.asarray(pk, BF16), jnp.asarray(pq, BF16), jnp.asarray(k_ones), jnp.asarray(q_ones))
    return outs


SCORE_EXTRA_ROWS = 8


def _store_scores(s_ref, s, t):
    s_ref[0:t, :] = s
    s_ref[t:t + 1, :] = jnp.max(s, axis=0, keepdims=True)


def _softmax_block(s, s_max, vt, m_ref, acc_ref, slot):
    m_old = m_ref[slot]
    m_new = jnp.maximum(m_old, s_max)
    alpha = jnp.exp2(m_old - m_new)
    p = jnp.exp2(s - m_new)
    acc_ref[slot] = alpha * acc_ref[slot] + _dot(vt, p.astype(BF16))
    m_ref[slot] = m_new


def _init_softmax_state(m_ref, acc_ref):
    m_ref[...] = jnp.full(m_ref.shape, NEG, F32)
    acc_ref[...] = jnp.zeros(acc_ref.shape, F32)


SCAN_BLOCKS_PER_TRIP = 4


def _attend_all_query_blocks(nb, n_tail, prepare_q, qk_into, process, begin, finish, bufs):
    ring, first_buf = bufs[:2], bufs[2]
    prepare_q(0, 0)
    qk_into(first_buf, 0, 0)

    def query_block(i, carry):
        slot = i & 1

        def start_next_query_block():
            prepare_q(jnp.minimum(i + 1, nb - 1), 1 - slot)
            qk_into(first_buf, 1 - slot, 0)

        def run(items, ends):
            for n, (buf, j, kind) in enumerate(items):
                last = n + 1 == len(items)
                if not last:
                    qk_into(items[n + 1][0], slot, items[n + 1][1])
                elif ends and buf is not first_buf:
                    start_next_query_block()
                process(buf, i, j, kind)
                if last and ends and buf is first_buf:
                    start_next_query_block()

        begin()
        for small in range(n_tail):
            items = [(first_buf, 0, small)] + [(ring[n % 2], n + 1, small - 1 - n) for n in range(small)]
            pl.when(i == small)(functools.partial(run, items, True))

        @pl.when(i >= n_tail)
        def _():
            n_plain = i - n_tail
            n_trips = n_plain // SCAN_BLOCKS_PER_TRIP
            rem = n_plain % SCAN_BLOCKS_PER_TRIP
            qk_into(ring[0], slot, 1)
            process(first_buf, i, 0, None)

            def trip(n, carry):
                j = 1 + SCAN_BLOCKS_PER_TRIP * n
                for m in range(SCAN_BLOCKS_PER_TRIP):
                    qk_into(ring[(m + 1) % 2], slot, j + m + 1)
                    process(ring[m % 2], i, j + m, None)
                return carry

            lax.fori_loop(0, n_trips, trip, 0)
            first = 1 + SCAN_BLOCKS_PER_TRIP * n_trips
            for r in range(SCAN_BLOCKS_PER_TRIP):
                kinds = [None] * r + list(range(n_tail - 1, -1, -1))
                items = [(ring[n % 2], first + n, kind) for n, kind in enumerate(kinds)]
                pl.when(rem == r)(functools.partial(run, items, True))

        finish(i)
        return carry

    lax.fori_loop(0, nb, query_block, 0)


def _fox_kernel(q_ref, fq_ref, k_ref, v_ref, o_ref, qbd_ref, sa_ref, sb_ref, sc_ref, m_ref, acc_ref,
                *, t, nb):
    d = HEAD_DIM
    qbd_ref[...] = jnp.zeros(qbd_ref.shape, BF16)

    def prepare_q(i, slot):
        qbd_ref[slot, 0:d, 0:t] = q_ref[0, i, 0:d, :]
        qbd_ref[slot, d:2 * d, t:2 * t] = q_ref[0, i, d:2 * d, :]
        qbd_ref[slot, 2 * d:2 * d + 16, 0:t] = fq_ref[0, i, 0:16, :]
        qbd_ref[slot, 2 * d + 16:2 * d + 32, t:2 * t] = fq_ref[0, i, 16:32, :]

    def qk_into(s_ref, slot, j):
        kblk = k_ref[0, pl.ds(pl.multiple_of(j * t, t), t), :]
        _store_scores(s_ref, _dot(kblk, qbd_ref[slot]), t)

    def process(s_ref, i, j, kind):
        for hh in range(2):
            sh = s_ref[0:t, hh * t:(hh + 1) * t]
            if kind == 0:
                key = lax.broadcasted_iota(I32, (t, t), 0)
                qry = lax.broadcasted_iota(I32, (t, t), 1)
                sh = jnp.where(key <= qry, sh, NEG)
                s_max = jnp.max(sh, axis=0, keepdims=True)
            else:
                s_max = s_ref[t:t + 1, hh * t:(hh + 1) * t]
            _softmax_block(sh, s_max, v_ref[0, hh, j], m_ref, acc_ref, hh)

    def finish(i):
        outs = [acc_ref[hh, 0:d] / acc_ref[hh, d:d + 1] for hh in range(2)]
        o_ref[0, pl.ds(pl.multiple_of(i * t, t), t), :] = (
            jnp.concatenate(outs, axis=0).T.astype(BF16))

    _attend_all_query_blocks(nb, 1, prepare_q, qk_into, process,
                             functools.partial(_init_softmax_state, m_ref, acc_ref), finish,
                             (sa_ref, sb_ref, sc_ref))


def _fox_attention(qt, fqt, kp, vt, t):
    bsz, nb = qt.shape[:2]
    s = nb * t
    npair = FOX_HEADS // 2
    return pl.pallas_call(
        functools.partial(_fox_kernel, t=t, nb=nb),
        grid=(bsz, npair),
        in_specs=[pl.BlockSpec((1, nb, 2 * HEAD_DIM, t), lambda b, p: (b, 0, p, 0)),
                  pl.BlockSpec((1, nb, 32, t), lambda b, p: (b, 0, p, 0)),
                  pl.BlockSpec((1, s, 2 * LANES), lambda b, p: (b, 0, p)),
                  pl.BlockSpec((1, 2, nb, HEAD_DIM + ONES_ROWS, t), lambda b, p: (b, p, 0, 0, 0))],
        out_specs=pl.BlockSpec((1, s, 2 * HEAD_DIM), lambda b, p: (b, 0, p)),
        out_shape=jax.ShapeDtypeStruct((bsz, s, FOX_W), BF16),
        scratch_shapes=[pltpu.VMEM((2, 2 * LANES, 2 * t), BF16)]
        + [pltpu.VMEM((t + SCORE_EXTRA_ROWS, 2 * t), F32)] * 3
        + [pltpu.VMEM((2, 1, t), F32), pltpu.VMEM((2, HEAD_DIM + ONES_ROWS, t), F32)],
        compiler_params=_params("arbitrary", "arbitrary"),
        name="fox_attn",
    )(qt, fqt, kp, vt)


def _diff_kernel(lam_ref, q_ref, k_ref, v_ref, bias_ref, g_ref, o_ref,
                 qbd_ref, sa_ref, sb_ref, sc_ref, m_ref, acc_ref, *, t, nb):
    d = HEAD_DIM
    dv = 2 * HEAD_DIM
    qbd_ref[...] = jnp.zeros(qbd_ref.shape, BF16)

    def prepare_q(i, slot):
        qbd_ref[slot, 0:d, 0:t] = q_ref[0, i, 0:d, :]
        qbd_ref[slot, d:2 * d, t:2 * t] = q_ref[0, i, d:2 * d, :]

    def qk_into(s_ref, slot, j):
        kblk = k_ref[0, pl.ds(pl.multiple_of(j * t, t), t), :]
        _store_scores(s_ref, _dot(kblk, qbd_ref[slot]), t)

    def process(s_ref, i, j, kind):
        vt = v_ref[0, 0, j]
        for hh in range(2):
            sh = s_ref[0:t, hh * t:(hh + 1) * t]
            if kind is not None:
                sh = sh + bias_ref[0, kind]
                s_max = jnp.max(sh, axis=0, keepdims=True)
            else:
                s_max = s_ref[t:t + 1, hh * t:(hh + 1) * t]
            _softmax_block(sh, s_max, vt, m_ref, acc_ref, hh)

    def finish(i):
        lam = (jnp.exp(jnp.sum(lam_ref[0:1, :] * lam_ref[1:2, :], axis=1, keepdims=True))
               - jnp.exp(jnp.sum(lam_ref[2:3, :] * lam_ref[3:4, :], axis=1, keepdims=True))
               + LAMBDA_INIT)
        out = (acc_ref[0, 0:dv] / acc_ref[0, dv:dv + 1]
               - lam * (acc_ref[1, 0:dv] / acc_ref[1, dv:dv + 1]))
        ms = jnp.mean(out * out, axis=0, keepdims=True)
        out = out * lax.rsqrt(ms + SUBLN_EPS) * g_ref[...] * (1.0 - LAMBDA_INIT)
        o_ref[0, pl.ds(pl.multiple_of(i * t, t), t), :] = out.T.astype(BF16)

    _attend_all_query_blocks(nb, 2, prepare_q, qk_into, process,
                             functools.partial(_init_softmax_state, m_ref, acc_ref), finish,
                             (sa_ref, sb_ref, sc_ref))


def _diff_attention(lam_vecs, qt, k, vt, bias_tiles, norm_g, t):
    bsz, nb = qt.shape[:2]
    s = nb * t
    dv = 2 * HEAD_DIM
    return pl.pallas_call(
        functools.partial(_diff_kernel, t=t, nb=nb),
        grid=(bsz, DIFF_HEADS),
        in_specs=[pl.BlockSpec((4, HEAD_DIM), lambda b, h: (0, 0)),
                  pl.BlockSpec((1, nb, dv, t), lambda b, h: (b, 0, h, 0)),
                  pl.BlockSpec((1, s, dv), lambda b, h: (b, 0, h)),
                  pl.BlockSpec((1, 1, nb, dv + ONES_ROWS, t), lambda b, h: (b, h, 0, 0, 0)),
                  pl.BlockSpec((1, 2, t, t), lambda b, h: (h, 0, 0, 0)),
                  pl.BlockSpec((dv, 1), lambda b, h: (0, 0))],
        out_specs=pl.BlockSpec((1, s, dv), lambda b, h: (b, 0, h)),
        out_shape=jax.ShapeDtypeStruct((bsz, s, DIFF_W), BF16),
        scratch_shapes=[pltpu.VMEM((2, dv, 2 * t), BF16)]
        + [pltpu.VMEM((t + SCORE_EXTRA_ROWS, 2 * t), F32)] * 3
        + [pltpu.VMEM((2, 1, t), F32), pltpu.VMEM((2, dv + ONES_ROWS, t), F32)],
        compiler_params=_params("arbitrary", "arbitrary"),
        name="diff_attn",
    )(lam_vecs, qt, k, vt, bias_tiles, norm_g.reshape(dv, 1))


def _post_attn_kernel(x_ref, yf_ref, yd_ref, sc1_ref, sh1_ref, g1_ref, sc2_ref, sh2_ref,
                      ln1_ref, ln2_ref, wgl_ref, bm_ref, wbf_ref, wbd_ref, wo_ref, wr_ref, br_ref,
                      x1_ref, h2_ref, idx_ref, rank_ref, gate_ref, cnt_ref, carry_ref, *, tm):
    i = pl.program_id(0)

    @pl.when(i == 0)
    def _():
        carry_ref[...] = jnp.zeros_like(carry_ref)

    d = x_ref.shape[1]
    x = x_ref[...]
    h1 = _modulated_rmsnorm(x, ln1_ref[...], sc1_ref[0], sh1_ref[0]).astype(BF16)
    gl = _dot(h1, wgl_ref[...]) + bm_ref[...]
    gates = 1.0 / (1.0 + jnp.exp(-gl))
    merged = (gates[:, :d] * _dot(yf_ref[...], wbf_ref[...])
              + gates[:, d:] * _dot(yd_ref[...], wbd_ref[...]))
    x1 = x + g1_ref[0] * _dot(merged.astype(BF16), wo_ref[...])
    x1_ref[...] = x1
    h2 = _modulated_rmsnorm(x1, ln2_ref[...], sc2_ref[0], sh2_ref[0])
    _store_rows_tiled(h2_ref, h2)

    logits = _dot_split(wr_ref[...], h2, _dot_nt) + br_ref[...]
    row = lax.broadcasted_iota(I32, (N_EXPERTS, tm), 0)
    cur = logits
    vals, idxs = [], []
    for _ in range(TOP_K):
        mx = jnp.max(cur, axis=0, keepdims=True)
        ik = jnp.min(jnp.where(cur == mx, row, N_EXPERTS), axis=0, keepdims=True)
        vals.append(mx)
        idxs.append(ik)
        cur = jnp.where(row == ik, -jnp.inf, cur)
    exps = [jnp.exp(v - vals[0]) for v in vals]
    denom = exps[0] + exps[1] + exps[2] + exps[3]
    gate_rows = [e / denom for e in exps]

    onehots = [row == ik for ik in idxs]
    cnt = jnp.zeros((N_EXPERTS, tm), F32)
    for oh in onehots:
        cnt = cnt + jnp.where(oh, 1.0, 0.0)
    r = lax.broadcasted_iota(I32, (tm, tm), 0)
    c = lax.broadcasted_iota(I32, (tm, tm), 1)
    before = jnp.where(r < c, 1.0, 0.0).astype(BF16)
    prior = carry_ref[:, 0:1] + _dot(cnt.astype(BF16), before)
    for k in range(TOP_K):
        rk = jnp.sum(jnp.where(onehots[k], prior, 0.0), axis=0, keepdims=True)
        rank_ref[k:k + 1, :] = rk.astype(I32)
        idx_ref[k:k + 1, :] = idxs[k]
    carry_ref[...] = carry_ref[...] + jnp.sum(cnt, axis=1, keepdims=True)
    cnt_ref[...] = carry_ref[...]

    lrow = lax.broadcasted_iota(I32, (LANES, tm), 0)
    g_t = jnp.zeros((LANES, tm), F32)
    for k in range(TOP_K):
        g_t = jnp.where(lrow == k, gate_rows[k], g_t)
    gate_ref[...] = g_t.T


def _post_attn(x2, yf, yd, mods, ln1_g, ln2_g, b_merge, w_gl, w_brf, w_brd, w_o, w_r, b_r, s, tm):
    n, d = x2.shape
    nbb = s // tm
    bsz = n // s
    sc1, sh1, g1, sc2, sh2 = [m.reshape(bsz, 1, d) for m in mods]
    row_spec = lambda w: pl.BlockSpec((tm, w), lambda i: (i, 0))
    mod_spec = pl.BlockSpec((1, 1, d), lambda i: (i // nbb, 0, 0))
    full = lambda shape: pl.BlockSpec(shape, lambda i: (0,) * len(shape))
    tok_spec = pl.BlockSpec((TOP_K, tm), lambda i: (0, i))
    return pl.pallas_call(
        functools.partial(_post_attn_kernel, tm=tm),
        grid=(n // tm,),
        in_specs=[row_spec(d), row_spec(FOX_W), row_spec(DIFF_W),
                  mod_spec, mod_spec, mod_spec, mod_spec, mod_spec,
                  full((1, d)), full((1, d)), full(w_gl.shape), full((1, 2 * d)),
                  full(w_brf.shape), full(w_brd.shape), full(w_o.shape),
                  full((N_EXPERTS, d)), full((N_EXPERTS, 1))],
        out_specs=[row_spec(d), pl.BlockSpec(_tiled_shape(tm, d), lambda i: (i, 0)),
                   tok_spec, tok_spec, row_spec(LANES), full((N_EXPERTS, LANES))],
        out_shape=[jax.ShapeDtypeStruct((n, d), F32), jax.ShapeDtypeStruct(_tiled_shape(n, d), F32),
                   jax.ShapeDtypeStruct((TOP_K, n), I32), jax.ShapeDtypeStruct((TOP_K, n), I32),
                   jax.ShapeDtypeStruct((n, LANES), F32),
                   jax.ShapeDtypeStruct((N_EXPERTS, LANES), F32)],
        scratch_shapes=[pltpu.VMEM((N_EXPERTS, LANES), F32)],
        compiler_params=_params("arbitrary"),
        name="post_attn",
    )(x2, yf, yd, sc1, sh1, g1, sc2, sh2, ln1_g.reshape(1, d), ln2_g.reshape(1, d),
      w_gl.astype(BF16), b_merge.reshape(1, 2 * d), w_brf.astype(BF16), w_brd.astype(BF16),
      w_o.astype(BF16), w_r.T, b_r.reshape(N_EXPERTS, 1))


def _route_kernel(cnt_ref, idx_ref, rank_ref, dest_ref, be_ref, pend_ref, *, blk, nblk_lanes):
    shift = blk.bit_length() - 1
    idx = idx_ref[...]
    start_of = jnp.zeros(idx.shape, I32)
    blk_start = lax.broadcasted_iota(I32, (1, nblk_lanes), 1) * blk
    lane = lax.broadcasted_iota(I32, (1, LANES), 1)
    blk_e = jnp.zeros((1, nblk_lanes), I32)
    pends = jnp.zeros((1, LANES), I32)
    pend = jnp.int32(0)
    for e in range(N_EXPERTS):
        padded = ((cnt_ref[e] + (blk - 1)) >> shift) << shift
        start_of = jnp.where(idx == e, pend, start_of)
        pend = pend + padded
        blk_e = blk_e + jnp.where(blk_start >= pend, 1, 0)
        pends = jnp.where(lane == e, pend, pends)
    dest_ref[...] = start_of + rank_ref[...]
    be_ref[...] = jnp.minimum(blk_e, N_EXPERTS - 1)
    pend_ref[...] = pends


def _route(counts, idx, rank, blk):
    n = idx.shape[1]
    nblk = (n * TOP_K + blk - 1) // blk + N_EXPERTS
    nblk_lanes = -(-nblk // LANES) * LANES
    dest, blk_e, pends = pl.pallas_call(
        functools.partial(_route_kernel, blk=blk, nblk_lanes=nblk_lanes),
        in_specs=[pl.BlockSpec(memory_space=pltpu.SMEM),
                  pl.BlockSpec(memory_space=pltpu.VMEM),
                  pl.BlockSpec(memory_space=pltpu.VMEM)],
        out_specs=[pl.BlockSpec(memory_space=pltpu.VMEM)] * 3,
        out_shape=[jax.ShapeDtypeStruct((TOP_K, n), I32),
                   jax.ShapeDtypeStruct((1, nblk_lanes), I32),
                   jax.ShapeDtypeStruct((1, LANES), I32)],
        compiler_params=pltpu.CompilerParams(vmem_limit_bytes=VMEM_LIMIT),
        name="route",
    )(counts, idx, rank)
    return dest, blk_e[0, :nblk], pends[0, :N_EXPERTS], nblk


def _dispatch_kernel(pend_ref, dest_ref, h_ref, xs_ref, zero_ref, sem, *, tm, blk, nblk):
    shift = blk.bit_length() - 1

    @pl.when(pl.program_id(0) == 0)
    def _():
        zero_ref[...] = jnp.zeros(zero_ref.shape, F32)

        def zero_block(b):
            lines = blk * ROW_TILE
            return pltpu.make_async_copy(
                zero_ref, xs_ref.at[pl.ds(pl.multiple_of(b * lines, lines), lines)], sem)

        def tails(fn):
            for e in range(N_EXPERTS):
                prev = pend_ref[e - 1] if e else 0

                @pl.when(pend_ref[e] > prev)
                def _():
                    fn(zero_block((pend_ref[e] >> shift) - 1))

        def rest(fn):
            def body(b, carry):
                fn(zero_block(b))
                return carry
            lax.fori_loop(pend_ref[N_EXPERTS - 1] >> shift, nblk, body, 0)

        tails(lambda cp: cp.start())
        rest(lambda cp: cp.start())
        tails(lambda cp: cp.wait())
        rest(lambda cp: cp.wait())

    def row_copy(t, k):
        return pltpu.make_async_copy(_row_tile(h_ref, t), _row_tile(xs_ref, dest_ref[k, t]), sem)

    def start(t, carry):
        for k in range(TOP_K):
            row_copy(t, k).start(priority=k % 2)
        return carry

    def wait(t, carry):
        for k in range(TOP_K):
            row_copy(t, k).wait()
        return carry

    lax.fori_loop(0, tm, start, 0, unroll=ROW_DMA_UNROLL)
    lax.fori_loop(0, tm, wait, 0, unroll=ROW_DMA_UNROLL)


def _dispatch(pends, dest, h2, tm, blk, nblk):
    n = h2.shape[0] // ROW_TILE
    d = ROW_TILE * LANES
    return pl.pallas_call(
        functools.partial(_dispatch_kernel, tm=tm, blk=blk, nblk=nblk),
        grid=(n // tm,),
        in_specs=[pl.BlockSpec(memory_space=pltpu.SMEM),
                  pl.BlockSpec((TOP_K, tm), lambda i: (0, i), memory_space=pltpu.SMEM),
                  pl.BlockSpec(_tiled_shape(tm, d), lambda i: (i, 0))],
        out_specs=pl.BlockSpec(memory_space=pl.ANY),
        out_shape=jax.ShapeDtypeStruct(_tiled_shape(nblk * blk, d), F32),
        scratch_shapes=[pltpu.VMEM(_tiled_shape(blk, d), F32), pltpu.SemaphoreType.DMA(())],
        compiler_params=_params("arbitrary"),
        name="dispatch",
    )(pends, dest, h2)


ROW_DMA_UNROLL = 4
UP_CHUNK = 2048
PICK_GROUP = 512


def _expert_kernel(be_ref, pend_ref, x_ref, wu_ref, bu_ref, wd_ref, bd_ref, o_ref,
                   wu_sc, wd_sc, *, blk):
    b = pl.program_id(0)
    live = b * blk < pend_ref[N_EXPERTS - 1]
    new_expert = (b == 0) | (be_ref[b] != be_ref[jnp.maximum(b - 1, 0)])

    @pl.when(live & new_expert)
    def _():
        wu_sc[...] = wu_ref[0].astype(BF16)
        wd_sc[...] = wd_ref[0].astype(BF16)

    @pl.when(live)
    def _():
        x = _load_rows_tiled(x_ref).astype(BF16)
        r = lax.broadcasted_iota(I32, (PICK_GROUP, PICK_GROUP // 2), 0)
        c = lax.broadcasted_iota(I32, (PICK_GROUP, PICK_GROUP // 2), 1)
        even = jnp.where(r == 2 * c, 1.0, 0.0).astype(BF16)
        acts = []
        for n in range(wu_sc.shape[1] // UP_CHUNK):
            cols = slice(n * UP_CHUNK, (n + 1) * UP_CHUNK)
            u = _dot(x, wu_sc[:, cols]) + bu_ref[0, :, cols]
            u_glu = jnp.minimum(u, SWIGLU_LIMIT)
            u_lin = jnp.clip(u, -SWIGLU_LIMIT, SWIGLU_LIMIT) + 1.0
            z = (u_glu * (1.0 / (1.0 + jnp.exp(-SWIGLU_ALPHA * u_glu)))
                 * pltpu.roll(u_lin, UP_CHUNK - 1, 1)).astype(BF16)
            for g in range(UP_CHUNK // PICK_GROUP):
                acts.append(_dot(z[:, g * PICK_GROUP:(g + 1) * PICK_GROUP], even).astype(BF16))
        _store_rows_tiled(o_ref, _dot(jnp.concatenate(acts, axis=1), wd_sc[...]) + bd_ref[0])

    @pl.when(jnp.logical_not(live))
    def _():
        o_ref[...] = jnp.zeros(o_ref.shape, F32)


def _experts(blk_e, pends, xs, w_up, b_up, w_down, b_down, blk):
    p = xs.shape[0] // ROW_TILE
    ne, d, f2 = w_up.shape
    f = w_down.shape[1]
    shift = blk.bit_length() - 1

    def live(b, be, pe):
        return jnp.minimum(b, (pe[N_EXPERTS - 1] >> shift) - 1)

    x_spec = pl.BlockSpec(_tiled_shape(blk, d), lambda b, be, pe: (live(b, be, pe), 0))
    w_spec = lambda r, c: pl.BlockSpec((1, r, c), lambda b, be, pe: (be[live(b, be, pe)], 0, 0))
    return pl.pallas_call(
        functools.partial(_expert_kernel, blk=blk),
        grid_spec=pltpu.PrefetchScalarGridSpec(
            num_scalar_prefetch=2,
            grid=(p // blk,),
            in_specs=[x_spec, w_spec(d, f2), w_spec(1, f2), w_spec(f, d), w_spec(1, d)],
            out_specs=pl.BlockSpec(_tiled_shape(blk, d), lambda b, be, pe: (b, 0)),
            scratch_shapes=[pltpu.VMEM((d, f2), BF16), pltpu.VMEM((f, d), BF16)]),
        out_shape=jax.ShapeDtypeStruct(_tiled_shape(p, d), F32),
        compiler_params=_params("arbitrary"),
        name="experts",
    )(blk_e, pends, xs, w_up, b_up.reshape(ne, 1, f2), w_down, b_down.reshape(ne, 1, d))


def _combine_kernel(dest_ref, x1_ref, gate_ref, g2_ref, fg_ref, ys_ref, o_ref, ybuf_ref, sem, *, tm):
    def row_copy(t, k):
        return pltpu.make_async_copy(_row_tile(ys_ref, dest_ref[k, t]), _row_tile(ybuf_ref.at[k], t), sem)

    def start(t, carry):
        for k in range(TOP_K):
            row_copy(t, k).start(priority=k % 2)
        return carry

    def wait(t, carry):
        for k in range(TOP_K):
            row_copy(t, k).wait()
        return carry

    lax.fori_loop(0, tm, start, 0, unroll=ROW_DMA_UNROLL)
    lax.fori_loop(0, tm, wait, 0, unroll=ROW_DMA_UNROLL)

    g = gate_ref[...]
    moe = g[:, 0:1] * _load_rows_tiled(ybuf_ref.at[0])
    for k in range(1, TOP_K):
        moe = moe + g[:, k:k + 1] * _load_rows_tiled(ybuf_ref.at[k])
    xo = x1_ref[...] + g2_ref[0] * moe
    o_ref[...] = xo * lax.rsqrt(jnp.mean(xo * xo, axis=-1, keepdims=True) + NORM_EPS) * fg_ref[...]


def _combine(dest, x1, gates, g2, final_g, ys, s, tm):
    n, d = x1.shape
    nbb = s // tm
    bsz = n // s
    return pl.pallas_call(
        functools.partial(_combine_kernel, tm=tm),
        grid=(n // tm,),
        in_specs=[pl.BlockSpec((TOP_K, tm), lambda i: (0, i), memory_space=pltpu.SMEM),
                  pl.BlockSpec((tm, d), lambda i: (i, 0)),
                  pl.BlockSpec((tm, LANES), lambda i: (i, 0)),
                  pl.BlockSpec((1, 1, d), lambda i: (i // nbb, 0, 0)),
                  pl.BlockSpec((1, d), lambda i: (0, 0)),
                  pl.BlockSpec(memory_space=pl.ANY)],
        out_specs=pl.BlockSpec((tm, d), lambda i: (i, 0)),
        out_shape=jax.ShapeDtypeStruct((n, d), F32),
        scratch_shapes=[pltpu.VMEM((TOP_K, *_tiled_shape(tm, d)), F32), pltpu.SemaphoreType.DMA(())],
        compiler_params=_params("arbitrary"),
        name="combine",
    )(dest, x1, gates, g2.reshape(bsz, 1, d), final_g.reshape(1, d), ys)


def kernel(x, c, w_ada, b_ada, ln1_g, w_in, b_fgate, b_merge, lam_q1, lam_k1, lam_q2, lam_k2,
           diff_norm_g, t5_bias, w_br_fox, w_br_diff, w_o, ln2_g, w_router, b_router,
           w_up, b_up, w_down, b_down, final_g):
    bsz, s, d = x.shape
    n = bsz * s
    t = min(512, s)
    tr = min(256, s)
    assert s % t == 0 and w_ada.shape[0] == 1

    mod = _adaln_mod(c, w_ada[0], b_ada[0])
    sh1, sc1, g1, sh2, sc2, g2 = jnp.split(mod, 6, axis=-1)

    qtf, fqt, kpf, vtf, qtd, kd, vtd = _in_proj(x, sc1, sh1, ln1_g[0], w_in[0], b_fgate[0], t)
    y_fox = _fox_attention(qtf, fqt, kpf, vtf, t)
    lam_vecs = jnp.stack([lam_q1[0], lam_k1[0], lam_q2[0], lam_k2[0]])
    y_diff = _diff_attention(lam_vecs, qtd, kd, vtd, _bias_tiles(t5_bias, t), diff_norm_g[0], t)

    w_gl = w_in[0][:, w_in.shape[2] - 2 * d:]
    x1, h2, idx, rank, gates, counts = _post_attn(
        x.reshape(n, d), y_fox.reshape(n, FOX_W), y_diff.reshape(n, DIFF_W),
        (sc1, sh1, g1, sc2, sh2), ln1_g[0], ln2_g[0], b_merge[0], w_gl,
        w_br_fox[0], w_br_diff[0], w_o[0], w_router[0], b_router[0], s, t)

    dest, blk_e, pends, nblk = _route(counts[:, 0].astype(I32), idx, rank, EXPERT_BLOCK)
    xs = _dispatch(pends, dest, h2, tr, EXPERT_BLOCK, nblk)
    ys = _experts(blk_e, pends, xs, w_up[0], b_up[0], w_down[0], b_down[0], EXPERT_BLOCK)
    out = _combine(dest, x1, gates, g2, final_g, ys, s, tr)
    return out.reshape(bsz, s, d)
```

```python
import functools
import math

import numpy as np
import jax
import jax.numpy as jnp
from jax import lax
from jax.experimental import pallas as pl
from jax.experimental.pallas import tpu as pltpu

F32 = jnp.float32
BF16 = jnp.bfloat16
I32 = jnp.int32

HEAD_DIM = 64
FOX_HEADS = 8
DIFF_HEADS = 4
FOX_W = FOX_HEADS * HEAD_DIM
DIFF_W = DIFF_HEADS * 2 * HEAD_DIM
T5_BUCKETS = 32
T5_MAX_DIST = 128
N_EXPERTS = 32
TOP_K = 4
EXPERT_BLOCK = 512
SWIGLU_ALPHA = 1.702
SWIGLU_LIMIT = 7.0
NORM_EPS = 1e-6
SUBLN_EPS = 1e-5
LAMBDA_INIT = 0.8 - 0.6 * math.exp(-0.3 * 0)

LANES = 128
NEG = -1e30
LOG2E = math.log2(math.e)
ONES_ROWS = 16
VMEM_LIMIT = 56 * 1024 * 1024

NT_DIMS = (((1,), (1,)), ((), ()))


def _dot(a, b):
    return jnp.dot(a, b, preferred_element_type=F32)


def _dot_nt(a, b):
    return lax.dot_general(a, b, NT_DIMS, preferred_element_type=F32)


def _split2(v):
    hi = v.astype(BF16)
    return hi, (v - hi.astype(F32)).astype(BF16)


def _dot_split(a, b, dot=_dot):
    a_hi, a_lo = _split2(a)
    b_hi, b_lo = _split2(b)
    return dot(a_hi, b_hi) + (dot(a_hi, b_lo) + dot(a_lo, b_hi))


def _params(*sem):
    return pltpu.CompilerParams(dimension_semantics=sem, vmem_limit_bytes=VMEM_LIMIT)


def _modulated_rmsnorm(x, g, scale, shift):
    y = x * lax.rsqrt(jnp.mean(x * x, axis=-1, keepdims=True) + NORM_EPS)
    return (y * g) * (1.0 + scale) + shift


ROW_TILE = 8


def _tiled_shape(n, d):
    assert d == ROW_TILE * LANES
    return (n * ROW_TILE, LANES)


def _row_tile(ref, r):
    return ref.at[pl.ds(pl.multiple_of(r * ROW_TILE, ROW_TILE), ROW_TILE)]


def _store_rows_tiled(ref, rows):
    n = rows.shape[0]
    for a in range(ROW_TILE):
        ref[pl.ds(a, n, stride=ROW_TILE), :] = rows[:, a * LANES:(a + 1) * LANES]


def _load_rows_tiled(ref):
    n = ref.shape[0] // ROW_TILE
    return jnp.concatenate([ref[pl.ds(a, n, stride=ROW_TILE), :] for a in range(ROW_TILE)], axis=1)


def _split3(v):
    hi = v.astype(BF16)
    r1 = v - hi.astype(F32)
    mid = r1.astype(BF16)
    lo = (r1 - mid.astype(F32)).astype(BF16)
    return hi, mid, lo


def _adaln_kernel(c_ref, w_ref, b_ref, o_ref):
    c = c_ref[...]
    s = c / (1.0 + jnp.exp(-c))
    o_ref[...] = _dot_split(s, w_ref[...]) + b_ref[...]


def _adaln_mod(c, w, b):
    bsz, d = c.shape
    n = w.shape[1]
    rows = 8
    tn = 1536
    c8 = jnp.zeros((rows, d), F32).at[:bsz].set(c)
    out = pl.pallas_call(
        _adaln_kernel,
        grid=(n // tn,),
        in_specs=[pl.BlockSpec((rows, d), lambda j: (0, 0)),
                  pl.BlockSpec((d, tn), lambda j: (0, j)),
                  pl.BlockSpec((1, tn), lambda j: (0, j))],
        out_specs=pl.BlockSpec((rows, tn), lambda j: (0, j)),
        out_shape=jax.ShapeDtypeStruct((rows, n), F32),
        compiler_params=_params("arbitrary"),
        name="adaln_mod",
    )(c8, w, b.reshape(1, n))
    return out[:bsz]


def _bias_tile_kernel(tb_ref, o_ref, *, t):
    h = pl.program_id(0)
    key = lax.broadcasted_iota(I32, (t, t), 0)
    qry = lax.broadcasted_iota(I32, (t, t), 1)
    far = tb_ref[h, T5_BUCKETS - 1]
    max_exact = T5_BUCKETS // 2
    for which in range(2):
        rel = qry - key + which * t
        n = jnp.maximum(rel, 0)
        nf = jnp.maximum(n, max_exact).astype(F32)
        large = max_exact + (jnp.log(nf / max_exact) / math.log(T5_MAX_DIST / max_exact)
                             * (T5_BUCKETS - max_exact)).astype(I32)
        large = jnp.minimum(large, T5_BUCKETS - 1)
        bucket = jnp.where(n < max_exact, n, large)
        bias = jnp.zeros((t, t), F32)
        for b in range(T5_BUCKETS):
            bias = jnp.where(bucket == b, tb_ref[h, b] - far, bias)
        o_ref[0, which] = jnp.where(rel >= 0, bias * LOG2E, NEG)


def _bias_tiles(t5_bias, t):
    tb = t5_bias.T.astype(F32)
    return pl.pallas_call(
        functools.partial(_bias_tile_kernel, t=t),
        grid=(DIFF_HEADS,),
        in_specs=[pl.BlockSpec(memory_space=pltpu.SMEM)],
        out_specs=pl.BlockSpec((1, 2, t, t), lambda h: (h, 0, 0, 0)),
        out_shape=jax.ShapeDtypeStruct((DIFF_HEADS, 2, t, t), F32),
        compiler_params=_params("arbitrary"),
        name="bias_tiles",
    )(tb)


def _fparts_perm():
    npair = FOX_HEADS // 2
    pk = np.zeros((3 * LANES, npair * LANES), np.float32)
    pq = np.zeros((npair * 32, 3 * LANES), np.float32)
    k_ones = np.zeros((1, npair * LANES), np.float32)
    q_ones = np.zeros((npair * 32, 1), np.float32)
    for p in range(npair):
        for hh in range(2):
            head = 2 * p + hh
            for part in range(3):
                pk[part * LANES + head, p * LANES + hh * 16 + 3 + part] = -1.0
                k_ones[0, p * LANES + hh * 16 + part] = 1.0
                pq[p * 32 + hh * 16 + part, part * LANES + head] = 1.0
                q_ones[p * 32 + hh * 16 + 3 + part, 0] = 1.0
    return pk, pq, k_ones, q_ones


def _in_proj_kernel(x_ref, sc_ref, sh_ref, g_ref, wt_ref, wn_ref, wfl_ref, bf_ref,
                    pk_ref, pq_ref, kone_ref, qone_ref,
                    qtf_ref, fqt_ref, kpf_ref, vtf_ref, qtd_ref, kd_ref, vtd_ref,
                    carry_ref, *, tm):
    i = pl.program_id(1)

    @pl.when(i == 0)
    def _():
        carry_ref[...] = jnp.zeros_like(carry_ref)

    h = _modulated_rmsnorm(x_ref[0], g_ref[...], sc_ref[0], sh_ref[0])
    hb = h.astype(BF16)

    h_lo = (h - hb.astype(F32)).astype(BF16)
    wfl_hi, wfl_lo = _split2(wfl_ref[...])
    fl = _dot(hb, wfl_hi) + (_dot(hb, wfl_lo) + _dot(h_lo, wfl_hi)) + bf_ref[...]
    logf = jnp.minimum(fl, 0.0) - jnp.log(1.0 + jnp.exp(-jnp.abs(fl)))
    r = lax.broadcasted_iota(I32, (tm, tm), 0)
    c = lax.broadcasted_iota(I32, (tm, tm), 1)
    tri = jnp.where(c <= r, 1.0, 0.0).astype(BF16)
    psum = _dot(tri, jnp.concatenate(_split3(logf), axis=1))
    fsum = (psum[:, 0:LANES] + (psum[:, LANES:2 * LANES] + psum[:, 2 * LANES:3 * LANES])
            + carry_ref[...])
    carry_ref[...] = fsum[tm - 1:tm, :]
    parts = jnp.concatenate(_split3(fsum * LOG2E), axis=1)

    scale = HEAD_DIM ** -0.5 * LOG2E
    yt = _dot_nt(wt_ref[...], hb)
    yn = _dot(hb, wn_ref[...])

    ones_rows = jnp.where(lax.broadcasted_iota(I32, (ONES_ROWS, tm), 0) == 0, 1.0, 0.0).astype(BF16)
    qtf_ref[0, 0] = (yt[0:FOX_W] * scale).astype(BF16)
    vt = yt[FOX_W:2 * FOX_W].astype(BF16)
    for hd in range(FOX_HEADS):
        vtf_ref[0, hd, 0, 0:HEAD_DIM] = vt[hd * HEAD_DIM:(hd + 1) * HEAD_DIM]
        vtf_ref[0, hd, 0, HEAD_DIM:HEAD_DIM + ONES_ROWS] = ones_rows
    qtd_ref[0, 0] = (yt[2 * FOX_W:2 * FOX_W + DIFF_W] * scale).astype(BF16)
    vtd = yt[2 * FOX_W + DIFF_W:2 * FOX_W + 2 * DIFF_W].astype(BF16)
    dv = 2 * HEAD_DIM
    for hd in range(DIFF_HEADS):
        vtd_ref[0, hd, 0, 0:dv] = vtd[hd * dv:(hd + 1) * dv]
        vtd_ref[0, hd, 0, dv:dv + ONES_ROWS] = ones_rows

    fqt_ref[0, 0] = (_dot_nt(pq_ref[...], parts) + qone_ref[...]).astype(BF16)
    kext = (_dot(parts, pk_ref[...]) + kone_ref[...]).astype(BF16)
    kf = yn[:, 0:FOX_W].astype(BF16)
    for p in range(FOX_HEADS // 2):
        kpf_ref[0, :, 2 * p * LANES:(2 * p + 1) * LANES] = kf[:, p * LANES:(p + 1) * LANES]
        kpf_ref[0, :, (2 * p + 1) * LANES:(2 * p + 2) * LANES] = kext[:, p * LANES:(p + 1) * LANES]
    kd_ref[0] = yn[:, FOX_W:FOX_W + DIFF_W].astype(BF16)


def _in_proj(x, sc1, sh1, ln_g, w_in, b_fgate, t):
    bsz, s, d = x.shape
    nb = s // t
    cuts = np.cumsum([FOX_W, FOX_W, FOX_W, FOX_HEADS, DIFF_W, DIFF_W, DIFF_W])
    w_fq, w_fk, w_fv, w_fl, w_dq, w_dk, w_dv = [
        w_in[:, a:b] for a, b in zip([0, *cuts[:-1]], cuts)]
    wt = jnp.concatenate([w_fq, w_fv, w_dq, w_dv], axis=1).T.astype(BF16)
    wn = jnp.concatenate([w_fk, w_dk], axis=1).astype(BF16)
    wfl = jnp.zeros((d, LANES), F32).at[:, :FOX_HEADS].set(w_fl)
    bfl = jnp.zeros((1, LANES), F32).at[0, :FOX_HEADS].set(b_fgate)
    pk, pq, k_ones, q_ones = _fparts_perm()
    npair = FOX_HEADS // 2

    full = lambda shape: pl.BlockSpec(shape, lambda b, i: (0,) * len(shape))
    outs = pl.pallas_call(
        functools.partial(_in_proj_kernel, tm=t),
        grid=(bsz, nb),
        in_specs=[pl.BlockSpec((1, t, d), lambda b, i: (b, i, 0)),
                  pl.BlockSpec((1, 1, d), lambda b, i: (b, 0, 0)),
                  pl.BlockSpec((1, 1, d), lambda b, i: (b, 0, 0)),
                  full((1, d)), full(wt.shape), full(wn.shape), full(wfl.shape), full(bfl.shape),
                  full(pk.shape), full(pq.shape), full(k_ones.shape), full(q_ones.shape)],
        out_specs=[pl.BlockSpec((1, 1, FOX_W, t), lambda b, i: (b, i, 0, 0)),
                   pl.BlockSpec((1, 1, npair * 32, t), lambda b, i: (b, i, 0, 0)),
                   pl.BlockSpec((1, t, npair * 2 * LANES), lambda b, i: (b, i, 0)),
                   pl.BlockSpec((1, FOX_HEADS, 1, HEAD_DIM + ONES_ROWS, t),
                                lambda b, i: (b, 0, i, 0, 0)),
                   pl.BlockSpec((1, 1, DIFF_W, t), lambda b, i: (b, i, 0, 0)),
                   pl.BlockSpec((1, t, DIFF_W), lambda b, i: (b, i, 0)),
                   pl.BlockSpec((1, DIFF_HEADS, 1, 2 * HEAD_DIM + ONES_ROWS, t),
                                lambda b, i: (b, 0, i, 0, 0))],
        out_shape=[jax.ShapeDtypeStruct((bsz, nb, FOX_W, t), BF16),
                   jax.ShapeDtypeStruct((bsz, nb, npair * 32, t), BF16),
                   jax.ShapeDtypeStruct((bsz, s, npair * 2 * LANES), BF16),
                   jax.ShapeDtypeStruct((bsz, FOX_HEADS, nb, HEAD_DIM + ONES_ROWS, t), BF16),
                   jax.ShapeDtypeStruct((bsz, nb, DIFF_W, t), BF16),
                   jax.ShapeDtypeStruct((bsz, s, DIFF_W), BF16),
                   jax.ShapeDtypeStruct((bsz, DIFF_HEADS, nb, 2 * HEAD_DIM + ONES_ROWS, t), BF16)],
        scratch_shapes=[pltpu.VMEM((1, LANES), F32)],
        compiler_params=_params("arbitrary", "arbitrary"),
        name="in_proj",
    )(x, sc1.reshape(bsz, 1, d), sh1.reshape(bsz, 1, d), ln_g.reshape(1, d), wt, wn, wfl, bfl,
      jnp.asarray(pk, BF16), jnp.asarray(pq, BF16), jnp.asarray(k_ones), jnp.asarray(q_ones))
    return outs


SCORE_EXTRA_ROWS = 8


def _store_scores(s_ref, s, t):
    s_ref[0:t, :] = s
    s_ref[t:t + 1, :] = jnp.max(s, axis=0, keepdims=True)


def _softmax_block(s, s_max, vt, m_ref, acc_ref, slot):
    m_old = m_ref[slot]
    m_new = jnp.maximum(m_old, s_max)
    alpha = jnp.exp2(m_old - m_new)
    p = jnp.exp2(s - m_new)
    acc_ref[slot] = alpha * acc_ref[slot] + _dot(vt, p.astype(BF16))
    m_ref[slot] = m_new


def _init_softmax_state(m_ref, acc_ref):
    m_ref[...] = jnp.full(m_ref.shape, NEG, F32)
    acc_ref[...] = jnp.zeros(acc_ref.shape, F32)


SCAN_BLOCKS_PER_TRIP = 4


def _attend_all_query_blocks(nb, n_tail, prepare_q, qk_into, process, begin, finish, bufs):
    ring, first_buf = bufs[:2], bufs[2]
    prepare_q(0, 0)
    qk_into(first_buf, 0, 0)

    def query_block(i, carry):
        slot = i & 1

        def start_next_query_block():
            prepare_q(jnp.minimum(i + 1, nb - 1), 1 - slot)
            qk_into(first_buf, 1 - slot, 0)

        def run(items, ends):
            for n, (buf, j, kind) in enumerate(items):
                last = n + 1 == len(items)
                if not last:
                    qk_into(items[n + 1][0], slot, items[n + 1][1])
                elif ends and buf is not first_buf:
                    start_next_query_block()
                process(buf, i, j, kind)
                if last and ends and buf is first_buf:
                    start_next_query_block()

        begin()
        for small in range(n_tail):
            items = [(first_buf, 0, small)] + [(ring[n % 2], n + 1, small - 1 - n) for n in range(small)]
            pl.when(i == small)(functools.partial(run, items, True))

        @pl.when(i >= n_tail)
        def _():
            n_plain = i - n_tail
            n_trips = n_plain // SCAN_BLOCKS_PER_TRIP
            rem = n_plain % SCAN_BLOCKS_PER_TRIP
            qk_into(ring[0], slot, 1)
            process(first_buf, i, 0, None)

            def trip(n, carry):
                j = 1 + SCAN_BLOCKS_PER_TRIP * n
                for m in range(SCAN_BLOCKS_PER_TRIP):
                    qk_into(ring[(m + 1) % 2], slot, j + m + 1)
                    process(ring[m % 2], i, j + m, None)
                return carry

            lax.fori_loop(0, n_trips, trip, 0)
            first = 1 + SCAN_BLOCKS_PER_TRIP * n_trips
            for r in range(SCAN_BLOCKS_PER_TRIP):
                kinds = [None] * r + list(range(n_tail - 1, -1, -1))
                items = [(ring[n % 2], first + n, kind) for n, kind in enumerate(kinds)]
                pl.when(rem == r)(functools.partial(run, items, True))

        finish(i)
        return carry

    lax.fori_loop(0, nb, query_block, 0)


def _fox_kernel(q_ref, fq_ref, k_ref, v_ref, o_ref, qbd_ref, sa_ref, sb_ref, sc_ref, m_ref, acc_ref,
                *, t, nb):
    d = HEAD_DIM
    qbd_ref[...] = jnp.zeros(qbd_ref.shape, BF16)

    def prepare_q(i, slot):
        qbd_ref[slot, 0:d, 0:t] = q_ref[0, i, 0:d, :]
        qbd_ref[slot, d:2 * d, t:2 * t] = q_ref[0, i, d:2 * d, :]
        qbd_ref[slot, 2 * d:2 * d + 16, 0:t] = fq_ref[0, i, 0:16, :]
        qbd_ref[slot, 2 * d + 16:2 * d + 32, t:2 * t] = fq_ref[0, i, 16:32, :]

    def qk_into(s_ref, slot, j):
        kblk = k_ref[0, pl.ds(pl.multiple_of(j * t, t), t), :]
        _store_scores(s_ref, _dot(kblk, qbd_ref[slot]), t)

    def process(s_ref, i, j, kind):
        for hh in range(2):
            sh = s_ref[0:t, hh * t:(hh + 1) * t]
            if kind == 0:
                key = lax.broadcasted_iota(I32, (t, t), 0)
                qry = lax.broadcasted_iota(I32, (t, t), 1)
                sh = jnp.where(key <= qry, sh, NEG)
                s_max = jnp.max(sh, axis=0, keepdims=True)
            else:
                s_max = s_ref[t:t + 1, hh * t:(hh + 1) * t]
            _softmax_block(sh, s_max, v_ref[0, hh, j], m_ref, acc_ref, hh)

    def finish(i):
        outs = [acc_ref[hh, 0:d] / acc_ref[hh, d:d + 1] for hh in range(2)]
        o_ref[0, pl.ds(pl.multiple_of(i * t, t), t), :] = (
            jnp.concatenate(outs, axis=0).T.astype(BF16))

    _attend_all_query_blocks(nb, 1, prepare_q, qk_into, process,
                             functools.partial(_init_softmax_state, m_ref, acc_ref), finish,
                             (sa_ref, sb_ref, sc_ref))


def _fox_attention(qt, fqt, kp, vt, t):
    bsz, nb = qt.shape[:2]
    s = nb * t
    npair = FOX_HEADS // 2
    return pl.pallas_call(
        functools.partial(_fox_kernel, t=t, nb=nb),
        grid=(bsz, npair),
        in_specs=[pl.BlockSpec((1, nb, 2 * HEAD_DIM, t), lambda b, p: (b, 0, p, 0)),
                  pl.BlockSpec((1, nb, 32, t), lambda b, p: (b, 0, p, 0)),
                  pl.BlockSpec((1, s, 2 * LANES), lambda b, p: (b, 0, p)),
                  pl.BlockSpec((1, 2, nb, HEAD_DIM + ONES_ROWS, t), lambda b, p: (b, p, 0, 0, 0))],
        out_specs=pl.BlockSpec((1, s, 2 * HEAD_DIM), lambda b, p: (b, 0, p)),
        out_shape=jax.ShapeDtypeStruct((bsz, s, FOX_W), BF16),
        scratch_shapes=[pltpu.VMEM((2, 2 * LANES, 2 * t), BF16)]
        + [pltpu.VMEM((t + SCORE_EXTRA_ROWS, 2 * t), F32)] * 3
        + [pltpu.VMEM((2, 1, t), F32), pltpu.VMEM((2, HEAD_DIM + ONES_ROWS, t), F32)],
        compiler_params=_params("arbitrary", "arbitrary"),
        name="fox_attn",
    )(qt, fqt, kp, vt)


def _diff_kernel(lam_ref, q_ref, k_ref, v_ref, bias_ref, g_ref, o_ref,
                 qbd_ref, sa_ref, sb_ref, sc_ref, m_ref, acc_ref, *, t, nb):
    d = HEAD_DIM
    dv = 2 * HEAD_DIM
    qbd_ref[...] = jnp.zeros(qbd_ref.shape, BF16)

    def prepare_q(i, slot):
        qbd_ref[slot, 0:d, 0:t] = q_ref[0, i, 0:d, :]
        qbd_ref[slot, d:2 * d, t:2 * t] = q_ref[0, i, d:2 * d, :]

    def qk_into(s_ref, slot, j):
        kblk = k_ref[0, pl.ds(pl.multiple_of(j * t, t), t), :]
        _store_scores(s_ref, _dot(kblk, qbd_ref[slot]), t)

    def process(s_ref, i, j, kind):
        vt = v_ref[0, 0, j]
        for hh in range(2):
            sh = s_ref[0:t, hh * t:(hh + 1) * t]
            if kind is not None:
                sh = sh + bias_ref[0, kind]
                s_max = jnp.max(sh, axis=0, keepdims=True)
            else:
                s_max = s_ref[t:t + 1, hh * t:(hh + 1) * t]
            _softmax_block(sh, s_max, vt, m_ref, acc_ref, hh)

    def finish(i):
        lam = (jnp.exp(jnp.sum(lam_ref[0:1, :] * lam_ref[1:2, :], axis=1, keepdims=True))
               - jnp.exp(jnp.sum(lam_ref[2:3, :] * lam_ref[3:4, :], axis=1, keepdims=True))
               + LAMBDA_INIT)
        out = (acc_ref[0, 0:dv] / acc_ref[0, dv:dv + 1]
               - lam * (acc_ref[1, 0:dv] / acc_ref[1, dv:dv + 1]))
        ms = jnp.mean(out * out, axis=0, keepdims=True)
        out = out * lax.rsqrt(ms + SUBLN_EPS) * g_ref[...] * (1.0 - LAMBDA_INIT)
        o_ref[0, pl.ds(pl.multiple_of(i * t, t), t), :] = out.T.astype(BF16)

    _attend_all_query_blocks(nb, 2, prepare_q, qk_into, process,
                             functools.partial(_init_softmax_state, m_ref, acc_ref), finish,
                             (sa_ref, sb_ref, sc_ref))


def _diff_attention(lam_vecs, qt, k, vt, bias_tiles, norm_g, t):
    bsz, nb = qt.shape[:2]
    s = nb * t
    dv = 2 * HEAD_DIM
    return pl.pallas_call(
        functools.partial(_diff_kernel, t=t, nb=nb),
        grid=(bsz, DIFF_HEADS),
        in_specs=[pl.BlockSpec((4, HEAD_DIM), lambda b, h: (0, 0)),
                  pl.BlockSpec((1, nb, dv, t), lambda b, h: (b, 0, h, 0)),
                  pl.BlockSpec((1, s, dv), lambda b, h: (b, 0, h)),
                  pl.BlockSpec((1, 1, nb, dv + ONES_ROWS, t), lambda b, h: (b, h, 0, 0, 0)),
                  pl.BlockSpec((1, 2, t, t), lambda b, h: (h, 0, 0, 0)),
                  pl.BlockSpec((dv, 1), lambda b, h: (0, 0))],
        out_specs=pl.BlockSpec((1, s, dv), lambda b, h: (b, 0, h)),
        out_shape=jax.ShapeDtypeStruct((bsz, s, DIFF_W), BF16),
        scratch_shapes=[pltpu.VMEM((2, dv, 2 * t), BF16)]
        + [pltpu.VMEM((t + SCORE_EXTRA_ROWS, 2 * t), F32)] * 3
        + [pltpu.VMEM((2, 1, t), F32), pltpu.VMEM((2, dv + ONES_ROWS, t), F32)],
        compiler_params=_params("arbitrary", "arbitrary"),
        name="diff_attn",
    )(lam_vecs, qt, k, vt, bias_tiles, norm_g.reshape(dv, 1))


def _post_attn_kernel(x_ref, yf_ref, yd_ref, sc1_ref, sh1_ref, g1_ref, sc2_ref, sh2_ref,
                      ln1_ref, ln2_ref, wgl_ref, bm_ref, wbf_ref, wbd_ref, wo_ref, wr_ref, br_ref,
                      x1_ref, h2_ref, idx_ref, rank_ref, gate_ref, cnt_ref, carry_ref, *, tm):
    i = pl.program_id(0)

    @pl.when(i == 0)
    def _():
        carry_ref[...] = jnp.zeros_like(carry_ref)

    d = x_ref.shape[1]
    x = x_ref[...]
    h1 = _modulated_rmsnorm(x, ln1_ref[...], sc1_ref[0], sh1_ref[0]).astype(BF16)
    gl = _dot(h1, wgl_ref[...]) + bm_ref[...]
    gates = 1.0 / (1.0 + jnp.exp(-gl))
    merged = (gates[:, :d] * _dot(yf_ref[...], wbf_ref[...])
              + gates[:, d:] * _dot(yd_ref[...], wbd_ref[...]))
    x1 = x + g1_ref[0] * _dot(merged.astype(BF16), wo_ref[...])
    x1_ref[...] = x1
    h2 = _modulated_rmsnorm(x1, ln2_ref[...], sc2_ref[0], sh2_ref[0])
    _store_rows_tiled(h2_ref, h2)

    logits = _dot_split(wr_ref[...], h2, _dot_nt) + br_ref[...]
    row = lax.broadcasted_iota(I32, (N_EXPERTS, tm), 0)
    cur = logits
    vals, idxs = [], []
    for _ in range(TOP_K):
        mx = jnp.max(cur, axis=0, keepdims=True)
        ik = jnp.min(jnp.where(cur == mx, row, N_EXPERTS), axis=0, keepdims=True)
        vals.append(mx)
        idxs.append(ik)
        cur = jnp.where(row == ik, -jnp.inf, cur)
    exps = [jnp.exp(v - vals[0]) for v in vals]
    denom = exps[0] + exps[1] + exps[2] + exps[3]
    gate_rows = [e / denom for e in exps]

    onehots = [row == ik for ik in idxs]
    cnt = jnp.zeros((N_EXPERTS, tm), F32)
    for oh in onehots:
        cnt = cnt + jnp.where(oh, 1.0, 0.0)
    r = lax.broadcasted_iota(I32, (tm, tm), 0)
    c = lax.broadcasted_iota(I32, (tm, tm), 1)
    before = jnp.where(r < c, 1.0, 0.0).astype(BF16)
    prior = carry_ref[:, 0:1] + _dot(cnt.astype(BF16), before)
    for k in range(TOP_K):
        rk = jnp.sum(jnp.where(onehots[k], prior, 0.0), axis=0, keepdims=True)
        rank_ref[k:k + 1, :] = rk.astype(I32)
        idx_ref[k:k + 1, :] = idxs[k]
    carry_ref[...] = carry_ref[...] + jnp.sum(cnt, axis=1, keepdims=True)
    cnt_ref[...] = carry_ref[...]

    lrow = lax.broadcasted_iota(I32, (LANES, tm), 0)
    g_t = jnp.zeros((LANES, tm), F32)
    for k in range(TOP_K):
        g_t = jnp.where(lrow == k, gate_rows[k], g_t)
    gate_ref[...] = g_t.T


def _post_attn(x2, yf, yd, mods, ln1_g, ln2_g, b_merge, w_gl, w_brf, w_brd, w_o, w_r, b_r, s, tm):
    n, d = x2.shape
    nbb = s // tm
    bsz = n // s
    sc1, sh1, g1, sc2, sh2 = [m.reshape(bsz, 1, d) for m in mods]
    row_spec = lambda w: pl.BlockSpec((tm, w), lambda i: (i, 0))
    mod_spec = pl.BlockSpec((1, 1, d), lambda i: (i // nbb, 0, 0))
    full = lambda shape: pl.BlockSpec(shape, lambda i: (0,) * len(shape))
    tok_spec = pl.BlockSpec((TOP_K, tm), lambda i: (0, i))
    return pl.pallas_call(
        functools.partial(_post_attn_kernel, tm=tm),
        grid=(n // tm,),
        in_specs=[row_spec(d), row_spec(FOX_W), row_spec(DIFF_W),
                  mod_spec, mod_spec, mod_spec, mod_spec, mod_spec,
                  full((1, d)), full((1, d)), full(w_gl.shape), full((1, 2 * d)),
                  full(w_brf.shape), full(w_brd.shape), full(w_o.shape),
                  full((N_EXPERTS, d)), full((N_EXPERTS, 1))],
        out_specs=[row_spec(d), pl.BlockSpec(_tiled_shape(tm, d), lambda i: (i, 0)),
                   tok_spec, tok_spec, row_spec(LANES), full((N_EXPERTS, LANES))],
        out_shape=[jax.ShapeDtypeStruct((n, d), F32), jax.ShapeDtypeStruct(_tiled_shape(n, d), F32),
                   jax.ShapeDtypeStruct((TOP_K, n), I32), jax.ShapeDtypeStruct((TOP_K, n), I32),
                   jax.ShapeDtypeStruct((n, LANES), F32),
                   jax.ShapeDtypeStruct((N_EXPERTS, LANES), F32)],
        scratch_shapes=[pltpu.VMEM((N_EXPERTS, LANES), F32)],
        compiler_params=_params("arbitrary"),
        name="post_attn",
    )(x2, yf, yd, sc1, sh1, g1, sc2, sh2, ln1_g.reshape(1, d), ln2_g.reshape(1, d),
      w_gl.astype(BF16), b_merge.reshape(1, 2 * d), w_brf.astype(BF16), w_brd.astype(BF16),
      w_o.astype(BF16), w_r.T, b_r.reshape(N_EXPERTS, 1))


def _route_kernel(cnt_ref, idx_ref, rank_ref, dest_ref, be_ref, pend_ref, *, blk, nblk_lanes):
    shift = blk.bit_length() - 1
    idx = idx_ref[...]
    start_of = jnp.zeros(idx.shape, I32)
    blk_start = lax.broadcasted_iota(I32, (1, nblk_lanes), 1) * blk
    lane = lax.broadcasted_iota(I32, (1, LANES), 1)
    blk_e = jnp.zeros((1, nblk_lanes), I32)
    pends = jnp.zeros((1, LANES), I32)
    pend = jnp.int32(0)
    for e in range(N_EXPERTS):
        padded = ((cnt_ref[e] + (blk - 1)) >> shift) << shift
        start_of = jnp.where(idx == e, pend, start_of)
        pend = pend + padded
        blk_e = blk_e + jnp.where(blk_start >= pend, 1, 0)
        pends = jnp.where(lane == e, pend, pends)
    dest_ref[...] = start_of + rank_ref[...]
    be_ref[...] = jnp.minimum(blk_e, N_EXPERTS - 1)
    pend_ref[...] = pends


def _route(counts, idx, rank, blk):
    n = idx.shape[1]
    nblk = (n * TOP_K + blk - 1) // blk + N_EXPERTS
    nblk_lanes = -(-nblk // LANES) * LANES
    dest, blk_e, pends = pl.pallas_call(
        functools.partial(_route_kernel, blk=blk, nblk_lanes=nblk_lanes),
        in_specs=[pl.BlockSpec(memory_space=pltpu.SMEM),
                  pl.BlockSpec(memory_space=pltpu.VMEM),
                  pl.BlockSpec(memory_space=pltpu.VMEM)],
        out_specs=[pl.BlockSpec(memory_space=pltpu.VMEM)] * 3,
        out_shape=[jax.ShapeDtypeStruct((TOP_K, n), I32),
                   jax.ShapeDtypeStruct((1, nblk_lanes), I32),
                   jax.ShapeDtypeStruct((1, LANES), I32)],
        compiler_params=pltpu.CompilerParams(vmem_limit_bytes=VMEM_LIMIT),
        name="route",
    )(counts, idx, rank)
    return dest, blk_e[0, :nblk], pends[0, :N_EXPERTS], nblk


def _dispatch_kernel(pend_ref, dest_ref, h_ref, xs_ref, zero_ref, sem, *, tm, blk, nblk):
    shift = blk.bit_length() - 1

    @pl.when(pl.program_id(0) == 0)
    def _():
        zero_ref[...] = jnp.zeros(zero_ref.shape, F32)

        def zero_block(b):
            lines = blk * ROW_TILE
            return pltpu.make_async_copy(
                zero_ref, xs_ref.at[pl.ds(pl.multiple_of(b * lines, lines), lines)], sem)

        def tails(fn):
            for e in range(N_EXPERTS):
                prev = pend_ref[e - 1] if e else 0

                @pl.when(pend_ref[e] > prev)
                def _():
                    fn(zero_block((pend_ref[e] >> shift) - 1))

        def rest(fn):
            def body(b, carry):
                fn(zero_block(b))
                return carry
            lax.fori_loop(pend_ref[N_EXPERTS - 1] >> shift, nblk, body, 0)

        tails(lambda cp: cp.start())
        rest(lambda cp: cp.start())
        tails(lambda cp: cp.wait())
        rest(lambda cp: cp.wait())

    def row_copy(t, k):
        return pltpu.make_async_copy(_row_tile(h_ref, t), _row_tile(xs_ref, dest_ref[k, t]), sem)

    def start(t, carry):
        for k in range(TOP_K):
            row_copy(t, k).start(priority=k % 2)
        return carry

    def wait(t, carry):
        for k in range(TOP_K):
            row_copy(t, k).wait()
        return carry

    lax.fori_loop(0, tm, start, 0, unroll=ROW_DMA_UNROLL)
    lax.fori_loop(0, tm, wait, 0, unroll=ROW_DMA_UNROLL)


def _dispatch(pends, dest, h2, tm, blk, nblk):
    n = h2.shape[0] // ROW_TILE
    d = ROW_TILE * LANES
    return pl.pallas_call(
        functools.partial(_dispatch_kernel, tm=tm, blk=blk, nblk=nblk),
        grid=(n // tm,),
        in_specs=[pl.BlockSpec(memory_space=pltpu.SMEM),
                  pl.BlockSpec((TOP_K, tm), lambda i: (0, i), memory_space=pltpu.SMEM),
                  pl.BlockSpec(_tiled_shape(tm, d), lambda i: (i, 0))],
        out_specs=pl.BlockSpec(memory_space=pl.ANY),
        out_shape=jax.ShapeDtypeStruct(_tiled_shape(nblk * blk, d), F32),
        scratch_shapes=[pltpu.VMEM(_tiled_shape(blk, d), F32), pltpu.SemaphoreType.DMA(())],
        compiler_params=_params("arbitrary"),
        name="dispatch",
    )(pends, dest, h2)


ROW_DMA_UNROLL = 4
UP_CHUNK = 2048
PICK_GROUP = 512


def _expert_kernel(be_ref, pend_ref, x_ref, wu_hbm, bu_ref, wd_hbm, bd_ref, o_ref,
                   wu_f32, wd_f32, wu_sc, wd_sc, sems, nseen_ref, *, blk):
    shift = blk.bit_length() - 1
    b = pl.program_id(0)
    n_live = pend_ref[N_EXPERTS - 1] >> shift
    live = b < n_live
    expert = be_ref[b]
    new_expert = (b == 0) | (expert != be_ref[jnp.maximum(b - 1, 0)])

    def fetch(e, slot):
        return (pltpu.make_async_copy(wu_hbm.at[e], wu_f32.at[slot], sems.at[0, slot]),
                pltpu.make_async_copy(wd_hbm.at[e], wd_f32.at[slot], sems.at[1, slot]))

    @pl.when(b == 0)
    def _():
        nseen_ref[0] = 0
        for cp in fetch(expert, 0):
            cp.start()

    @pl.when(live & new_expert)
    def _():
        slot = nseen_ref[0] & 1
        nxt = pend_ref[expert] >> shift

        @pl.when(nxt < n_live)
        def _():
            for cp in fetch(be_ref[nxt], 1 - slot):
                cp.start()

        for cp in fetch(expert, slot):
            cp.wait()
        wu_sc[...] = wu_f32[slot].astype(BF16)
        wd_sc[...] = wd_f32[slot].astype(BF16)
        nseen_ref[0] = nseen_ref[0] + 1

    @pl.when(live)
    def _():
        x = _load_rows_tiled(x_ref).astype(BF16)
        r = lax.broadcasted_iota(I32, (PICK_GROUP, PICK_GROUP // 2), 0)
        c = lax.broadcasted_iota(I32, (PICK_GROUP, PICK_GROUP // 2), 1)
        even = jnp.where(r == 2 * c, 1.0, 0.0).astype(BF16)
        acts = []
        for n in range(wu_sc.shape[1] // UP_CHUNK):
            cols = slice(n * UP_CHUNK, (n + 1) * UP_CHUNK)
            u = _dot(x, wu_sc[:, cols]) + bu_ref[0, :, cols]
            u_glu = jnp.minimum(u, SWIGLU_LIMIT)
            u_lin = jnp.clip(u, -SWIGLU_LIMIT, SWIGLU_LIMIT) + 1.0
            z = (u_glu * (1.0 / (1.0 + jnp.exp(-SWIGLU_ALPHA * u_glu)))
                 * pltpu.roll(u_lin, UP_CHUNK - 1, 1)).astype(BF16)
            for g in range(UP_CHUNK // PICK_GROUP):
                acts.append(_dot(z[:, g * PICK_GROUP:(g + 1) * PICK_GROUP], even).astype(BF16))
        _store_rows_tiled(o_ref, _dot(jnp.concatenate(acts, axis=1), wd_sc[...]) + bd_ref[0])

    @pl.when(jnp.logical_not(live))
    def _():
        o_ref[...] = jnp.zeros(o_ref.shape, F32)


def _experts(blk_e, pends, xs, w_up, b_up, w_down, b_down, blk):
    p = xs.shape[0] // ROW_TILE
    ne, d, f2 = w_up.shape
    f = w_down.shape[1]
    shift = blk.bit_length() - 1

    def live(b, be, pe):
        return jnp.minimum(b, (pe[N_EXPERTS - 1] >> shift) - 1)

    x_spec = pl.BlockSpec(_tiled_shape(blk, d), lambda b, be, pe: (live(b, be, pe), 0))
    w_spec = lambda r, c: pl.BlockSpec((1, r, c), lambda b, be, pe: (be[live(b, be, pe)], 0, 0))
    return pl.pallas_call(
        functools.partial(_expert_kernel, blk=blk),
        grid_spec=pltpu.PrefetchScalarGridSpec(
            num_scalar_prefetch=2,
            grid=(p // blk,),
            in_specs=[x_spec, pl.BlockSpec(memory_space=pl.ANY), w_spec(1, f2),
                      pl.BlockSpec(memory_space=pl.ANY), w_spec(1, d)],
            out_specs=pl.BlockSpec(_tiled_shape(blk, d), lambda b, be, pe: (b, 0)),
            scratch_shapes=[pltpu.VMEM((2, d, f2), F32), pltpu.VMEM((2, f, d), F32),
                            pltpu.VMEM((d, f2), BF16), pltpu.VMEM((f, d), BF16),
                            pltpu.SemaphoreType.DMA((2, 2)), pltpu.SMEM((1,), I32)]),
        out_shape=jax.ShapeDtypeStruct(_tiled_shape(p, d), F32),
        compiler_params=_params("arbitrary"),
        name="experts",
    )(blk_e, pends, xs, w_up, b_up.reshape(ne, 1, f2), w_down, b_down.reshape(ne, 1, d))


def _combine_kernel(dest_ref, dest_next_ref, x1_ref, gate_ref, g2_ref, fg_ref, ys_ref, o_ref,
                    ybuf_ref, sems, *, tm):
    i = pl.program_id(0)
    slot = i & 1

    def gather(idx_ref, buf_slot, fn):
        def body(t, carry):
            for k in range(TOP_K):
                fn(pltpu.make_async_copy(_row_tile(ys_ref, idx_ref[k, t]),
                                         _row_tile(ybuf_ref.at[buf_slot, k], t), sems.at[buf_slot]), k)
            return carry
        lax.fori_loop(0, tm, body, 0, unroll=ROW_DMA_UNROLL)

    start = lambda cp, k: cp.start(priority=k % 2)
    pl.when(i == 0)(lambda: gather(dest_ref, 0, start))
    pl.when(i + 1 < pl.num_programs(0))(lambda: gather(dest_next_ref, 1 - slot, start))
    gather(dest_ref, slot, lambda cp, k: cp.wait())

    g = gate_ref[...]
    moe = g[:, 0:1] * _load_rows_tiled(ybuf_ref.at[slot, 0])
    for k in range(1, TOP_K):
        moe = moe + g[:, k:k + 1] * _load_rows_tiled(ybuf_ref.at[slot, k])
    xo = x1_ref[...] + g2_ref[0] * moe
    o_ref[...] = xo * lax.rsqrt(jnp.mean(xo * xo, axis=-1, keepdims=True) + NORM_EPS) * fg_ref[...]


def _combine(dest, x1, gates, g2, final_g, ys, s, tm):
    n, d = x1.shape
    nbb = s // tm
    bsz = n // s
    last = n // tm - 1
    return pl.pallas_call(
        functools.partial(_combine_kernel, tm=tm),
        grid=(n // tm,),
        in_specs=[pl.BlockSpec((TOP_K, tm), lambda i: (0, i), memory_space=pltpu.SMEM),
                  pl.BlockSpec((TOP_K, tm), lambda i: (0, jnp.minimum(i + 1, last)),
                               memory_space=pltpu.SMEM),
                  pl.BlockSpec((tm, d), lambda i: (i, 0)),
                  pl.BlockSpec((tm, LANES), lambda i: (i, 0)),
                  pl.BlockSpec((1, 1, d), lambda i: (i // nbb, 0, 0)),
                  pl.BlockSpec((1, d), lambda i: (0, 0)),
                  pl.BlockSpec(memory_space=pl.ANY)],
        out_specs=pl.BlockSpec((tm, d), lambda i: (i, 0)),
        out_shape=jax.ShapeDtypeStruct((n, d), F32),
        scratch_shapes=[pltpu.VMEM((2, TOP_K, *_tiled_shape(tm, d)), F32),
                        pltpu.SemaphoreType.DMA((2,))],
        compiler_params=_params("arbitrary"),
        name="combine",
    )(dest, dest, x1, gates, g2.reshape(bsz, 1, d), final_g.reshape(1, d), ys)


def kernel(x, c, w_ada, b_ada, ln1_g, w_in, b_fgate, b_merge, lam_q1, lam_k1, lam_q2, lam_k2,
           diff_norm_g, t5_bias, w_br_fox, w_br_diff, w_o, ln2_g, w_router, b_router,
           w_up, b_up, w_down, b_down, final_g):
    bsz, s, d = x.shape
    n = bsz * s
    t = min(512, s)
    tr = min(256, s)
    assert s % t == 0 and w_ada.shape[0] == 1

    mod = _adaln_mod(c, w_ada[0], b_ada[0])
    sh1, sc1, g1, sh2, sc2, g2 = jnp.split(mod, 6, axis=-1)

    qtf, fqt, kpf, vtf, qtd, kd, vtd = _in_proj(x, sc1, sh1, ln1_g[0], w_in[0], b_fgate[0], t)
    y_fox = _fox_attention(qtf, fqt, kpf, vtf, t)
    lam_vecs = jnp.stack([lam_q1[0], lam_k1[0], lam_q2[0], lam_k2[0]])
    y_diff = _diff_attention(lam_vecs, qtd, kd, vtd, _bias_tiles(t5_bias, t), diff_norm_g[0], t)

    w_gl = w_in[0][:, w_in.shape[2] - 2 * d:]
    x1, h2, idx, rank, gates, counts = _post_attn(
        x.reshape(n, d), y_fox.reshape(n, FOX_W), y_diff.reshape(n, DIFF_W),
        (sc1, sh1, g1, sc2, sh2), ln1_g[0], ln2_g[0], b_merge[0], w_gl,
        w_br_fox[0], w_br_diff[0], w_o[0], w_router[0], b_router[0], s, t)

    dest, blk_e, pends, nblk = _route(counts[:, 0].astype(I32), idx, rank, EXPERT_BLOCK)
    xs = _dispatch(pends, dest, h2, tr, EXPERT_BLOCK, nblk)
    ys = _experts(blk_e, pends, xs, w_up[0], b_up[0], w_down[0], b_down[0], EXPERT_BLOCK)
    out = _combine(dest, x1, gates, g2, final_g, ys, s, tr)
    return out.reshape(bsz, s, d)
```

```python
import functools
import math

import numpy as np
import jax
import jax.numpy as jnp
from jax import lax
from jax.experimental import pallas as pl
from jax.experimental.pallas import tpu as pltpu

F32 = jnp.float32
BF16 = jnp.bfloat16
I32 = jnp.int32

HEAD_DIM = 64
FOX_HEADS = 8
DIFF_HEADS = 4
FOX_W = FOX_HEADS * HEAD_DIM
DIFF_W = DIFF_HEADS * 2 * HEAD_DIM
T5_BUCKETS = 32
T5_MAX_DIST = 128
N_EXPERTS = 32
TOP_K = 4
EXPERT_BLOCK = 512
SWIGLU_ALPHA = 1.702
SWIGLU_LIMIT = 7.0
NORM_EPS = 1e-6
SUBLN_EPS = 1e-5
LAMBDA_INIT = 0.8 - 0.6 * math.exp(-0.3 * 0)

LANES = 128
NEG = -1e30
LOG2E = math.log2(math.e)
ONES_ROWS = 16
VMEM_LIMIT = 56 * 1024 * 1024

NT_DIMS = (((1,), (1,)), ((), ()))


def _dot(a, b):
    return jnp.dot(a, b, preferred_element_type=F32)


def _dot_nt(a, b):
    return lax.dot_general(a, b, NT_DIMS, preferred_element_type=F32)


def _split2(v):
    hi = v.astype(BF16)
    return hi, (v - hi.astype(F32)).astype(BF16)


def _dot_split(a, b, dot=_dot):
    a_hi, a_lo = _split2(a)
    b_hi, b_lo = _split2(b)
    return dot(a_hi, b_hi) + (dot(a_hi, b_lo) + dot(a_lo, b_hi))


def _params(*sem):
    return pltpu.CompilerParams(dimension_semantics=sem, vmem_limit_bytes=VMEM_LIMIT)


def _modulated_rmsnorm(x, g, scale, shift):
    y = x * lax.rsqrt(jnp.mean(x * x, axis=-1, keepdims=True) + NORM_EPS)
    return (y * g) * (1.0 + scale) + shift


ROW_TILE = 8
ROW_PARTS = 1


def _tiled_shape(n, d):
    assert d == ROW_TILE * LANES
    return (n * ROW_TILE, LANES)


def _row_tile(ref, r):
    return ref.at[pl.ds(pl.multiple_of(r * ROW_TILE, ROW_TILE), ROW_TILE)]


def _store_rows_tiled(ref, rows):
    n = rows.shape[0]
    for a in range(ROW_TILE):
        ref[pl.ds(a, n, stride=ROW_TILE), :] = rows[:, a * LANES:(a + 1) * LANES]


def _load_rows_tiled(ref):
    n = ref.shape[0] // ROW_TILE
    return jnp.concatenate([ref[pl.ds(a, n, stride=ROW_TILE), :] for a in range(ROW_TILE)], axis=1)


def _split3(v):
    hi = v.astype(BF16)
    r1 = v - hi.astype(F32)
    mid = r1.astype(BF16)
    lo = (r1 - mid.astype(F32)).astype(BF16)
    return hi, mid, lo


def _adaln_kernel(c_ref, w_ref, b_ref, o_ref):
    c = c_ref[...]
    s = c / (1.0 + jnp.exp(-c))
    o_ref[...] = _dot_split(s, w_ref[...]) + b_ref[...]


def _adaln_mod(c, w, b):
    bsz, d = c.shape
    n = w.shape[1]
    rows = 8
    tn = 1536
    c8 = jnp.zeros((rows, d), F32).at[:bsz].set(c)
    out = pl.pallas_call(
        _adaln_kernel,
        grid=(n // tn,),
        in_specs=[pl.BlockSpec((rows, d), lambda j: (0, 0)),
                  pl.BlockSpec((d, tn), lambda j: (0, j)),
                  pl.BlockSpec((1, tn), lambda j: (0, j))],
        out_specs=pl.BlockSpec((rows, tn), lambda j: (0, j)),
        out_shape=jax.ShapeDtypeStruct((rows, n), F32),
        compiler_params=_params("arbitrary"),
        name="adaln_mod",
    )(c8, w, b.reshape(1, n))
    return out[:bsz]


def _bias_tile_kernel(tb_ref, o_ref, *, t):
    h = pl.program_id(0)
    key = lax.broadcasted_iota(I32, (t, t), 0)
    qry = lax.broadcasted_iota(I32, (t, t), 1)
    far = tb_ref[h, T5_BUCKETS - 1]
    max_exact = T5_BUCKETS // 2
    for which in range(2):
        rel = qry - key + which * t
        n = jnp.maximum(rel, 0)
        nf = jnp.maximum(n, max_exact).astype(F32)
        large = max_exact + (jnp.log(nf / max_exact) / math.log(T5_MAX_DIST / max_exact)
                             * (T5_BUCKETS - max_exact)).astype(I32)
        large = jnp.minimum(large, T5_BUCKETS - 1)
        bucket = jnp.where(n < max_exact, n, large)
        bias = jnp.zeros((t, t), F32)
        for b in range(T5_BUCKETS):
            bias = jnp.where(bucket == b, tb_ref[h, b] - far, bias)
        o_ref[0, which] = jnp.where(rel >= 0, bias * LOG2E, NEG)


def _bias_tiles(t5_bias, t):
    tb = t5_bias.T.astype(F32)
    return pl.pallas_call(
        functools.partial(_bias_tile_kernel, t=t),
        grid=(DIFF_HEADS,),
        in_specs=[pl.BlockSpec(memory_space=pltpu.SMEM)],
        out_specs=pl.BlockSpec((1, 2, t, t), lambda h: (h, 0, 0, 0)),
        out_shape=jax.ShapeDtypeStruct((DIFF_HEADS, 2, t, t), F32),
        compiler_params=_params("arbitrary"),
        name="bias_tiles",
    )(tb)


def _fparts_perm():
    npair = FOX_HEADS // 2
    pk = np.zeros((3 * LANES, npair * LANES), np.float32)
    pq = np.zeros((npair * 32, 3 * LANES), np.float32)
    k_ones = np.zeros((1, npair * LANES), np.float32)
    q_ones = np.zeros((npair * 32, 1), np.float32)
    for p in range(npair):
        for hh in range(2):
            head = 2 * p + hh
            for part in range(3):
                pk[part * LANES + head, p * LANES + hh * 16 + 3 + part] = -1.0
                k_ones[0, p * LANES + hh * 16 + part] = 1.0
                pq[p * 32 + hh * 16 + part, part * LANES + head] = 1.0
                q_ones[p * 32 + hh * 16 + 3 + part, 0] = 1.0
    return pk, pq, k_ones, q_ones


def _in_proj_kernel(x_ref, sc_ref, sh_ref, g_ref, wt_ref, wn_ref, wfl_ref, bf_ref,
                    pk_ref, pq_ref, kone_ref, qone_ref,
                    qtf_ref, fqt_ref, kpf_ref, vtf_ref, qtd_ref, kd_ref, vtd_ref,
                    carry_ref, *, tm):
    i = pl.program_id(1)

    @pl.when(i == 0)
    def _():
        carry_ref[...] = jnp.zeros_like(carry_ref)

    tp = tm // ROW_PARTS
    r = lax.broadcasted_iota(I32, (tp, tp), 0)
    c = lax.broadcasted_iota(I32, (tp, tp), 1)
    tri = jnp.where(c <= r, 1.0, 0.0).astype(BF16)
    wfl_hi, wfl_lo = _split2(wfl_ref[...])
    ones_rows = jnp.where(lax.broadcasted_iota(I32, (ONES_ROWS, tp), 0) == 0, 1.0, 0.0).astype(BF16)
    scale = HEAD_DIM ** -0.5 * LOG2E
    dv = 2 * HEAD_DIM
    carry = carry_ref[...]

    for part in range(ROW_PARTS):
        rows = slice(part * tp, (part + 1) * tp)
        h = _modulated_rmsnorm(x_ref[0, rows], g_ref[...], sc_ref[0], sh_ref[0])
        hb = h.astype(BF16)

        h_lo = (h - hb.astype(F32)).astype(BF16)
        fl = _dot(hb, wfl_hi) + (_dot(hb, wfl_lo) + _dot(h_lo, wfl_hi)) + bf_ref[...]
        logf = jnp.minimum(fl, 0.0) - jnp.log(1.0 + jnp.exp(-jnp.abs(fl)))
        psum = _dot(tri, jnp.concatenate(_split3(logf), axis=1))
        fsum = (psum[:, 0:LANES] + (psum[:, LANES:2 * LANES] + psum[:, 2 * LANES:3 * LANES])
                + carry)
        carry = fsum[tp - 1:tp, :]
        parts = jnp.concatenate(_split3(fsum * LOG2E), axis=1)

        yt = _dot_nt(wt_ref[...], hb)
        yn = _dot(hb, wn_ref[...])

        qtf_ref[0, 0, :, rows] = (yt[0:FOX_W] * scale).astype(BF16)
        vt = yt[FOX_W:2 * FOX_W].astype(BF16)
        for hd in range(FOX_HEADS):
            vtf_ref[0, hd, 0, 0:HEAD_DIM, rows] = vt[hd * HEAD_DIM:(hd + 1) * HEAD_DIM]
            vtf_ref[0, hd, 0, HEAD_DIM:HEAD_DIM + ONES_ROWS, rows] = ones_rows
        qtd_ref[0, 0, :, rows] = (yt[2 * FOX_W:2 * FOX_W + DIFF_W] * scale).astype(BF16)
        vtd = yt[2 * FOX_W + DIFF_W:2 * FOX_W + 2 * DIFF_W].astype(BF16)
        for hd in range(DIFF_HEADS):
            vtd_ref[0, hd, 0, 0:dv, rows] = vtd[hd * dv:(hd + 1) * dv]
            vtd_ref[0, hd, 0, dv:dv + ONES_ROWS, rows] = ones_rows

        fqt_ref[0, 0, :, rows] = (_dot_nt(pq_ref[...], parts) + qone_ref[...]).astype(BF16)
        kext = (_dot(parts, pk_ref[...]) + kone_ref[...]).astype(BF16)
        kf = yn[:, 0:FOX_W].astype(BF16)
        for p in range(FOX_HEADS // 2):
            kpf_ref[0, rows, 2 * p * LANES:(2 * p + 1) * LANES] = kf[:, p * LANES:(p + 1) * LANES]
            kpf_ref[0, rows, (2 * p + 1) * LANES:(2 * p + 2) * LANES] = (
                kext[:, p * LANES:(p + 1) * LANES])
        kd_ref[0, rows] = yn[:, FOX_W:FOX_W + DIFF_W].astype(BF16)
    carry_ref[...] = carry


def _in_proj(x, sc1, sh1, ln_g, w_in, b_fgate, t):
    bsz, s, d = x.shape
    nb = s // t
    cuts = np.cumsum([FOX_W, FOX_W, FOX_W, FOX_HEADS, DIFF_W, DIFF_W, DIFF_W])
    w_fq, w_fk, w_fv, w_fl, w_dq, w_dk, w_dv = [
        w_in[:, a:b] for a, b in zip([0, *cuts[:-1]], cuts)]
    wt = jnp.concatenate([w_fq, w_fv, w_dq, w_dv], axis=1).T.astype(BF16)
    wn = jnp.concatenate([w_fk, w_dk], axis=1).astype(BF16)
    wfl = jnp.zeros((d, LANES), F32).at[:, :FOX_HEADS].set(w_fl)
    bfl = jnp.zeros((1, LANES), F32).at[0, :FOX_HEADS].set(b_fgate)
    pk, pq, k_ones, q_ones = _fparts_perm()
    npair = FOX_HEADS // 2

    full = lambda shape: pl.BlockSpec(shape, lambda b, i: (0,) * len(shape))
    outs = pl.pallas_call(
        functools.partial(_in_proj_kernel, tm=t),
        grid=(bsz, nb),
        in_specs=[pl.BlockSpec((1, t, d), lambda b, i: (b, i, 0)),
                  pl.BlockSpec((1, 1, d), lambda b, i: (b, 0, 0)),
                  pl.BlockSpec((1, 1, d), lambda b, i: (b, 0, 0)),
                  full((1, d)), full(wt.shape), full(wn.shape), full(wfl.shape), full(bfl.shape),
                  full(pk.shape), full(pq.shape), full(k_ones.shape), full(q_ones.shape)],
        out_specs=[pl.BlockSpec((1, 1, FOX_W, t), lambda b, i: (b, i, 0, 0)),
                   pl.BlockSpec((1, 1, npair * 32, t), lambda b, i: (b, i, 0, 0)),
                   pl.BlockSpec((1, t, npair * 2 * LANES), lambda b, i: (b, i, 0)),
                   pl.BlockSpec((1, FOX_HEADS, 1, HEAD_DIM + ONES_ROWS, t),
                                lambda b, i: (b, 0, i, 0, 0)),
                   pl.BlockSpec((1, 1, DIFF_W, t), lambda b, i: (b, i, 0, 0)),
                   pl.BlockSpec((1, t, DIFF_W), lambda b, i: (b, i, 0)),
                   pl.BlockSpec((1, DIFF_HEADS, 1, 2 * HEAD_DIM + ONES_ROWS, t),
                                lambda b, i: (b, 0, i, 0, 0))],
        out_shape=[jax.ShapeDtypeStruct((bsz, nb, FOX_W, t), BF16),
                   jax.ShapeDtypeStruct((bsz, nb, npair * 32, t), BF16),
                   jax.ShapeDtypeStruct((bsz, s, npair * 2 * LANES), BF16),
                   jax.ShapeDtypeStruct((bsz, FOX_HEADS, nb, HEAD_DIM + ONES_ROWS, t), BF16),
                   jax.ShapeDtypeStruct((bsz, nb, DIFF_W, t), BF16),
                   jax.ShapeDtypeStruct((bsz, s, DIFF_W), BF16),
                   jax.ShapeDtypeStruct((bsz, DIFF_HEADS, nb, 2 * HEAD_DIM + ONES_ROWS, t), BF16)],
        scratch_shapes=[pltpu.VMEM((1, LANES), F32)],
        compiler_params=_params("arbitrary", "arbitrary"),
        name="in_proj",
    )(x, sc1.reshape(bsz, 1, d), sh1.reshape(bsz, 1, d), ln_g.reshape(1, d), wt, wn, wfl, bfl,
      jnp.asarray(pk, BF16), jnp.asarray(pq, BF16), jnp.asarray(k_ones), jnp.asarray(q_ones))
    return outs


SCORE_EXTRA_ROWS = 8


def _store_scores(s_ref, s, t):
    s_ref[0:t, :] = s
    s_ref[t:t + 1, :] = jnp.max(s, axis=0, keepdims=True)


def _softmax_block(s, s_max, vt, m_ref, acc_ref, slot):
    m_old = m_ref[slot]
    m_new = jnp.maximum(m_old, s_max)
    alpha = jnp.exp2(m_old - m_new)
    p = jnp.exp2(s - m_new)
    acc_ref[slot] = alpha * acc_ref[slot] + _dot(vt, p.astype(BF16))
    m_ref[slot] = m_new


def _init_softmax_state(m_ref, acc_ref):
    m_ref[...] = jnp.full(m_ref.shape, NEG, F32)
    acc_ref[...] = jnp.zeros(acc_ref.shape, F32)


SCAN_BLOCKS_PER_TRIP = 4


def _attend_all_query_blocks(nb, n_tail, prepare_q, qk_into, process, begin, finish, bufs):
    ring, first_buf = bufs[:2], bufs[2]
    prepare_q(0, 0)
    qk_into(first_buf, 0, 0)

    def query_block(i, carry):
        slot = i & 1

        def start_next_query_block():
            prepare_q(jnp.minimum(i + 1, nb - 1), 1 - slot)
            qk_into(first_buf, 1 - slot, 0)

        def run(items, ends):
            for n, (buf, j, kind) in enumerate(items):
                last = n + 1 == len(items)
                if not last:
                    qk_into(items[n + 1][0], slot, items[n + 1][1])
                elif ends and buf is not first_buf:
                    start_next_query_block()
                process(buf, i, j, kind)
                if last and ends and buf is first_buf:
                    start_next_query_block()

        begin()
        for small in range(n_tail):
            items = [(first_buf, 0, small)] + [(ring[n % 2], n + 1, small - 1 - n) for n in range(small)]
            pl.when(i == small)(functools.partial(run, items, True))

        @pl.when(i >= n_tail)
        def _():
            n_plain = i - n_tail
            n_trips = n_plain // SCAN_BLOCKS_PER_TRIP
            rem = n_plain % SCAN_BLOCKS_PER_TRIP
            qk_into(ring[0], slot, 1)
            process(first_buf, i, 0, None)

            def trip(n, carry):
                j = 1 + SCAN_BLOCKS_PER_TRIP * n
                for m in range(SCAN_BLOCKS_PER_TRIP):
                    qk_into(ring[(m + 1) % 2], slot, j + m + 1)
                    process(ring[m % 2], i, j + m, None)
                return carry

            lax.fori_loop(0, n_trips, trip, 0)
            first = 1 + SCAN_BLOCKS_PER_TRIP * n_trips
            for r in range(SCAN_BLOCKS_PER_TRIP):
                kinds = [None] * r + list(range(n_tail - 1, -1, -1))
                items = [(ring[n % 2], first + n, kind) for n, kind in enumerate(kinds)]
                pl.when(rem == r)(functools.partial(run, items, True))

        finish(i)
        return carry

    lax.fori_loop(0, nb, query_block, 0)


def _fox_kernel(q_ref, fq_ref, k_ref, v_ref, o_ref, qbd_ref, sa_ref, sb_ref, sc_ref, m_ref, acc_ref,
                *, t, nb):
    d = HEAD_DIM
    qbd_ref[...] = jnp.zeros(qbd_ref.shape, BF16)

    def prepare_q(i, slot):
        qbd_ref[slot, 0:d, 0:t] = q_ref[0, i, 0:d, :]
        qbd_ref[slot, d:2 * d, t:2 * t] = q_ref[0, i, d:2 * d, :]
        qbd_ref[slot, 2 * d:2 * d + 16, 0:t] = fq_ref[0, i, 0:16, :]
        qbd_ref[slot, 2 * d + 16:2 * d + 32, t:2 * t] = fq_ref[0, i, 16:32, :]

    def qk_into(s_ref, slot, j):
        kblk = k_ref[0, pl.ds(pl.multiple_of(j * t, t), t), :]
        _store_scores(s_ref, _dot(kblk, qbd_ref[slot]), t)

    def process(s_ref, i, j, kind):
        for hh in range(2):
            sh = s_ref[0:t, hh * t:(hh + 1) * t]
            if kind == 0:
                key = lax.broadcasted_iota(I32, (t, t), 0)
                qry = lax.broadcasted_iota(I32, (t, t), 1)
                sh = jnp.where(key <= qry, sh, NEG)
                s_max = jnp.max(sh, axis=0, keepdims=True)
            else:
                s_max = s_ref[t:t + 1, hh * t:(hh + 1) * t]
            _softmax_block(sh, s_max, v_ref[0, hh, j], m_ref, acc_ref, hh)

    def finish(i):
        outs = [acc_ref[hh, 0:d] / acc_ref[hh, d:d + 1] for hh in range(2)]
        o_ref[0, pl.ds(pl.multiple_of(i * t, t), t), :] = (
            jnp.concatenate(outs, axis=0).T.astype(BF16))

    _attend_all_query_blocks(nb, 1, prepare_q, qk_into, process,
                             functools.partial(_init_softmax_state, m_ref, acc_ref), finish,
                             (sa_ref, sb_ref, sc_ref))


def _fox_attention(qt, fqt, kp, vt, t):
    bsz, nb = qt.shape[:2]
    s = nb * t
    npair = FOX_HEADS // 2
    return pl.pallas_call(
        functools.partial(_fox_kernel, t=t, nb=nb),
        grid=(bsz, npair),
        in_specs=[pl.BlockSpec((1, nb, 2 * HEAD_DIM, t), lambda b, p: (b, 0, p, 0)),
                  pl.BlockSpec((1, nb, 32, t), lambda b, p: (b, 0, p, 0)),
                  pl.BlockSpec((1, s, 2 * LANES), lambda b, p: (b, 0, p)),
                  pl.BlockSpec((1, 2, nb, HEAD_DIM + ONES_ROWS, t), lambda b, p: (b, p, 0, 0, 0))],
        out_specs=pl.BlockSpec((1, s, 2 * HEAD_DIM), lambda b, p: (b, 0, p)),
        out_shape=jax.ShapeDtypeStruct((bsz, s, FOX_W), BF16),
        scratch_shapes=[pltpu.VMEM((2, 2 * LANES, 2 * t), BF16)]
        + [pltpu.VMEM((t + SCORE_EXTRA_ROWS, 2 * t), F32)] * 3
        + [pltpu.VMEM((2, 1, t), F32), pltpu.VMEM((2, HEAD_DIM + ONES_ROWS, t), F32)],
        compiler_params=_params("arbitrary", "arbitrary"),
        name="fox_attn",
    )(qt, fqt, kp, vt)


def _diff_kernel(lam_ref, q_ref, k_ref, v_ref, bias_ref, g_ref, o_ref,
                 qbd_ref, sa_ref, sb_ref, sc_ref, m_ref, acc_ref, *, t, nb):
    d = HEAD_DIM
    dv = 2 * HEAD_DIM
    qbd_ref[...] = jnp.zeros(qbd_ref.shape, BF16)

    def prepare_q(i, slot):
        qbd_ref[slot, 0:d, 0:t] = q_ref[0, i, 0:d, :]
        qbd_ref[slot, d:2 * d, t:2 * t] = q_ref[0, i, d:2 * d, :]

    def qk_into(s_ref, slot, j):
        kblk = k_ref[0, pl.ds(pl.multiple_of(j * t, t), t), :]
        _store_scores(s_ref, _dot(kblk, qbd_ref[slot]), t)

    def process(s_ref, i, j, kind):
        vt = v_ref[0, 0, j]
        for hh in range(2):
            sh = s_ref[0:t, hh * t:(hh + 1) * t]
            if kind is not None:
                sh = sh + bias_ref[0, kind]
                s_max = jnp.max(sh, axis=0, keepdims=True)
            else:
                s_max = s_ref[t:t + 1, hh * t:(hh + 1) * t]
            _softmax_block(sh, s_max, vt, m_ref, acc_ref, hh)

    def finish(i):
        lam = (jnp.exp(jnp.sum(lam_ref[0:1, :] * lam_ref[1:2, :], axis=1, keepdims=True))
               - jnp.exp(jnp.sum(lam_ref[2:3, :] * lam_ref[3:4, :], axis=1, keepdims=True))
               + LAMBDA_INIT)
        out = (acc_ref[0, 0:dv] / acc_ref[0, dv:dv + 1]
               - lam * (acc_ref[1, 0:dv] / acc_ref[1, dv:dv + 1]))
        ms = jnp.mean(out * out, axis=0, keepdims=True)
        out = out * lax.rsqrt(ms + SUBLN_EPS) * g_ref[...] * (1.0 - LAMBDA_INIT)
        o_ref[0, pl.ds(pl.multiple_of(i * t, t), t), :] = out.T.astype(BF16)

    _attend_all_query_blocks(nb, 2, prepare_q, qk_into, process,
                             functools.partial(_init_softmax_state, m_ref, acc_ref), finish,
                             (sa_ref, sb_ref, sc_ref))


def _diff_attention(lam_vecs, qt, k, vt, bias_tiles, norm_g, t):
    bsz, nb = qt.shape[:2]
    s = nb * t
    dv = 2 * HEAD_DIM
    return pl.pallas_call(
        functools.partial(_diff_kernel, t=t, nb=nb),
        grid=(bsz, DIFF_HEADS),
        in_specs=[pl.BlockSpec((4, HEAD_DIM), lambda b, h: (0, 0)),
                  pl.BlockSpec((1, nb, dv, t), lambda b, h: (b, 0, h, 0)),
                  pl.BlockSpec((1, s, dv), lambda b, h: (b, 0, h)),
                  pl.BlockSpec((1, 1, nb, dv + ONES_ROWS, t), lambda b, h: (b, h, 0, 0, 0)),
                  pl.BlockSpec((1, 2, t, t), lambda b, h: (h, 0, 0, 0)),
                  pl.BlockSpec((dv, 1), lambda b, h: (0, 0))],
        out_specs=pl.BlockSpec((1, s, dv), lambda b, h: (b, 0, h)),
        out_shape=jax.ShapeDtypeStruct((bsz, s, DIFF_W), BF16),
        scratch_shapes=[pltpu.VMEM((2, dv, 2 * t), BF16)]
        + [pltpu.VMEM((t + SCORE_EXTRA_ROWS, 2 * t), F32)] * 3
        + [pltpu.VMEM((2, 1, t), F32), pltpu.VMEM((2, dv + ONES_ROWS, t), F32)],
        compiler_params=_params("arbitrary", "arbitrary"),
        name="diff_attn",
    )(lam_vecs, qt, k, vt, bias_tiles, norm_g.reshape(dv, 1))


def _post_attn_kernel(x_ref, yf_ref, yd_ref, sc1_ref, sh1_ref, g1_ref, sc2_ref, sh2_ref,
                      ln1_ref, ln2_ref, wgl_ref, bm_ref, wbf_ref, wbd_ref, wo_ref, wr_ref, br_ref,
                      x1_ref, h2_ref, idx_ref, rank_ref, gate_ref, cnt_ref, carry_ref, *, tm):
    i = pl.program_id(0)

    @pl.when(i == 0)
    def _():
        carry_ref[...] = jnp.zeros_like(carry_ref)

    d = x_ref.shape[1]
    tp = tm // ROW_PARTS
    logit_parts = []
    for part in range(ROW_PARTS):
        rows = slice(part * tp, (part + 1) * tp)
        x = x_ref[rows]
        h1 = _modulated_rmsnorm(x, ln1_ref[...], sc1_ref[0], sh1_ref[0]).astype(BF16)
        gl = _dot(h1, wgl_ref[...]) + bm_ref[...]
        gates = 1.0 / (1.0 + jnp.exp(-gl))
        merged = (gates[:, :d] * _dot(yf_ref[rows], wbf_ref[...])
                  + gates[:, d:] * _dot(yd_ref[rows], wbd_ref[...]))
        x1 = x + g1_ref[0] * _dot(merged.astype(BF16), wo_ref[...])
        x1_ref[rows] = x1
        h2 = _modulated_rmsnorm(x1, ln2_ref[...], sc2_ref[0], sh2_ref[0])
        _store_rows_tiled(h2_ref.at[pl.ds(part * tp * ROW_TILE, tp * ROW_TILE)], h2)
        logit_parts.append(_dot_split(wr_ref[...], h2, _dot_nt))
    logits = jnp.concatenate(logit_parts, axis=1) + br_ref[...]
    row = lax.broadcasted_iota(I32, (N_EXPERTS, tm), 0)
    cur = logits
    vals, idxs = [], []
    for _ in range(TOP_K):
        mx = jnp.max(cur, axis=0, keepdims=True)
        ik = jnp.min(jnp.where(cur == mx, row, N_EXPERTS), axis=0, keepdims=True)
        vals.append(mx)
        idxs.append(ik)
        cur = jnp.where(row == ik, -jnp.inf, cur)
    exps = [jnp.exp(v - vals[0]) for v in vals]
    denom = exps[0] + exps[1] + exps[2] + exps[3]
    gate_rows = [e / denom for e in exps]

    onehots = [row == ik for ik in idxs]
    cnt = jnp.zeros((N_EXPERTS, tm), F32)
    for oh in onehots:
        cnt = cnt + jnp.where(oh, 1.0, 0.0)
    r = lax.broadcasted_iota(I32, (tm, tm), 0)
    c = lax.broadcasted_iota(I32, (tm, tm), 1)
    before = jnp.where(r < c, 1.0, 0.0).astype(BF16)
    prior = carry_ref[:, 0:1] + _dot(cnt.astype(BF16), before)
    for k in range(TOP_K):
        rk = jnp.sum(jnp.where(onehots[k], prior, 0.0), axis=0, keepdims=True)
        rank_ref[k:k + 1, :] = rk.astype(I32)
        idx_ref[k:k + 1, :] = idxs[k]
    carry_ref[...] = carry_ref[...] + jnp.sum(cnt, axis=1, keepdims=True)
    cnt_ref[...] = carry_ref[...]

    lrow = lax.broadcasted_iota(I32, (LANES, tm), 0)
    g_t = jnp.zeros((LANES, tm), F32)
    for k in range(TOP_K):
        g_t = jnp.where(lrow == k, gate_rows[k], g_t)
    gate_ref[...] = g_t.T


def _post_attn(x2, yf, yd, mods, ln1_g, ln2_g, b_merge, w_gl, w_brf, w_brd, w_o, w_r, b_r, s, tm):
    n, d = x2.shape
    nbb = s // tm
    bsz = n // s
    sc1, sh1, g1, sc2, sh2 = [m.reshape(bsz, 1, d) for m in mods]
    row_spec = lambda w: pl.BlockSpec((tm, w), lambda i: (i, 0))
    mod_spec = pl.BlockSpec((1, 1, d), lambda i: (i // nbb, 0, 0))
    full = lambda shape: pl.BlockSpec(shape, lambda i: (0,) * len(shape))
    tok_spec = pl.BlockSpec((TOP_K, tm), lambda i: (0, i))
    return pl.pallas_call(
        functools.partial(_post_attn_kernel, tm=tm),
        grid=(n // tm,),
        in_specs=[row_spec(d), row_spec(FOX_W), row_spec(DIFF_W),
                  mod_spec, mod_spec, mod_spec, mod_spec, mod_spec,
                  full((1, d)), full((1, d)), full(w_gl.shape), full((1, 2 * d)),
                  full(w_brf.shape), full(w_brd.shape), full(w_o.shape),
                  full((N_EXPERTS, d)), full((N_EXPERTS, 1))],
        out_specs=[row_spec(d), pl.BlockSpec(_tiled_shape(tm, d), lambda i: (i, 0)),
                   tok_spec, tok_spec, row_spec(LANES), full((N_EXPERTS, LANES))],
        out_shape=[jax.ShapeDtypeStruct((n, d), F32), jax.ShapeDtypeStruct(_tiled_shape(n, d), F32),
                   jax.ShapeDtypeStruct((TOP_K, n), I32), jax.ShapeDtypeStruct((TOP_K, n), I32),
                   jax.ShapeDtypeStruct((n, LANES), F32),
                   jax.ShapeDtypeStruct((N_EXPERTS, LANES), F32)],
        scratch_shapes=[pltpu.VMEM((N_EXPERTS, LANES), F32)],
        compiler_params=_params("arbitrary"),
        name="post_attn",
    )(x2, yf, yd, sc1, sh1, g1, sc2, sh2, ln1_g.reshape(1, d), ln2_g.reshape(1, d),
      w_gl.astype(BF16), b_merge.reshape(1, 2 * d), w_brf.astype(BF16), w_brd.astype(BF16),
      w_o.astype(BF16), w_r.T, b_r.reshape(N_EXPERTS, 1))


def _route_kernel(cnt_ref, idx_ref, rank_ref, dest_ref, be_ref, pend_ref, *, blk, nblk_lanes):
    shift = blk.bit_length() - 1
    idx = idx_ref[...]
    start_of = jnp.zeros(idx.shape, I32)
    blk_start = lax.broadcasted_iota(I32, (1, nblk_lanes), 1) * blk
    lane = lax.broadcasted_iota(I32, (1, LANES), 1)
    blk_e = jnp.zeros((1, nblk_lanes), I32)
    pends = jnp.zeros((1, LANES), I32)
    pend = jnp.int32(0)
    for e in range(N_EXPERTS):
        padded = ((cnt_ref[e] + (blk - 1)) >> shift) << shift
        start_of = jnp.where(idx == e, pend, start_of)
        pend = pend + padded
        blk_e = blk_e + jnp.where(blk_start >= pend, 1, 0)
        pends = jnp.where(lane == e, pend, pends)
    dest_ref[...] = start_of + rank_ref[...]
    be_ref[...] = jnp.minimum(blk_e, N_EXPERTS - 1)
    pend_ref[...] = pends


def _route(counts, idx, rank, blk):
    n = idx.shape[1]
    nblk = (n * TOP_K + blk - 1) // blk + N_EXPERTS
    nblk_lanes = -(-nblk // LANES) * LANES
    dest, blk_e, pends = pl.pallas_call(
        functools.partial(_route_kernel, blk=blk, nblk_lanes=nblk_lanes),
        in_specs=[pl.BlockSpec(memory_space=pltpu.SMEM),
                  pl.BlockSpec(memory_space=pltpu.VMEM),
                  pl.BlockSpec(memory_space=pltpu.VMEM)],
        out_specs=[pl.BlockSpec(memory_space=pltpu.VMEM)] * 3,
        out_shape=[jax.ShapeDtypeStruct((TOP_K, n), I32),
                   jax.ShapeDtypeStruct((1, nblk_lanes), I32),
                   jax.ShapeDtypeStruct((1, LANES), I32)],
        compiler_params=pltpu.CompilerParams(vmem_limit_bytes=VMEM_LIMIT),
        name="route",
    )(counts, idx, rank)
    return dest, blk_e[0, :nblk], pends[0, :N_EXPERTS], nblk


def _dispatch_kernel(pend_ref, dest_ref, h_ref, xs_ref, zero_ref, sem, *, tm, blk, nblk):
    shift = blk.bit_length() - 1

    @pl.when(pl.program_id(0) == 0)
    def _():
        zero_ref[...] = jnp.zeros(zero_ref.shape, F32)

        def zero_block(b):
            lines = blk * ROW_TILE
            return pltpu.make_async_copy(
                zero_ref, xs_ref.at[pl.ds(pl.multiple_of(b * lines, lines), lines)], sem)

        def tails(fn):
            for e in range(N_EXPERTS):
                prev = pend_ref[e - 1] if e else 0

                @pl.when(pend_ref[e] > prev)
                def _():
                    fn(zero_block((pend_ref[e] >> shift) - 1))

        def rest(fn):
            def body(b, carry):
                fn(zero_block(b))
                return carry
            lax.fori_loop(pend_ref[N_EXPERTS - 1] >> shift, nblk, body, 0)

        tails(lambda cp: cp.start())
        rest(lambda cp: cp.start())
        tails(lambda cp: cp.wait())
        rest(lambda cp: cp.wait())

    def row_copy(t, k):
        return pltpu.make_async_copy(_row_tile(h_ref, t), _row_tile(xs_ref, dest_ref[k, t]), sem)

    def start(t, carry):
        for k in range(TOP_K):
            row_copy(t, k).start(priority=k % 2)
        return carry

    def wait(t, carry):
        for k in range(TOP_K):
            row_copy(t, k).wait()
        return carry

    lax.fori_loop(0, tm, start, 0, unroll=ROW_DMA_UNROLL)
    lax.fori_loop(0, tm, wait, 0, unroll=ROW_DMA_UNROLL)


def _dispatch(pends, dest, h2, tm, blk, nblk):
    n = h2.shape[0] // ROW_TILE
    d = ROW_TILE * LANES
    return pl.pallas_call(
        functools.partial(_dispatch_kernel, tm=tm, blk=blk, nblk=nblk),
        grid=(n // tm,),
        in_specs=[pl.BlockSpec(memory_space=pltpu.SMEM),
                  pl.BlockSpec((TOP_K, tm), lambda i: (0, i), memory_space=pltpu.SMEM),
                  pl.BlockSpec(_tiled_shape(tm, d), lambda i: (i, 0))],
        out_specs=pl.BlockSpec(memory_space=pl.ANY),
        out_shape=jax.ShapeDtypeStruct(_tiled_shape(nblk * blk, d), F32),
        scratch_shapes=[pltpu.VMEM(_tiled_shape(blk, d), F32), pltpu.SemaphoreType.DMA(())],
        compiler_params=_params("arbitrary"),
        name="dispatch",
    )(pends, dest, h2)


ROW_DMA_UNROLL = 4
UP_CHUNK = 2048
PICK_GROUP = 512


def _expert_kernel(be_ref, pend_ref, x_ref, wu_hbm, bu_ref, wd_hbm, bd_ref, o_ref,
                   wu_f32, wd_f32, wu_sc, wd_sc, sems, nseen_ref, *, blk):
    shift = blk.bit_length() - 1
    b = pl.program_id(0)
    n_live = pend_ref[N_EXPERTS - 1] >> shift
    live = b < n_live
    expert = be_ref[b]
    new_expert = (b == 0) | (expert != be_ref[jnp.maximum(b - 1, 0)])

    def fetch(e, slot):
        return (pltpu.make_async_copy(wu_hbm.at[e], wu_f32.at[slot], sems.at[0, slot]),
                pltpu.make_async_copy(wd_hbm.at[e], wd_f32.at[slot], sems.at[1, slot]))

    @pl.when(b == 0)
    def _():
        nseen_ref[0] = 0
        for cp in fetch(expert, 0):
            cp.start()

    @pl.when(live & new_expert)
    def _():
        slot = nseen_ref[0] & 1
        nxt = pend_ref[expert] >> shift

        @pl.when(nxt < n_live)
        def _():
            for cp in fetch(be_ref[nxt], 1 - slot):
                cp.start()

        for cp in fetch(expert, slot):
            cp.wait()
        wu_sc[...] = wu_f32[slot].astype(BF16)
        wd_sc[...] = wd_f32[slot].astype(BF16)
        nseen_ref[0] = nseen_ref[0] + 1

    @pl.when(live)
    def _():
        x = _load_rows_tiled(x_ref).astype(BF16)
        r = lax.broadcasted_iota(I32, (PICK_GROUP, PICK_GROUP // 2), 0)
        c = lax.broadcasted_iota(I32, (PICK_GROUP, PICK_GROUP // 2), 1)
        even = jnp.where(r == 2 * c, 1.0, 0.0).astype(BF16)
        acts = []
        for n in range(wu_sc.shape[1] // UP_CHUNK):
            cols = slice(n * UP_CHUNK, (n + 1) * UP_CHUNK)
            u = _dot(x, wu_sc[:, cols]) + bu_ref[0, :, cols]
            u_glu = jnp.minimum(u, SWIGLU_LIMIT)
            u_lin = jnp.clip(u, -SWIGLU_LIMIT, SWIGLU_LIMIT) + 1.0
            z = (u_glu * (1.0 / (1.0 + jnp.exp(-SWIGLU_ALPHA * u_glu)))
                 * pltpu.roll(u_lin, UP_CHUNK - 1, 1)).astype(BF16)
            for g in range(UP_CHUNK // PICK_GROUP):
                acts.append(_dot(z[:, g * PICK_GROUP:(g + 1) * PICK_GROUP], even).astype(BF16))
        _store_rows_tiled(o_ref, _dot(jnp.concatenate(acts, axis=1), wd_sc[...]) + bd_ref[0])

    @pl.when(jnp.logical_not(live))
    def _():
        o_ref[...] = jnp.zeros(o_ref.shape, F32)


def _experts(blk_e, pends, xs, w_up, b_up, w_down, b_down, blk):
    p = xs.shape[0] // ROW_TILE
    ne, d, f2 = w_up.shape
    f = w_down.shape[1]
    shift = blk.bit_length() - 1

    def live(b, be, pe):
        return jnp.minimum(b, (pe[N_EXPERTS - 1] >> shift) - 1)

    x_spec = pl.BlockSpec(_tiled_shape(blk, d), lambda b, be, pe: (live(b, be, pe), 0))
    w_spec = lambda r, c: pl.BlockSpec((1, r, c), lambda b, be, pe: (be[live(b, be, pe)], 0, 0))
    return pl.pallas_call(
        functools.partial(_expert_kernel, blk=blk),
        grid_spec=pltpu.PrefetchScalarGridSpec(
            num_scalar_prefetch=2,
            grid=(p // blk,),
            in_specs=[x_spec, pl.BlockSpec(memory_space=pl.ANY), w_spec(1, f2),
                      pl.BlockSpec(memory_space=pl.ANY), w_spec(1, d)],
            out_specs=pl.BlockSpec(_tiled_shape(blk, d), lambda b, be, pe: (b, 0)),
            scratch_shapes=[pltpu.VMEM((2, d, f2), F32), pltpu.VMEM((2, f, d), F32),
                            pltpu.VMEM((d, f2), BF16), pltpu.VMEM((f, d), BF16),
                            pltpu.SemaphoreType.DMA((2, 2)), pltpu.SMEM((1,), I32)]),
        out_shape=jax.ShapeDtypeStruct(_tiled_shape(p, d), F32),
        compiler_params=_params("arbitrary"),
        name="experts",
    )(blk_e, pends, xs, w_up, b_up.reshape(ne, 1, f2), w_down, b_down.reshape(ne, 1, d))


def _combine_kernel(dest_ref, dest_next_ref, x1_ref, gate_ref, g2_ref, fg_ref, ys_ref, o_ref,
                    ybuf_ref, sems, *, tm):
    i = pl.program_id(0)
    slot = i & 1

    def gather(idx_ref, buf_slot, fn):
        def body(t, carry):
            for k in range(TOP_K):
                fn(pltpu.make_async_copy(_row_tile(ys_ref, idx_ref[k, t]),
                                         _row_tile(ybuf_ref.at[buf_slot, k], t), sems.at[buf_slot]), k)
            return carry
        lax.fori_loop(0, tm, body, 0, unroll=ROW_DMA_UNROLL)

    start = lambda cp, k: cp.start(priority=k % 2)
    pl.when(i == 0)(lambda: gather(dest_ref, 0, start))
    pl.when(i + 1 < pl.num_programs(0))(lambda: gather(dest_next_ref, 1 - slot, start))
    gather(dest_ref, slot, lambda cp, k: cp.wait())

    g = gate_ref[...]
    moe = g[:, 0:1] * _load_rows_tiled(ybuf_ref.at[slot, 0])
    for k in range(1, TOP_K):
        moe = moe + g[:, k:k + 1] * _load_rows_tiled(ybuf_ref.at[slot, k])
    xo = x1_ref[...] + g2_ref[0] * moe
    o_ref[...] = xo * lax.rsqrt(jnp.mean(xo * xo, axis=-1, keepdims=True) + NORM_EPS) * fg_ref[...]


def _combine(dest, x1, gates, g2, final_g, ys, s, tm):
    n, d = x1.shape
    nbb = s // tm
    bsz = n // s
    last = n // tm - 1
    return pl.pallas_call(
        functools.partial(_combine_kernel, tm=tm),
        grid=(n // tm,),
        in_specs=[pl.BlockSpec((TOP_K, tm), lambda i: (0, i), memory_space=pltpu.SMEM),
                  pl.BlockSpec((TOP_K, tm), lambda i: (0, jnp.minimum(i + 1, last)),
                               memory_space=pltpu.SMEM),
                  pl.BlockSpec((tm, d), lambda i: (i, 0)),
                  pl.BlockSpec((tm, LANES), lambda i: (i, 0)),
                  pl.BlockSpec((1, 1, d), lambda i: (i // nbb, 0, 0)),
                  pl.BlockSpec((1, d), lambda i: (0, 0)),
                  pl.BlockSpec(memory_space=pl.ANY)],
        out_specs=pl.BlockSpec((tm, d), lambda i: (i, 0)),
        out_shape=jax.ShapeDtypeStruct((n, d), F32),
        scratch_shapes=[pltpu.VMEM((2, TOP_K, *_tiled_shape(tm, d)), F32),
                        pltpu.SemaphoreType.DMA((2,))],
        compiler_params=_params("arbitrary"),
        name="combine",
    )(dest, dest, x1, gates, g2.reshape(bsz, 1, d), final_g.reshape(1, d), ys)


def kernel(x, c, w_ada, b_ada, ln1_g, w_in, b_fgate, b_merge, lam_q1, lam_k1, lam_q2, lam_k2,
           diff_norm_g, t5_bias, w_br_fox, w_br_diff, w_o, ln2_g, w_router, b_router,
           w_up, b_up, w_down, b_down, final_g):
    bsz, s, d = x.shape
    n = bsz * s
    t = min(512, s)
    t_scatter = min(1024, s)
    t_gather = min(512, s)
    assert s % t == 0 and w_ada.shape[0] == 1

    mod = _adaln_mod(c, w_ada[0], b_ada[0])
    sh1, sc1, g1, sh2, sc2, g2 = jnp.split(mod, 6, axis=-1)

    qtf, fqt, kpf, vtf, qtd, kd, vtd = _in_proj(x, sc1, sh1, ln1_g[0], w_in[0], b_fgate[0], t)
    y_fox = _fox_attention(qtf, fqt, kpf, vtf, t)
    lam_vecs = jnp.stack([lam_q1[0], lam_k1[0], lam_q2[0], lam_k2[0]])
    y_diff = _diff_attention(lam_vecs, qtd, kd, vtd, _bias_tiles(t5_bias, t), diff_norm_g[0], t)

    w_gl = w_in[0][:, w_in.shape[2] - 2 * d:]
    x1, h2, idx, rank, gates, counts = _post_attn(
        x.reshape(n, d), y_fox.reshape(n, FOX_W), y_diff.reshape(n, DIFF_W),
        (sc1, sh1, g1, sc2, sh2), ln1_g[0], ln2_g[0], b_merge[0], w_gl,
        w_br_fox[0], w_br_diff[0], w_o[0], w_router[0], b_router[0], s, t)

    dest, blk_e, pends, nblk = _route(counts[:, 0].astype(I32), idx, rank, EXPERT_BLOCK)
    xs = _dispatch(pends, dest, h2, t_scatter, EXPERT_BLOCK, nblk)
    ys = _experts(blk_e, pends, xs, w_up[0], b_up[0], w_down[0], b_down[0], EXPERT_BLOCK)
    out = _combine(dest, x1, gates, g2, final_g, ys, s, t_gather)
    return out.reshape(bsz, s, d)
```

```python
import functools
import math

import numpy as np
import jax
import jax.numpy as jnp
from jax import lax
from jax.experimental import pallas as pl
from jax.experimental.pallas import tpu as pltpu

F32 = jnp.float32
BF16 = jnp.bfloat16
I32 = jnp.int32

HEAD_DIM = 64
FOX_HEADS = 8
DIFF_HEADS = 4
FOX_W = FOX_HEADS * HEAD_DIM
DIFF_W = DIFF_HEADS * 2 * HEAD_DIM
T5_BUCKETS = 32
T5_MAX_DIST = 128
N_EXPERTS = 32
TOP_K = 4
EXPERT_BLOCK = 512
SWIGLU_ALPHA = 1.702
SWIGLU_LIMIT = 7.0
NORM_EPS = 1e-6
SUBLN_EPS = 1e-5
LAMBDA_INIT = 0.8 - 0.6 * math.exp(-0.3 * 0)

LANES = 128
NEG = -1e30
LOG2E = math.log2(math.e)
ONES_ROWS = 16
VMEM_LIMIT = 56 * 1024 * 1024

NT_DIMS = (((1,), (1,)), ((), ()))


def _dot(a, b):
    return jnp.dot(a, b, preferred_element_type=F32)


def _dot_nt(a, b):
    return lax.dot_general(a, b, NT_DIMS, preferred_element_type=F32)


def _split2(v):
    hi = v.astype(BF16)
    return hi, (v - hi.astype(F32)).astype(BF16)


def _dot_split(a, b, dot=_dot):
    a_hi, a_lo = _split2(a)
    b_hi, b_lo = _split2(b)
    return dot(a_hi, b_hi) + (dot(a_hi, b_lo) + dot(a_lo, b_hi))


def _params(*sem):
    return pltpu.CompilerParams(dimension_semantics=sem, vmem_limit_bytes=VMEM_LIMIT)


def _modulated_rmsnorm(x, g, scale, shift):
    y = x * lax.rsqrt(jnp.mean(x * x, axis=-1, keepdims=True) + NORM_EPS)
    return (y * g) * (1.0 + scale) + shift


ROW_TILE = 8
ROW_PARTS = 1


def _tiled_shape(n, d):
    assert d == ROW_TILE * LANES
    return (n * ROW_TILE, LANES)


def _row_tile(ref, r):
    return ref.at[pl.ds(pl.multiple_of(r * ROW_TILE, ROW_TILE), ROW_TILE)]


def _store_rows_tiled(ref, rows):
    n = rows.shape[0]
    for a in range(ROW_TILE):
        ref[pl.ds(a, n, stride=ROW_TILE), :] = rows[:, a * LANES:(a + 1) * LANES]


def _load_rows_tiled(ref):
    n = ref.shape[0] // ROW_TILE
    return jnp.concatenate([ref[pl.ds(a, n, stride=ROW_TILE), :] for a in range(ROW_TILE)], axis=1)


def _split3(v):
    hi = v.astype(BF16)
    r1 = v - hi.astype(F32)
    mid = r1.astype(BF16)
    lo = (r1 - mid.astype(F32)).astype(BF16)
    return hi, mid, lo


def _adaln_kernel(c_ref, w_ref, b_ref, o_ref):
    c = c_ref[...]
    s = c / (1.0 + jnp.exp(-c))
    o_ref[...] = _dot_split(s, w_ref[...]) + b_ref[...]


def _adaln_mod(c, w, b):
    bsz, d = c.shape
    n = w.shape[1]
    rows = 8
    tn = 1536
    c8 = jnp.zeros((rows, d), F32).at[:bsz].set(c)
    out = pl.pallas_call(
        _adaln_kernel,
        grid=(n // tn,),
        in_specs=[pl.BlockSpec((rows, d), lambda j: (0, 0)),
                  pl.BlockSpec((d, tn), lambda j: (0, j)),
                  pl.BlockSpec((1, tn), lambda j: (0, j))],
        out_specs=pl.BlockSpec((rows, tn), lambda j: (0, j)),
        out_shape=jax.ShapeDtypeStruct((rows, n), F32),
        compiler_params=_params("arbitrary"),
        name="adaln_mod",
    )(c8, w, b.reshape(1, n))
    return out[:bsz]


def _bias_tile_kernel(tb_ref, o_ref, *, t):
    h = pl.program_id(0)
    key = lax.broadcasted_iota(I32, (t, t), 0)
    qry = lax.broadcasted_iota(I32, (t, t), 1)
    far = tb_ref[h, T5_BUCKETS - 1]
    max_exact = T5_BUCKETS // 2
    for which in range(2):
        rel = qry - key + which * t
        n = jnp.maximum(rel, 0)
        nf = jnp.maximum(n, max_exact).astype(F32)
        large = max_exact + (jnp.log(nf / max_exact) / math.log(T5_MAX_DIST / max_exact)
                             * (T5_BUCKETS - max_exact)).astype(I32)
        large = jnp.minimum(large, T5_BUCKETS - 1)
        bucket = jnp.where(n < max_exact, n, large)
        bias = jnp.zeros((t, t), F32)
        for b in range(T5_BUCKETS):
            bias = jnp.where(bucket == b, tb_ref[h, b] - far, bias)
        o_ref[0, which] = jnp.where(rel >= 0, bias * LOG2E, NEG)


def _bias_tiles(t5_bias, t):
    tb = t5_bias.T.astype(F32)
    return pl.pallas_call(
        functools.partial(_bias_tile_kernel, t=t),
        grid=(DIFF_HEADS,),
        in_specs=[pl.BlockSpec(memory_space=pltpu.SMEM)],
        out_specs=pl.BlockSpec((1, 2, t, t), lambda h: (h, 0, 0, 0)),
        out_shape=jax.ShapeDtypeStruct((DIFF_HEADS, 2, t, t), F32),
        compiler_params=_params("arbitrary"),
        name="bias_tiles",
    )(tb)


def _fparts_perm():
    npair = FOX_HEADS // 2
    pk = np.zeros((3 * LANES, npair * LANES), np.float32)
    pq = np.zeros((npair * 32, 3 * LANES), np.float32)
    k_ones = np.zeros((1, npair * LANES), np.float32)
    q_ones = np.zeros((npair * 32, 1), np.float32)
    for p in range(npair):
        for hh in range(2):
            head = 2 * p + hh
            for part in range(3):
                pk[part * LANES + head, p * LANES + hh * 16 + 3 + part] = -1.0
                k_ones[0, p * LANES + hh * 16 + part] = 1.0
                pq[p * 32 + hh * 16 + part, part * LANES + head] = 1.0
                q_ones[p * 32 + hh * 16 + 3 + part, 0] = 1.0
    return pk, pq, k_ones, q_ones


def _in_proj_kernel(x_ref, sc_ref, sh_ref, g_ref, wt_ref, wn_ref, wfl_ref, bf_ref,
                    pk_ref, pq_ref, kone_ref, qone_ref,
                    qtf_ref, fqt_ref, kpf_ref, vtf_ref, qtd_ref, kd_ref, vtd_ref,
                    carry_ref, *, tm):
    i = pl.program_id(1)

    @pl.when(i == 0)
    def _():
        carry_ref[...] = jnp.zeros_like(carry_ref)

    tp = tm // ROW_PARTS
    r = lax.broadcasted_iota(I32, (tp, tp), 0)
    c = lax.broadcasted_iota(I32, (tp, tp), 1)
    tri = jnp.where(c <= r, 1.0, 0.0).astype(BF16)
    wfl_hi, wfl_lo = _split2(wfl_ref[...])
    ones_rows = jnp.where(lax.broadcasted_iota(I32, (ONES_ROWS, tp), 0) == 0, 1.0, 0.0).astype(BF16)
    scale = HEAD_DIM ** -0.5 * LOG2E
    dv = 2 * HEAD_DIM
    carry = carry_ref[...]

    for part in range(ROW_PARTS):
        rows = slice(part * tp, (part + 1) * tp)
        h = _modulated_rmsnorm(x_ref[0, rows], g_ref[...], sc_ref[0], sh_ref[0])
        hb = h.astype(BF16)

        h_lo = (h - hb.astype(F32)).astype(BF16)
        fl = _dot(hb, wfl_hi) + (_dot(hb, wfl_lo) + _dot(h_lo, wfl_hi)) + bf_ref[...]
        logf = jnp.minimum(fl, 0.0) - jnp.log(1.0 + jnp.exp(-jnp.abs(fl)))
        psum = _dot(tri, jnp.concatenate(_split3(logf), axis=1))
        fsum = (psum[:, 0:LANES] + (psum[:, LANES:2 * LANES] + psum[:, 2 * LANES:3 * LANES])
                + carry)
        carry = fsum[tp - 1:tp, :]
        parts = jnp.concatenate(_split3(fsum * LOG2E), axis=1)

        yt = _dot_nt(wt_ref[...], hb)
        yn = _dot(hb, wn_ref[...])

        qtf_ref[0, 0, :, rows] = (yt[0:FOX_W] * scale).astype(BF16)
        vt = yt[FOX_W:2 * FOX_W].astype(BF16)
        for hd in range(FOX_HEADS):
            vtf_ref[0, hd, 0, 0:HEAD_DIM, rows] = vt[hd * HEAD_DIM:(hd + 1) * HEAD_DIM]
            vtf_ref[0, hd, 0, HEAD_DIM:HEAD_DIM + ONES_ROWS, rows] = ones_rows
        qtd_ref[0, 0, :, rows] = (yt[2 * FOX_W:2 * FOX_W + DIFF_W] * scale).astype(BF16)
        vtd = yt[2 * FOX_W + DIFF_W:2 * FOX_W + 2 * DIFF_W].astype(BF16)
        for hd in range(DIFF_HEADS):
            vtd_ref[0, hd, 0, 0:dv, rows] = vtd[hd * dv:(hd + 1) * dv]
            vtd_ref[0, hd, 0, dv:dv + ONES_ROWS, rows] = ones_rows

        fqt_ref[0, 0, :, rows] = (_dot_nt(pq_ref[...], parts) + qone_ref[...]).astype(BF16)
        kext = (_dot(parts, pk_ref[...]) + kone_ref[...]).astype(BF16)
        kf = yn[:, 0:FOX_W].astype(BF16)
        for p in range(FOX_HEADS // 2):
            kpf_ref[0, rows, 2 * p * LANES:(2 * p + 1) * LANES] = kf[:, p * LANES:(p + 1) * LANES]
            kpf_ref[0, rows, (2 * p + 1) * LANES:(2 * p + 2) * LANES] = (
                kext[:, p * LANES:(p + 1) * LANES])
        kd_ref[0, rows] = yn[:, FOX_W:FOX_W + DIFF_W].astype(BF16)
    carry_ref[...] = carry


def _in_proj(x, sc1, sh1, ln_g, w_in, b_fgate, t):
    bsz, s, d = x.shape
    nb = s // t
    cuts = np.cumsum([FOX_W, FOX_W, FOX_W, FOX_HEADS, DIFF_W, DIFF_W, DIFF_W])
    w_fq, w_fk, w_fv, w_fl, w_dq, w_dk, w_dv = [
        w_in[:, a:b] for a, b in zip([0, *cuts[:-1]], cuts)]
    wt = jnp.concatenate([w_fq, w_fv, w_dq, w_dv], axis=1).T.astype(BF16)
    wn = jnp.concatenate([w_fk, w_dk], axis=1).astype(BF16)
    wfl = jnp.zeros((d, LANES), F32).at[:, :FOX_HEADS].set(w_fl)
    bfl = jnp.zeros((1, LANES), F32).at[0, :FOX_HEADS].set(b_fgate)
    pk, pq, k_ones, q_ones = _fparts_perm()
    npair = FOX_HEADS // 2

    full = lambda shape: pl.BlockSpec(shape, lambda b, i: (0,) * len(shape))
    outs = pl.pallas_call(
        functools.partial(_in_proj_kernel, tm=t),
        grid=(bsz, nb),
        in_specs=[pl.BlockSpec((1, t, d), lambda b, i: (b, i, 0)),
                  pl.BlockSpec((1, 1, d), lambda b, i: (b, 0, 0)),
                  pl.BlockSpec((1, 1, d), lambda b, i: (b, 0, 0)),
                  full((1, d)), full(wt.shape), full(wn.shape), full(wfl.shape), full(bfl.shape),
                  full(pk.shape), full(pq.shape), full(k_ones.shape), full(q_ones.shape)],
        out_specs=[pl.BlockSpec((1, 1, FOX_W, t), lambda b, i: (b, i, 0, 0)),
                   pl.BlockSpec((1, 1, npair * 32, t), lambda b, i: (b, i, 0, 0)),
                   pl.BlockSpec((1, t, npair * 2 * LANES), lambda b, i: (b, i, 0)),
                   pl.BlockSpec((1, FOX_HEADS, 1, HEAD_DIM + ONES_ROWS, t),
                                lambda b, i: (b, 0, i, 0, 0)),
                   pl.BlockSpec((1, 1, DIFF_W, t), lambda b, i: (b, i, 0, 0)),
                   pl.BlockSpec((1, t, DIFF_W), lambda b, i: (b, i, 0)),
                   pl.BlockSpec((1, DIFF_HEADS, 1, 2 * HEAD_DIM + ONES_ROWS, t),
                                lambda b, i: (b, 0, i, 0, 0))],
        out_shape=[jax.ShapeDtypeStruct((bsz, nb, FOX_W, t), BF16),
                   jax.ShapeDtypeStruct((bsz, nb, npair * 32, t), BF16),
                   jax.ShapeDtypeStruct((bsz, s, npair * 2 * LANES), BF16),
                   jax.ShapeDtypeStruct((bsz, FOX_HEADS, nb, HEAD_DIM + ONES_ROWS, t), BF16),
                   jax.ShapeDtypeStruct((bsz, nb, DIFF_W, t), BF16),
                   jax.ShapeDtypeStruct((bsz, s, DIFF_W), BF16),
                   jax.ShapeDtypeStruct((bsz, DIFF_HEADS, nb, 2 * HEAD_DIM + ONES_ROWS, t), BF16)],
        scratch_shapes=[pltpu.VMEM((1, LANES), F32)],
        compiler_params=_params("arbitrary", "arbitrary"),
        name="in_proj",
    )(x, sc1.reshape(bsz, 1, d), sh1.reshape(bsz, 1, d), ln_g.reshape(1, d), wt, wn, wfl, bfl,
      jnp.asarray(pk, BF16), jnp.asarray(pq, BF16), jnp.asarray(k_ones), jnp.asarray(q_ones))
    return outs


SCORE_EXTRA_ROWS = 8


def _store_scores(s_ref, s, t):
    s_ref[0:t, :] = s
    s_ref[t:t + 1, :] = jnp.max(s, axis=0, keepdims=True)


def _softmax_block(s, s_max, vt, m_ref, acc_ref, slot):
    m_old = m_ref[slot]
    m_new = jnp.maximum(m_old, s_max)
    alpha = jnp.exp2(m_old - m_new)
    p = jnp.exp2(s - m_new)
    acc_ref[slot] = alpha * acc_ref[slot] + _dot(vt, p.astype(BF16))
    m_ref[slot] = m_new


def _init_softmax_state(m_ref, acc_ref):
    m_ref[...] = jnp.full(m_ref.shape, NEG, F32)
    acc_ref[...] = jnp.zeros(acc_ref.shape, F32)


SCAN_BLOCKS_PER_TRIP = 4


def _attend_all_query_blocks(nb, n_tail, prepare_q, qk_into, process, begin, finish, bufs):
    ring, first_buf = bufs[:2], bufs[2]
    prepare_q(0, 0)
    qk_into(first_buf, 0, 0)

    def query_block(i, carry):
        slot = i & 1

        def start_next_query_block():
            prepare_q(jnp.minimum(i + 1, nb - 1), 1 - slot)
            qk_into(first_buf, 1 - slot, 0)

        def run(items, ends):
            for n, (buf, j, kind) in enumerate(items):
                last = n + 1 == len(items)
                if not last:
                    qk_into(items[n + 1][0], slot, items[n + 1][1])
                elif ends and buf is not first_buf:
                    start_next_query_block()
                process(buf, i, j, kind)
                if last and ends and buf is first_buf:
                    start_next_query_block()

        begin()
        for small in range(n_tail):
            items = [(first_buf, 0, small)] + [(ring[n % 2], n + 1, small - 1 - n) for n in range(small)]
            pl.when(i == small)(functools.partial(run, items, True))

        @pl.when(i >= n_tail)
        def _():
            n_plain = i - n_tail
            n_trips = n_plain // SCAN_BLOCKS_PER_TRIP
            rem = n_plain % SCAN_BLOCKS_PER_TRIP
            qk_into(ring[0], slot, 1)
            process(first_buf, i, 0, None)

            def trip(n, carry):
                j = 1 + SCAN_BLOCKS_PER_TRIP * n
                for m in range(SCAN_BLOCKS_PER_TRIP):
                    qk_into(ring[(m + 1) % 2], slot, j + m + 1)
                    process(ring[m % 2], i, j + m, None)
                return carry

            lax.fori_loop(0, n_trips, trip, 0)
            first = 1 + SCAN_BLOCKS_PER_TRIP * n_trips
            for r in range(SCAN_BLOCKS_PER_TRIP):
                kinds = [None] * r + list(range(n_tail - 1, -1, -1))
                items = [(ring[n % 2], first + n, kind) for n, kind in enumerate(kinds)]
                pl.when(rem == r)(functools.partial(run, items, True))

        finish(i)
        return carry

    lax.fori_loop(0, nb, query_block, 0)


def _fox_kernel(q_ref, fq_ref, k_ref, v_ref, o_ref, qbd_ref, sa_ref, sb_ref, sc_ref, m_ref, acc_ref,
                *, t, nb):
    d = HEAD_DIM
    qbd_ref[...] = jnp.zeros(qbd_ref.shape, BF16)

    def prepare_q(i, slot):
        qbd_ref[slot, 0:d, 0:t] = q_ref[0, i, 0:d, :]
        qbd_ref[slot, d:2 * d, t:2 * t] = q_ref[0, i, d:2 * d, :]
        qbd_ref[slot, 2 * d:2 * d + 16, 0:t] = fq_ref[0, i, 0:16, :]
        qbd_ref[slot, 2 * d + 16:2 * d + 32, t:2 * t] = fq_ref[0, i, 16:32, :]

    def qk_into(s_ref, slot, j):
        kblk = k_ref[0, pl.ds(pl.multiple_of(j * t, t), t), :]
        _store_scores(s_ref, _dot(kblk, qbd_ref[slot]), t)

    def process(s_ref, i, j, kind):
        for hh in range(2):
            sh = s_ref[0:t, hh * t:(hh + 1) * t]
            if kind == 0:
                key = lax.broadcasted_iota(I32, (t, t), 0)
                qry = lax.broadcasted_iota(I32, (t, t), 1)
                sh = jnp.where(key <= qry, sh, NEG)
                s_max = jnp.max(sh, axis=0, keepdims=True)
            else:
                s_max = s_ref[t:t + 1, hh * t:(hh + 1) * t]
            _softmax_block(sh, s_max, v_ref[0, hh, j], m_ref, acc_ref, hh)

    def finish(i):
        outs = [acc_ref[hh, 0:d] / acc_ref[hh, d:d + 1] for hh in range(2)]
        o_ref[0, pl.ds(pl.multiple_of(i * t, t), t), :] = (
            jnp.concatenate(outs, axis=0).T.astype(BF16))

    _attend_all_query_blocks(nb, 1, prepare_q, qk_into, process,
                             functools.partial(_init_softmax_state, m_ref, acc_ref), finish,
                             (sa_ref, sb_ref, sc_ref))


def _fox_attention(qt, fqt, kp, vt, t):
    bsz, nb = qt.shape[:2]
    s = nb * t
    npair = FOX_HEADS // 2
    return pl.pallas_call(
        functools.partial(_fox_kernel, t=t, nb=nb),
        grid=(bsz, npair),
        in_specs=[pl.BlockSpec((1, nb, 2 * HEAD_DIM, t), lambda b, p: (b, 0, p, 0)),
                  pl.BlockSpec((1, nb, 32, t), lambda b, p: (b, 0, p, 0)),
                  pl.BlockSpec((1, s, 2 * LANES), lambda b, p: (b, 0, p)),
                  pl.BlockSpec((1, 2, nb, HEAD_DIM + ONES_ROWS, t), lambda b, p: (b, p, 0, 0, 0))],
        out_specs=pl.BlockSpec((1, s, 2 * HEAD_DIM), lambda b, p: (b, 0, p)),
        out_shape=jax.ShapeDtypeStruct((bsz, s, FOX_W), BF16),
        scratch_shapes=[pltpu.VMEM((2, 2 * LANES, 2 * t), BF16)]
        + [pltpu.VMEM((t + SCORE_EXTRA_ROWS, 2 * t), F32)] * 3
        + [pltpu.VMEM((2, 1, t), F32), pltpu.VMEM((2, HEAD_DIM + ONES_ROWS, t), F32)],
        compiler_params=_params("arbitrary", "arbitrary"),
        name="fox_attn",
    )(qt, fqt, kp, vt)


def _diff_kernel(lam_ref, q_ref, k_ref, v_ref, bias_ref, g_ref, o_ref,
                 qbd_ref, sa_ref, sb_ref, sc_ref, m_ref, acc_ref, *, t, nb):
    d = HEAD_DIM
    dv = 2 * HEAD_DIM
    qbd_ref[...] = jnp.zeros(qbd_ref.shape, BF16)

    def prepare_q(i, slot):
        qbd_ref[slot, 0:d, 0:t] = q_ref[0, i, 0:d, :]
        qbd_ref[slot, d:2 * d, t:2 * t] = q_ref[0, i, d:2 * d, :]

    def qk_into(s_ref, slot, j):
        kblk = k_ref[0, pl.ds(pl.multiple_of(j * t, t), t), :]
        _store_scores(s_ref, _dot(kblk, qbd_ref[slot]), t)

    def process(s_ref, i, j, kind):
        vt = v_ref[0, 0, j]
        for hh in range(2):
            sh = s_ref[0:t, hh * t:(hh + 1) * t]
            if kind is not None:
                sh = sh + bias_ref[0, kind]
                s_max = jnp.max(sh, axis=0, keepdims=True)
            else:
                s_max = s_ref[t:t + 1, hh * t:(hh + 1) * t]
            _softmax_block(sh, s_max, vt, m_ref, acc_ref, hh)

    def finish(i):
        lam = (jnp.exp(jnp.sum(lam_ref[0:1, :] * lam_ref[1:2, :], axis=1, keepdims=True))
               - jnp.exp(jnp.sum(lam_ref[2:3, :] * lam_ref[3:4, :], axis=1, keepdims=True))
               + LAMBDA_INIT)
        out = (acc_ref[0, 0:dv] / acc_ref[0, dv:dv + 1]
               - lam * (acc_ref[1, 0:dv] / acc_ref[1, dv:dv + 1]))
        ms = jnp.mean(out * out, axis=0, keepdims=True)
        out = out * lax.rsqrt(ms + SUBLN_EPS) * g_ref[...] * (1.0 - LAMBDA_INIT)
        o_ref[0, pl.ds(pl.multiple_of(i * t, t), t), :] = out.T.astype(BF16)

    _attend_all_query_blocks(nb, 2, prepare_q, qk_into, process,
                             functools.partial(_init_softmax_state, m_ref, acc_ref), finish,
                             (sa_ref, sb_ref, sc_ref))


def _diff_attention(lam_vecs, qt, k, vt, bias_tiles, norm_g, t):
    bsz, nb = qt.shape[:2]
    s = nb * t
    dv = 2 * HEAD_DIM
    return pl.pallas_call(
        functools.partial(_diff_kernel, t=t, nb=nb),
        grid=(bsz, DIFF_HEADS),
        in_specs=[pl.BlockSpec((4, HEAD_DIM), lambda b, h: (0, 0)),
                  pl.BlockSpec((1, nb, dv, t), lambda b, h: (b, 0, h, 0)),
                  pl.BlockSpec((1, s, dv), lambda b, h: (b, 0, h)),
                  pl.BlockSpec((1, 1, nb, dv + ONES_ROWS, t), lambda b, h: (b, h, 0, 0, 0)),
                  pl.BlockSpec((1, 2, t, t), lambda b, h: (h, 0, 0, 0)),
                  pl.BlockSpec((dv, 1), lambda b, h: (0, 0))],
        out_specs=pl.BlockSpec((1, s, dv), lambda b, h: (b, 0, h)),
        out_shape=jax.ShapeDtypeStruct((bsz, s, DIFF_W), BF16),
        scratch_shapes=[pltpu.VMEM((2, dv, 2 * t), BF16)]
        + [pltpu.VMEM((t + SCORE_EXTRA_ROWS, 2 * t), F32)] * 3
        + [pltpu.VMEM((2, 1, t), F32), pltpu.VMEM((2, dv + ONES_ROWS, t), F32)],
        compiler_params=_params("arbitrary", "arbitrary"),
        name="diff_attn",
    )(lam_vecs, qt, k, vt, bias_tiles, norm_g.reshape(dv, 1))


def _post_attn_kernel(x_ref, yf_ref, yd_ref, sc1_ref, sh1_ref, g1_ref, sc2_ref, sh2_ref,
                      ln1_ref, ln2_ref, wgl_ref, bm_ref, wbf_ref, wbd_ref, wo_ref, wr_ref, br_ref,
                      x1_ref, h2_ref, idx_ref, rank_ref, gate_ref, cnt_ref, carry_ref, *, tm):
    i = pl.program_id(0)

    @pl.when(i == 0)
    def _():
        carry_ref[...] = jnp.zeros_like(carry_ref)

    d = x_ref.shape[1]
    tp = tm // ROW_PARTS
    logit_parts = []
    for part in range(ROW_PARTS):
        rows = slice(part * tp, (part + 1) * tp)
        x = x_ref[rows]
        h1 = _modulated_rmsnorm(x, ln1_ref[...], sc1_ref[0], sh1_ref[0]).astype(BF16)
        gl = _dot(h1, wgl_ref[...]) + bm_ref[...]
        gates = 1.0 / (1.0 + jnp.exp(-gl))
        merged = (gates[:, :d] * _dot(yf_ref[rows], wbf_ref[...])
                  + gates[:, d:] * _dot(yd_ref[rows], wbd_ref[...]))
        x1 = x + g1_ref[0] * _dot(merged.astype(BF16), wo_ref[...])
        x1_ref[rows] = x1
        h2 = _modulated_rmsnorm(x1, ln2_ref[...], sc2_ref[0], sh2_ref[0])
        _store_rows_tiled(h2_ref.at[pl.ds(part * tp * ROW_TILE, tp * ROW_TILE)], h2)
        logit_parts.append(_dot_split(wr_ref[...], h2, _dot_nt))
    logits = jnp.concatenate(logit_parts, axis=1) + br_ref[...]
    row = lax.broadcasted_iota(I32, (N_EXPERTS, tm), 0)
    cur = logits
    vals, idxs = [], []
    for _ in range(TOP_K):
        mx = jnp.max(cur, axis=0, keepdims=True)
        ik = jnp.min(jnp.where(cur == mx, row, N_EXPERTS), axis=0, keepdims=True)
        vals.append(mx)
        idxs.append(ik)
        cur = jnp.where(row == ik, -jnp.inf, cur)
    exps = [jnp.exp(v - vals[0]) for v in vals]
    denom = exps[0] + exps[1] + exps[2] + exps[3]
    gate_rows = [e / denom for e in exps]

    onehots = [row == ik for ik in idxs]
    cnt = jnp.zeros((N_EXPERTS, tm), F32)
    for oh in onehots:
        cnt = cnt + jnp.where(oh, 1.0, 0.0)
    r = lax.broadcasted_iota(I32, (tm, tm), 0)
    c = lax.broadcasted_iota(I32, (tm, tm), 1)
    before = jnp.where(r < c, 1.0, 0.0).astype(BF16)
    prior = carry_ref[:, 0:1] + _dot(cnt.astype(BF16), before)
    for k in range(TOP_K):
        rk = jnp.sum(jnp.where(onehots[k], prior, 0.0), axis=0, keepdims=True)
        rank_ref[k:k + 1, :] = rk.astype(I32)
        idx_ref[k:k + 1, :] = idxs[k]
    carry_ref[...] = carry_ref[...] + jnp.sum(cnt, axis=1, keepdims=True)
    cnt_ref[...] = carry_ref[...]

    lrow = lax.broadcasted_iota(I32, (LANES, tm), 0)
    g_t = jnp.zeros((LANES, tm), F32)
    for k in range(TOP_K):
        g_t = jnp.where(lrow == k, gate_rows[k], g_t)
    gate_ref[...] = g_t.T


def _post_attn(x2, yf, yd, mods, ln1_g, ln2_g, b_merge, w_gl, w_brf, w_brd, w_o, w_r, b_r, s, tm):
    n, d = x2.shape
    nbb = s // tm
    bsz = n // s
    sc1, sh1, g1, sc2, sh2 = [m.reshape(bsz, 1, d) for m in mods]
    row_spec = lambda w: pl.BlockSpec((tm, w), lambda i: (i, 0))
    mod_spec = pl.BlockSpec((1, 1, d), lambda i: (i // nbb, 0, 0))
    full = lambda shape: pl.BlockSpec(shape, lambda i: (0,) * len(shape))
    tok_spec = pl.BlockSpec((TOP_K, tm), lambda i: (0, i))
    return pl.pallas_call(
        functools.partial(_post_attn_kernel, tm=tm),
        grid=(n // tm,),
        in_specs=[row_spec(d), row_spec(FOX_W), row_spec(DIFF_W),
                  mod_spec, mod_spec, mod_spec, mod_spec, mod_spec,
                  full((1, d)), full((1, d)), full(w_gl.shape), full((1, 2 * d)),
                  full(w_brf.shape), full(w_brd.shape), full(w_o.shape),
                  full((N_EXPERTS, d)), full((N_EXPERTS, 1))],
        out_specs=[row_spec(d), pl.BlockSpec(_tiled_shape(tm, d), lambda i: (i, 0)),
                   tok_spec, tok_spec, row_spec(LANES), full((N_EXPERTS, LANES))],
        out_shape=[jax.ShapeDtypeStruct((n, d), F32), jax.ShapeDtypeStruct(_tiled_shape(n, d), F32),
                   jax.ShapeDtypeStruct((TOP_K, n), I32), jax.ShapeDtypeStruct((TOP_K, n), I32),
                   jax.ShapeDtypeStruct((n, LANES), F32),
                   jax.ShapeDtypeStruct((N_EXPERTS, LANES), F32)],
        scratch_shapes=[pltpu.VMEM((N_EXPERTS, LANES), F32)],
        compiler_params=_params("arbitrary"),
        name="post_attn",
    )(x2, yf, yd, sc1, sh1, g1, sc2, sh2, ln1_g.reshape(1, d), ln2_g.reshape(1, d),
      w_gl.astype(BF16), b_merge.reshape(1, 2 * d), w_brf.astype(BF16), w_brd.astype(BF16),
      w_o.astype(BF16), w_r.T, b_r.reshape(N_EXPERTS, 1))


def _route_kernel(cnt_ref, idx_ref, rank_ref, dest_ref, be_ref, pend_ref, *, blk, nblk_lanes):
    shift = blk.bit_length() - 1
    idx = idx_ref[...]
    start_of = jnp.zeros(idx.shape, I32)
    blk_start = lax.broadcasted_iota(I32, (1, nblk_lanes), 1) * blk
    lane = lax.broadcasted_iota(I32, (1, LANES), 1)
    blk_e = jnp.zeros((1, nblk_lanes), I32)
    pends = jnp.zeros((1, LANES), I32)
    pend = jnp.int32(0)
    for e in range(N_EXPERTS):
        padded = ((cnt_ref[e] + (blk - 1)) >> shift) << shift
        start_of = jnp.where(idx == e, pend, start_of)
        pend = pend + padded
        blk_e = blk_e + jnp.where(blk_start >= pend, 1, 0)
        pends = jnp.where(lane == e, pend, pends)
    dest_ref[...] = start_of + rank_ref[...]
    be_ref[...] = jnp.minimum(blk_e, N_EXPERTS - 1)
    pend_ref[...] = pends


def _route(counts, idx, rank, blk):
    n = idx.shape[1]
    nblk = (n * TOP_K + blk - 1) // blk + N_EXPERTS
    nblk_lanes = -(-nblk // LANES) * LANES
    dest, blk_e, pends = pl.pallas_call(
        functools.partial(_route_kernel, blk=blk, nblk_lanes=nblk_lanes),
        in_specs=[pl.BlockSpec(memory_space=pltpu.SMEM),
                  pl.BlockSpec(memory_space=pltpu.VMEM),
                  pl.BlockSpec(memory_space=pltpu.VMEM)],
        out_specs=[pl.BlockSpec(memory_space=pltpu.VMEM)] * 3,
        out_shape=[jax.ShapeDtypeStruct((TOP_K, n), I32),
                   jax.ShapeDtypeStruct((1, nblk_lanes), I32),
                   jax.ShapeDtypeStruct((1, LANES), I32)],
        compiler_params=pltpu.CompilerParams(vmem_limit_bytes=VMEM_LIMIT),
        name="route",
    )(counts, idx, rank)
    return dest, blk_e[0, :nblk], pends[0, :N_EXPERTS], nblk


def _dispatch_kernel(pend_ref, dest_ref, h_ref, xs_ref, zero_ref, sem, *, tm, blk, nblk):
    shift = blk.bit_length() - 1

    @pl.when(pl.program_id(0) == 0)
    def _():
        zero_ref[...] = jnp.zeros(zero_ref.shape, F32)

        def zero_block(b):
            lines = blk * ROW_TILE
            return pltpu.make_async_copy(
                zero_ref, xs_ref.at[pl.ds(pl.multiple_of(b * lines, lines), lines)], sem)

        def tails(fn):
            for e in range(N_EXPERTS):
                prev = pend_ref[e - 1] if e else 0

                @pl.when(pend_ref[e] > prev)
                def _():
                    fn(zero_block((pend_ref[e] >> shift) - 1))

        def rest(fn):
            def body(b, carry):
                fn(zero_block(b))
                return carry
            lax.fori_loop(pend_ref[N_EXPERTS - 1] >> shift, nblk, body, 0)

        tails(lambda cp: cp.start())
        rest(lambda cp: cp.start())
        tails(lambda cp: cp.wait())
        rest(lambda cp: cp.wait())

    def row_copy(t, k):
        return pltpu.make_async_copy(_row_tile(h_ref, t), _row_tile(xs_ref, dest_ref[k, t]), sem)

    def start(t, carry):
        for k in range(TOP_K):
            row_copy(t, k).start(priority=k % 2)
        return carry

    def wait(t, carry):
        for k in range(TOP_K):
            row_copy(t, k).wait()
        return carry

    lax.fori_loop(0, tm, start, 0, unroll=ROW_DMA_UNROLL)
    lax.fori_loop(0, tm, wait, 0, unroll=ROW_DMA_UNROLL)


def _dispatch(pends, dest, h2, tm, blk, nblk):
    n = h2.shape[0] // ROW_TILE
    d = ROW_TILE * LANES
    return pl.pallas_call(
        functools.partial(_dispatch_kernel, tm=tm, blk=blk, nblk=nblk),
        grid=(n // tm,),
        in_specs=[pl.BlockSpec(memory_space=pltpu.SMEM),
                  pl.BlockSpec((TOP_K, tm), lambda i: (0, i), memory_space=pltpu.SMEM),
                  pl.BlockSpec(_tiled_shape(tm, d), lambda i: (i, 0))],
        out_specs=pl.BlockSpec(memory_space=pl.ANY),
        out_shape=jax.ShapeDtypeStruct(_tiled_shape(nblk * blk, d), F32),
        scratch_shapes=[pltpu.VMEM(_tiled_shape(blk, d), F32), pltpu.SemaphoreType.DMA(())],
        compiler_params=_params("arbitrary"),
        name="dispatch",
    )(pends, dest, h2)


ROW_DMA_UNROLL = 4
UP_CHUNK = 2048
PICK_GROUP = 512


def _expert_kernel(be_ref, pend_ref, x_ref, wu_hbm, bu_ref, wd_hbm, bd_ref, o_ref,
                   wu_f32, wd_f32, wu_sc, wd_sc, sems, nseen_ref, *, blk):
    shift = blk.bit_length() - 1
    b = pl.program_id(0)
    n_live = pend_ref[N_EXPERTS - 1] >> shift
    live = b < n_live
    expert = be_ref[b]
    new_expert = (b == 0) | (expert != be_ref[jnp.maximum(b - 1, 0)])

    def fetch(e, slot):
        return (pltpu.make_async_copy(wu_hbm.at[e], wu_f32.at[slot], sems.at[0, slot]),
                pltpu.make_async_copy(wd_hbm.at[e], wd_f32.at[slot], sems.at[1, slot]))

    @pl.when(b == 0)
    def _():
        nseen_ref[0] = 0
        for cp in fetch(expert, 0):
            cp.start()

    @pl.when(live & new_expert)
    def _():
        slot = nseen_ref[0] & 1
        nxt = pend_ref[expert] >> shift

        @pl.when(nxt < n_live)
        def _():
            for cp in fetch(be_ref[nxt], 1 - slot):
                cp.start()

        for cp in fetch(expert, slot):
            cp.wait()
        wu_sc[...] = wu_f32[slot].astype(BF16)
        wd_sc[...] = wd_f32[slot].astype(BF16)
        nseen_ref[0] = nseen_ref[0] + 1

    @pl.when(live)
    def _():
        x = _load_rows_tiled(x_ref).astype(BF16)
        r = lax.broadcasted_iota(I32, (PICK_GROUP, PICK_GROUP // 2), 0)
        c = lax.broadcasted_iota(I32, (PICK_GROUP, PICK_GROUP // 2), 1)
        even = jnp.where(r == 2 * c, 1.0, 0.0).astype(BF16)
        acts = []
        for n in range(wu_sc.shape[1] // UP_CHUNK):
            cols = slice(n * UP_CHUNK, (n + 1) * UP_CHUNK)
            u = _dot(x, wu_sc[:, cols]) + bu_ref[0, :, cols]
            u_glu = jnp.minimum(u, SWIGLU_LIMIT)
            u_lin = jnp.clip(u, -SWIGLU_LIMIT, SWIGLU_LIMIT) + 1.0
            z = (u_glu * (1.0 / (1.0 + jnp.exp(-SWIGLU_ALPHA * u_glu)))
                 * pltpu.roll(u_lin, UP_CHUNK - 1, 1)).astype(BF16)
            for g in range(UP_CHUNK // PICK_GROUP):
                acts.append(_dot(z[:, g * PICK_GROUP:(g + 1) * PICK_GROUP], even).astype(BF16))
        _store_rows_tiled(o_ref, _dot(jnp.concatenate(acts, axis=1), wd_sc[...]) + bd_ref[0])

    @pl.when(jnp.logical_not(live))
    def _():
        o_ref[...] = jnp.zeros(o_ref.shape, F32)


def _experts(blk_e, pends, xs, w_up, b_up, w_down, b_down, blk):
    p = xs.shape[0] // ROW_TILE
    ne, d, f2 = w_up.shape
    f = w_down.shape[1]
    shift = blk.bit_length() - 1

    def live(b, be, pe):
        return jnp.minimum(b, (pe[N_EXPERTS - 1] >> shift) - 1)

    x_spec = pl.BlockSpec(_tiled_shape(blk, d), lambda b, be, pe: (live(b, be, pe), 0))
    w_spec = lambda r, c: pl.BlockSpec((1, r, c), lambda b, be, pe: (be[live(b, be, pe)], 0, 0))
    return pl.pallas_call(
        functools.partial(_expert_kernel, blk=blk),
        grid_spec=pltpu.PrefetchScalarGridSpec(
            num_scalar_prefetch=2,
            grid=(p // blk,),
            in_specs=[x_spec, pl.BlockSpec(memory_space=pl.ANY), w_spec(1, f2),
                      pl.BlockSpec(memory_space=pl.ANY), w_spec(1, d)],
            out_specs=pl.BlockSpec(_tiled_shape(blk, d), lambda b, be, pe: (b, 0)),
            scratch_shapes=[pltpu.VMEM((2, d, f2), F32), pltpu.VMEM((2, f, d), F32),
                            pltpu.VMEM((d, f2), BF16), pltpu.VMEM((f, d), BF16),
                            pltpu.SemaphoreType.DMA((2, 2)), pltpu.SMEM((1,), I32)]),
        out_shape=jax.ShapeDtypeStruct(_tiled_shape(p, d), F32),
        compiler_params=_params("arbitrary"),
        name="experts",
    )(blk_e, pends, xs, w_up, b_up.reshape(ne, 1, f2), w_down, b_down.reshape(ne, 1, d))


def _combine_kernel(dest_ref, dest_next_ref, x1_ref, gate_ref, g2_ref, fg_ref, ys_ref, o_ref,
                    ybuf_ref, sems, *, tm):
    i = pl.program_id(0)
    slot = i & 1

    def gather(idx_ref, buf_slot, fn):
        def body(t, carry):
            for k in range(TOP_K):
                fn(pltpu.make_async_copy(_row_tile(ys_ref, idx_ref[k, t]),
                                         _row_tile(ybuf_ref.at[buf_slot, k], t), sems.at[buf_slot]), k)
            return carry
        lax.fori_loop(0, tm, body, 0, unroll=ROW_DMA_UNROLL)

    start = lambda cp, k: cp.start(priority=k % 2)
    pl.when(i == 0)(lambda: gather(dest_ref, 0, start))
    pl.when(i + 1 < pl.num_programs(0))(lambda: gather(dest_next_ref, 1 - slot, start))
    gather(dest_ref, slot, lambda cp, k: cp.wait())

    g = gate_ref[...]
    moe = g[:, 0:1] * _load_rows_tiled(ybuf_ref.at[slot, 0])
    for k in range(1, TOP_K):
        moe = moe + g[:, k:k + 1] * _load_rows_tiled(ybuf_ref.at[slot, k])
    xo = x1_ref[...] + g2_ref[0] * moe
    o_ref[...] = xo * lax.rsqrt(jnp.mean(xo * xo, axis=-1, keepdims=True) + NORM_EPS) * fg_ref[...]


def _combine(dest, x1, gates, g2, final_g, ys, s, tm):
    n, d = x1.shape
    nbb = s // tm
    bsz = n // s
    last = n // tm - 1
    return pl.pallas_call(
        functools.partial(_combine_kernel, tm=tm),
        grid=(n // tm,),
        in_specs=[pl.BlockSpec((TOP_K, tm), lambda i: (0, i), memory_space=pltpu.SMEM),
                  pl.BlockSpec((TOP_K, tm), lambda i: (0, jnp.minimum(i + 1, last)),
                               memory_space=pltpu.SMEM),
                  pl.BlockSpec((tm, d), lambda i: (i, 0)),
                  pl.BlockSpec((tm, LANES), lambda i: (i, 0)),
                  pl.BlockSpec((1, 1, d), lambda i: (i // nbb, 0, 0)),
                  pl.BlockSpec((1, d), lambda i: (0, 0)),
                  pl.BlockSpec(memory_space=pl.ANY)],
        out_specs=pl.BlockSpec((tm, d), lambda i: (i, 0)),
        out_shape=jax.ShapeDtypeStruct((n, d), F32),
        scratch_shapes=[pltpu.VMEM((2, TOP_K, *_tiled_shape(tm, d)), F32),
                        pltpu.SemaphoreType.DMA((2,))],
        compiler_params=_params("arbitrary"),
        name="combine",
    )(dest, dest, x1, gates, g2.reshape(bsz, 1, d), final_g.reshape(1, d), ys)


def kernel(x, c, w_ada, b_ada, ln1_g, w_in, b_fgate, b_merge, lam_q1, lam_k1, lam_q2, lam_k2,
           diff_norm_g, t5_bias, w_br_fox, w_br_diff, w_o, ln2_g, w_router, b_router,
           w_up, b_up, w_down, b_down, final_g):
    bsz, s, d = x.shape
    n = bsz * s
    t = min(512, s)
    t_scatter = min(512, s)
    t_gather = min(256, s)
    assert s % t == 0 and w_ada.shape[0] == 1

    mod = _adaln_mod(c, w_ada[0], b_ada[0])
    sh1, sc1, g1, sh2, sc2, g2 = jnp.split(mod, 6, axis=-1)

    qtf, fqt, kpf, vtf, qtd, kd, vtd = _in_proj(x, sc1, sh1, ln1_g[0], w_in[0], b_fgate[0], t)
    y_fox = _fox_attention(qtf, fqt, kpf, vtf, t)
    lam_vecs = jnp.stack([lam_q1[0], lam_k1[0], lam_q2[0], lam_k2[0]])
    y_diff = _diff_attention(lam_vecs, qtd, kd, vtd, _bias_tiles(t5_bias, t), diff_norm_g[0], t)

    w_gl = w_in[0][:, w_in.shape[2] - 2 * d:]
    x1, h2, idx, rank, gates, counts = _post_attn(
        x.reshape(n, d), y_fox.reshape(n, FOX_W), y_diff.reshape(n, DIFF_W),
        (sc1, sh1, g1, sc2, sh2), ln1_g[0], ln2_g[0], b_merge[0], w_gl,
        w_br_fox[0], w_br_diff[0], w_o[0], w_router[0], b_router[0], s, t)

    dest, blk_e, pends, nblk = _route(counts[:, 0].astype(I32), idx, rank, EXPERT_BLOCK)
    xs = _dispatch(pends, dest, h2, t_scatter, EXPERT_BLOCK, nblk)
    ys = _experts(blk_e, pends, xs, w_up[0], b_up[0], w_down[0], b_down[0], EXPERT_BLOCK)
    out = _combine(dest, x1, gates, g2, final_g, ys, s, t_gather)
    return out.reshape(bsz, s, d)
```
